```python
import functools
import jax, jax.numpy as jnp
from jax import lax
import numpy as np

D_MODEL = 1024
BATCH = 2
SEQ = 8192
DEPTH = 2
DEC_BATCH = 128
DEC_SEQ = 8
PAST_LEN = 8192
PAGE_SIZE = 128

D_MIX = D_MODEL
D_LRU = D_MIX // 4
LRU_BLOCKS = 4
LRU_BS = D_LRU // LRU_BLOCKS
LRU_CONV = 4
LRU_C = 8.0
D_CONF = D_MIX // 4
CONF_KERNEL = 31
MLA_HEADS = 4
MLA_NOPE = 64
MLA_ROPE = 32
MLA_V = 64
MLA_Q_LORA = 256
MLA_KV_LORA = 128
MLA_CACHE_W = MLA_KV_LORA + MLA_ROPE
ROPE_THETA = 10000.0
NSA_HEADS = 4
NSA_HD = 64
NSA_CMP_BLOCK = 32
NSA_CMP_STRIDE = 16
NSA_SEL_BLOCK = 64
NSA_N_SEL = 16
NSA_WINDOW = 512
NSA_FORCE = 1.0e4
D_FF = 4 * D_MODEL
Q_BLOCK = 128
EPS = 1e-6
IN_SPLITS = (D_LRU, D_LRU, 2 * D_CONF, MLA_Q_LORA, MLA_KV_LORA, MLA_ROPE, NSA_HEADS * NSA_HD, 6 * NSA_HD, 3 * NSA_HEADS)
D_IN = sum(IN_SPLITS)

kernel_name = 'hymba_style_lru_conv_mla_nsa_step'

F32 = jnp.float32


def rmsnorm(x, g):
    xf = x.astype(F32)
    y = xf * lax.rsqrt(jnp.mean(xf * xf, axis=-1, keepdims=True) + EPS)
    return (y * g.astype(F32)).astype(x.dtype)


def layernorm(x, g, b):
    xf = x.astype(F32)
    mu = jnp.mean(xf, axis=-1, keepdims=True)
    var = jnp.mean(jnp.square(xf - mu), axis=-1, keepdims=True)
    y = (xf - mu) * lax.rsqrt(var + EPS) * g.astype(F32) + b.astype(F32)
    return y.astype(x.dtype)


def masked_softmax(s, mask):
    s = jnp.where(mask, s, -1e30)
    m = jnp.max(s, axis=-1, keepdims=True)
    e = jnp.where(mask, jnp.exp(s - m), 0.0)
    return e / jnp.maximum(jnp.sum(e, axis=-1, keepdims=True), 1e-30)


def causal_dwconv(xp, w, b):
    c = xp.shape[-1]
    y = lax.conv_general_dilated(xp, w[:, None, :].astype(xp.dtype), (1,), 'VALID',
                                 dimension_numbers=('NWC', 'WIO', 'NWC'), feature_group_count=c)
    return y + b


def rope(x, pos):
    half = x.shape[-1] // 2
    freq = jnp.power(ROPE_THETA, -jnp.arange(half, dtype=F32) / half)
    ang = pos.astype(F32)[:, None] * freq
    ang = ang.reshape((1, ang.shape[0]) + (1,) * (x.ndim - 3) + (half,))
    cos, sin = jnp.cos(ang), jnp.sin(ang)
    xf = x.astype(F32)
    x1, x2 = xf[..., :half], xf[..., half:]
    return jnp.concatenate([x1 * cos - x2 * sin, x2 * cos + x1 * sin], axis=-1).astype(x.dtype)


def split_in(u):
    idx = np.cumsum(IN_SPLITS)[:-1].tolist()
    return jnp.split(u, idx, axis=-1)


def to_blocks(x, n):
    return x.reshape((x.shape[0], n, Q_BLOCK) + x.shape[2:]).swapaxes(0, 1)


def from_blocks(x):
    return x.swapaxes(0, 1).reshape((x.shape[1], x.shape[0] * x.shape[2]) + x.shape[3:])


def rglru(xb, gate, conv_state, h0, P):
    B, T, _ = xb.shape
    xp = jnp.concatenate([conv_state, xb], axis=1)
    y = causal_dwconv(xp, P['lru_conv_w'], P['lru_conv_b'])
    yb = y.reshape(B, T, LRU_BLOCKS, LRU_BS)
    r = jax.nn.sigmoid((jnp.einsum('btnd,nde->btne', yb, P['lru_wa']).reshape(B, T, D_LRU) + P['lru_ba']).astype(F32))
    i = jax.nn.sigmoid((jnp.einsum('btnd,nde->btne', yb, P['lru_wx']).reshape(B, T, D_LRU) + P['lru_bx']).astype(F32))
    log_a = -LRU_C * r * jax.nn.softplus(-P['lru_lam'].astype(F32))
    drive = jnp.sqrt(1.0 - jnp.exp(2.0 * log_a)) * i * y.astype(F32)

    def step(h, inp):
        a_t, b_t = inp
        h = a_t * h + b_t
        return h, h

    h_last, hs = lax.scan(step, h0.astype(F32), (jnp.exp(log_a).swapaxes(0, 1), drive.swapaxes(0, 1)))
    out = (hs.swapaxes(0, 1) * jax.nn.gelu(gate.astype(F32))).astype(xb.dtype)
    return out, xp[:, -(LRU_CONV - 1):], h_last.astype(h0.dtype)


def conformer_conv(u, conv_state, P):
    glu = u[..., :D_CONF] * jax.nn.sigmoid(u[..., D_CONF:])
    xp = jnp.concatenate([conv_state, glu], axis=1)
    y = causal_dwconv(xp, P['conf_dw_w'], P['conf_dw_b'])
    y = layernorm(y, P['conf_ln_g'], P['conf_ln_b'])
    return jax.nn.silu(y), xp[:, -(CONF_KERNEL - 1):]


def mla_project(cq, ckv, kr, pos, P):
    B, T, _ = cq.shape
    q = (rmsnorm(cq, P['mla_gq']) @ P['mla_wuq']).reshape(B, T, MLA_HEADS, MLA_NOPE + MLA_ROPE)
    q_rope = rope(q[..., MLA_NOPE:], pos).astype(F32)
    q_lat = jnp.einsum('bthd,rhd->bthr', q[..., :MLA_NOPE].astype(F32), P['mla_wuk'].astype(F32))
    row = jnp.concatenate([rmsnorm(ckv, P['mla_gkv']), rope(kr, pos)], axis=-1)
    return q_lat, q_rope, row


def mla_attend(q_lat, q_rope, q_pos, kv, k_pos):
    ckv = kv[..., :MLA_KV_LORA]
    kr = kv[..., MLA_KV_LORA:]
    s = (jnp.einsum('bqhr,bkr->bhqk', q_lat, ckv) + jnp.einsum('bqhp,bkp->bhqk', q_rope, kr)) * (MLA_NOPE + MLA_ROPE) ** -0.5
    p = masked_softmax(s, k_pos[None, :] <= q_pos[:, None])
    return jnp.einsum('bhqk,bkr->bqhr', p, ckv)


def mla_output(o_lat, P, dt):
    B, T = o_lat.shape[:2]
    return jnp.einsum('bthr,rhd->bthd', o_lat, P['mla_wuv'].astype(F32)).reshape(B, T, MLA_HEADS * MLA_V).astype(dt)


def nsa_project(nq, ng, nkv):
    B, T, _ = nq.shape
    q = nq.reshape(B, T, NSA_HEADS, NSA_HD)
    g = jax.nn.sigmoid(ng.astype(F32)).reshape(B, T, NSA_HEADS, 3)
    kv = nkv.reshape(B, T, 6, NSA_HD)
    return q, g, kv


def nsa_compress(rows, pe, wc):
    B, L, _ = rows.shape
    ratio = NSA_CMP_BLOCK // NSA_CMP_STRIDE
    n_chunk = L // NSA_CMP_STRIDE
    nc = n_chunk - ratio + 1
    ch = rows[:, :n_chunk * NSA_CMP_STRIDE].reshape(B, n_chunk, NSA_CMP_STRIDE, NSA_HD)
    blocks = jnp.concatenate([ch[:, j:j + nc] for j in range(ratio)], axis=2) + pe.astype(F32)
    return blocks.reshape(B, nc, NSA_CMP_BLOCK * NSA_HD) @ wc.astype(F32)


def nsa_cmp_end(nc):
    return jnp.arange(nc, dtype=jnp.int32) * NSA_CMP_STRIDE + (NSA_CMP_BLOCK - 1)


def nsa_attend(q, g, q_pos, kc, vc, kc_end, ks, vs, kw, vw, kw_pos):
    B, Q, H, _ = q.shape
    qf = q.astype(F32) * NSA_HD ** -0.5
    p_c = masked_softmax(jnp.einsum('bqhd,bnd->bhqn', qf, kc), kc_end[None, :] <= q_pos[:, None])
    o_c = jnp.einsum('bhqn,bnd->bqhd', p_c, vc)
    L = ks.shape[1]
    n_blk = -(-L // NSA_SEL_BLOCK)
    pad = n_blk * NSA_SEL_BLOCK - L
    ks_blk = jnp.pad(ks, ((0, 0), (0, pad), (0, 0))).reshape(B, n_blk, NSA_SEL_BLOCK, NSA_HD)
    vs_blk = jnp.pad(vs, ((0, 0), (0, pad), (0, 0))).reshape(B, n_blk, NSA_SEL_BLOCK, NSA_HD)
    cmp_start = jnp.arange(kc.shape[1]) * NSA_CMP_STRIDE
    sel_start = jnp.arange(n_blk) * NSA_SEL_BLOCK
    overlap = ((cmp_start[:, None] < sel_start[None, :] + NSA_SEL_BLOCK)
               & (cmp_start[:, None] + NSA_CMP_BLOCK > sel_start[None, :])).astype(F32)
    imp = jnp.einsum('bhqn,nj->bqj', p_c, overlap)
    blk = jnp.arange(n_blk)[None, :]
    cur = (q_pos // NSA_SEL_BLOCK)[:, None]
    valid = blk * NSA_SEL_BLOCK <= q_pos[:, None]
    forced = ((blk == 0) | (blk == cur) | (blk == cur - 1)).astype(F32)
    score = jnp.where(valid, imp + NSA_FORCE * forced, -1.0)
    n_sel = min(NSA_N_SEL, n_blk)
    _, idx = lax.top_k(score, n_sel)
    bidx = jnp.arange(B)[:, None, None]
    k_sel = ks_blk[bidx, idx].reshape(B, Q, n_sel * NSA_SEL_BLOCK, NSA_HD)
    v_sel = vs_blk[bidx, idx].reshape(B, Q, n_sel * NSA_SEL_BLOCK, NSA_HD)
    sel_pos = (idx[..., None] * NSA_SEL_BLOCK + jnp.arange(NSA_SEL_BLOCK)).reshape(B, Q, n_sel * NSA_SEL_BLOCK)
    p_s = masked_softmax(jnp.einsum('bqhd,bqkd->bhqk', qf, k_sel), (sel_pos <= q_pos[None, :, None])[:, None])
    o_s = jnp.einsum('bhqk,bqkd->bqhd', p_s, v_sel)
    kp = kw_pos[None, :]
    qp = q_pos[:, None]
    w_mask = (kp >= 0) & (kp <= qp) & (kp > qp - NSA_WINDOW)
    p_w = masked_softmax(jnp.einsum('bqhd,bkd->bhqk', qf, kw), w_mask)
    o_w = jnp.einsum('bhqk,bkd->bqhd', p_w, vw)
    return g[..., 0:1] * o_c + g[..., 1:2] * o_s + g[..., 2:3] * o_w


def mix_prompt(h, P):
    B, S, _ = h.shape
    dt = h.dtype
    pos = jnp.arange(S, dtype=jnp.int32)
    nqb = S // Q_BLOCK
    lru_x, lru_g, conf_u, cq, ckv, kr, nq, nkv, ng = split_in(h @ P['w_in'])
    a_out, a_conv, a_h = rglru(lru_x, lru_g, jnp.zeros((B, LRU_CONV - 1, D_LRU), dt), jnp.zeros((B, D_LRU), dt), P)
    b_out, b_conv = conformer_conv(conf_u, jnp.zeros((B, CONF_KERNEL - 1, D_CONF), dt), P)
    q_lat, q_rope, mla_row = mla_project(cq, ckv, kr, pos, P)
    keys = mla_row.astype(F32)
    o_lat = lax.map(lambda a: mla_attend(a[0], a[1], a[2], keys, pos),
                    (to_blocks(q_lat, nqb), to_blocks(q_rope, nqb), pos.reshape(nqb, Q_BLOCK)))
    c_out = mla_output(from_blocks(o_lat), P, dt)
    q, g, kv = nsa_project(nq, ng, nkv)
    kvf = kv.astype(F32)
    kc = nsa_compress(kvf[:, :, 0], P['nsa_pe_k'], P['nsa_wc_k'])
    vc = nsa_compress(kvf[:, :, 1], P['nsa_pe_v'], P['nsa_wc_v'])
    kc_end = nsa_cmp_end(kc.shape[1])
    ks, vs = kvf[:, :, 2], kvf[:, :, 3]
    kw_pad = jnp.pad(kvf[:, :, 4], ((0, 0), (NSA_WINDOW, 0), (0, 0)))
    vw_pad = jnp.pad(kvf[:, :, 5], ((0, 0), (NSA_WINDOW, 0), (0, 0)))

    def nsa_block(a):
        qb, gb, qp = a
        s0 = qp[0]
        kw = lax.dynamic_slice_in_dim(kw_pad, s0, NSA_WINDOW + Q_BLOCK, axis=1)
        vw = lax.dynamic_slice_in_dim(vw_pad, s0, NSA_WINDOW + Q_BLOCK, axis=1)
        kw_pos = s0 - NSA_WINDOW + jnp.arange(NSA_WINDOW + Q_BLOCK, dtype=jnp.int32)
        return nsa_attend(qb, gb, qp, kc, vc, kc_end, ks, vs, kw, vw, kw_pos)

    o_d = from_blocks(lax.map(nsa_block, (to_blocks(q, nqb), to_blocks(g, nqb), pos.reshape(nqb, Q_BLOCK))))
    d_out = o_d.reshape(B, S, NSA_HEADS * NSA_HD).astype(dt)
    mix = jnp.concatenate([a_out, b_out, c_out, d_out], axis=-1)
    n_win = min(NSA_WINDOW, S)
    return mix, (a_h, a_conv, b_conv, mla_row, kv[:, :, :4], kv[:, -n_win:, 4:])


def mix_sample(h, P, st):
    lru_h, lru_conv, conf_conv, mla_past, nsa_past, win_buf = st
    B, T, _ = h.shape
    dt = h.dtype
    past = mla_past.shape[1]
    pos = past + jnp.arange(T, dtype=jnp.int32)
    lru_x, lru_g, conf_u, cq, ckv, kr, nq, nkv, ng = split_in(h @ P['w_in'])
    a_out, a_conv, a_h = rglru(lru_x, lru_g, lru_conv, lru_h, P)
    b_out, b_conv = conformer_conv(conf_u, conf_conv, P)
    q_lat, q_rope, mla_row = mla_project(cq, ckv, kr, pos, P)
    keys = jnp.concatenate([mla_past, mla_row.astype(mla_past.dtype)], axis=1).astype(F32)
    o_lat = mla_attend(q_lat, q_rope, pos, keys, jnp.arange(past + T, dtype=jnp.int32))
    c_out = mla_output(o_lat, P, dt)
    q, g, kv = nsa_project(nq, ng, nkv)
    full = jnp.concatenate([nsa_past, kv[:, :, :4].astype(nsa_past.dtype)], axis=1).astype(F32)
    kc = nsa_compress(full[:, :, 0], P['nsa_pe_k'], P['nsa_wc_k'])
    vc = nsa_compress(full[:, :, 1], P['nsa_pe_v'], P['nsa_wc_v'])
    kc_end = nsa_cmp_end(kc.shape[1])
    wkv = jnp.concatenate([win_buf, kv[:, :, 4:].astype(win_buf.dtype)], axis=1)
    n_buf = win_buf.shape[1]
    kw_pos = past - n_buf + jnp.arange(n_buf + T, dtype=jnp.int32)
    wkvf = wkv.astype(F32)
    o_d = nsa_attend(q, g, pos, kc, vc, kc_end, full[:, :, 2], full[:, :, 3], wkvf[:, :, 0], wkvf[:, :, 1], kw_pos)
    d_out = o_d.reshape(B, T, NSA_HEADS * NSA_HD).astype(dt)
    mix = jnp.concatenate([a_out, b_out, c_out, d_out], axis=-1)
    n_win = min(NSA_WINDOW, past + T)
    return mix, (a_h, a_conv, b_conv, mla_row, kv[:, :, :4], wkv[:, -n_win:])


def block(x, c, P, mixer):
    mod = (jax.nn.silu(c) @ P['w_mod'] + P['b_mod'])[:, None, :]
    sh1, sc1, gt1, sh2, sc2, gt2 = jnp.split(mod, 6, axis=-1)
    h = rmsnorm(x, P['g_pre_mix']) * (1.0 + sc1) + sh1
    mix, states = mixer(h)
    x = x + gt1 * rmsnorm(mix @ P['w_out'], P['g_post_mix'])
    h = rmsnorm(x, P['g_pre_ffn']) * (1.0 + sc2) + sh2
    f = jnp.square(jax.nn.relu(h @ P['w_ff1'])) @ P['w_ff2']
    x = x + gt2 * rmsnorm(f, P['g_post_ffn'])
    return x, states


def stack_state(states, i):
    return jnp.stack([s[i] for s in states])


def setup_inputs(seed: int = 0) -> dict:
    key = jax.random.key(seed)
    ks = iter(jax.random.split(key, 64))

    def nrm(shape, scale):
        return jax.random.normal(next(ks), shape, jnp.float32) * scale

    def gain(shape):
        return 1.0 + nrm(shape, 0.1)

    L = DEPTH
    n_pages = PAST_LEN // PAGE_SIZE
    n_pool = (DEC_BATCH * n_pages * 5) // 4
    n_buf = min(NSA_WINDOW, PAST_LEN)
    a0 = jax.random.uniform(next(ks), (L, D_LRU), jnp.float32, 0.9, 0.999)
    page_table = jax.random.permutation(next(ks), n_pool)[:DEC_BATCH * n_pages].reshape(DEC_BATCH, n_pages).astype(jnp.int32)
    return {
        'x_prompt': nrm((BATCH, SEQ, D_MODEL), 1.0),
        'x_sample': nrm((DEC_BATCH, DEC_SEQ, D_MODEL), 1.0),
        'c_prompt': nrm((BATCH, D_MODEL), 1.0),
        'c_sample': nrm((DEC_BATCH, D_MODEL), 1.0),
        'state_lru_h': nrm((L, DEC_BATCH, D_LRU), 0.5),
        'state_lru_conv': nrm((L, DEC_BATCH, LRU_CONV - 1, D_LRU), 1.0),
        'state_conv': nrm((L, DEC_BATCH, CONF_KERNEL - 1, D_CONF), 0.5),
        'cache_mla': nrm((L, n_pool, PAGE_SIZE, MLA_CACHE_W), 1.0),
        'cache_nsa': nrm((L, n_pool, PAGE_SIZE, 4, NSA_HD), 1.0),
        'cache_nsa_win': nrm((L, DEC_BATCH, n_buf, 2, NSA_HD), 1.0),
        'page_table': page_table,
        'w_mod': nrm((L, D_MODEL, 6 * D_MODEL), 0.5 * D_MODEL ** -0.5),
        'b_mod': nrm((L, 6 * D_MODEL), 0.02),
        'g_pre_mix': gain((L, D_MODEL)),
        'g_post_mix': gain((L, D_MODEL)),
        'g_pre_ffn': gain((L, D_MODEL)),
        'g_post_ffn': gain((L, D_MODEL)),
        'w_in': nrm((L, D_MODEL, D_IN), D_MODEL ** -0.5),
        'lru_conv_w': nrm((L, LRU_CONV, D_LRU), LRU_CONV ** -0.5),
        'lru_conv_b': nrm((L, D_LRU), 0.02),
        'lru_wa': nrm((L, LRU_BLOCKS, LRU_BS, LRU_BS), LRU_BS ** -0.5),
        'lru_ba': nrm((L, D_LRU), 0.02),
        'lru_wx': nrm((L, LRU_BLOCKS, LRU_BS, LRU_BS), LRU_BS ** -0.5),
        'lru_bx': nrm((L, D_LRU), 0.02),
        'lru_lam': jnp.log(a0) - jnp.log1p(-a0),
        'conf_dw_w': nrm((L, CONF_KERNEL, D_CONF), CONF_KERNEL ** -0.5),
        'conf_dw_b': nrm((L, D_CONF), 0.02),
        'conf_ln_g': gain((L, D_CONF)),
        'conf_ln_b': nrm((L, D_CONF), 0.02),
        'mla_gq': gain((L, MLA_Q_LORA)),
        'mla_wuq': nrm((L, MLA_Q_LORA, MLA_HEADS * (MLA_NOPE + MLA_ROPE)), MLA_Q_LORA ** -0.5),
        'mla_gkv': gain((L, MLA_KV_LORA)),
        'mla_wuk': nrm((L, MLA_KV_LORA, MLA_HEADS, MLA_NOPE), MLA_KV_LORA ** -0.5),
        'mla_wuv': nrm((L, MLA_KV_LORA, MLA_HEADS, MLA_V), MLA_KV_LORA ** -0.5),
        'nsa_pe_k': nrm((L, NSA_CMP_BLOCK, NSA_HD), 0.1),
        'nsa_wc_k': nrm((L, NSA_CMP_BLOCK * NSA_HD, NSA_HD), (NSA_CMP_BLOCK * NSA_HD) ** -0.5),
        'nsa_pe_v': nrm((L, NSA_CMP_BLOCK, NSA_HD), 0.1),
        'nsa_wc_v': nrm((L, NSA_CMP_BLOCK * NSA_HD, NSA_HD), (NSA_CMP_BLOCK * NSA_HD) ** -0.5),
        'w_out': nrm((L, D_MIX, D_MODEL), D_MIX ** -0.5),
        'w_ff1': nrm((L, D_MODEL, D_FF), D_MODEL ** -0.5),
        'w_ff2': nrm((L, D_FF, D_MODEL), D_FF ** -0.5),
    }


def reference(x_prompt, x_sample, c_prompt, c_sample, state_lru_h, state_lru_conv, state_conv,
              cache_mla, cache_nsa, cache_nsa_win, page_table,
              w_mod, b_mod, g_pre_mix, g_post_mix, g_pre_ffn, g_post_ffn, w_in,
              lru_conv_w, lru_conv_b, lru_wa, lru_ba, lru_wx, lru_bx, lru_lam,
              conf_dw_w, conf_dw_b, conf_ln_g, conf_ln_b,
              mla_gq, mla_wuq, mla_gkv, mla_wuk, mla_wuv,
              nsa_pe_k, nsa_wc_k, nsa_pe_v, nsa_wc_v,
              w_out, w_ff1, w_ff2):
    db = x_sample.shape[0]
    past_len = page_table.shape[1] * cache_mla.shape[2]
    xp, xs = x_prompt, x_sample
    sp, ss = [], []
    for l in range(DEPTH):
        P = {
            'w_mod': w_mod[l], 'b_mod': b_mod[l],
            'g_pre_mix': g_pre_mix[l], 'g_post_mix': g_post_mix[l],
            'g_pre_ffn': g_pre_ffn[l], 'g_post_ffn': g_post_ffn[l],
            'w_in': w_in[l],
            'lru_conv_w': lru_conv_w[l], 'lru_conv_b': lru_conv_b[l],
            'lru_wa': lru_wa[l], 'lru_ba': lru_ba[l], 'lru_wx': lru_wx[l], 'lru_bx': lru_bx[l],
            'lru_lam': lru_lam[l],
            'conf_dw_w': conf_dw_w[l], 'conf_dw_b': conf_dw_b[l],
            'conf_ln_g': conf_ln_g[l], 'conf_ln_b': conf_ln_b[l],
            'mla_gq': mla_gq[l], 'mla_wuq': mla_wuq[l], 'mla_gkv': mla_gkv[l],
            'mla_wuk': mla_wuk[l], 'mla_wuv': mla_wuv[l],
            'nsa_pe_k': nsa_pe_k[l], 'nsa_wc_k': nsa_wc_k[l],
            'nsa_pe_v': nsa_pe_v[l], 'nsa_wc_v': nsa_wc_v[l],
            'w_out': w_out[l], 'w_ff1': w_ff1[l], 'w_ff2': w_ff2[l],
        }
        xp, st_p = block(xp, c_prompt, P, functools.partial(mix_prompt, P=P))
        sp.append(st_p)
        past_state = (
            state_lru_h[l], state_lru_conv[l], state_conv[l],
            cache_mla[l, page_table].reshape(db, past_len, MLA_CACHE_W),
            cache_nsa[l, page_table].reshape(db, past_len, 4, NSA_HD),
            cache_nsa_win[l],
        )
        xs, st_s = block(xs, c_sample, P, functools.partial(mix_sample, P=P, st=past_state))
        ss.append(st_s)
    lru_h_p, lru_h_s = stack_state(sp, 0), stack_state(ss, 0)
    lru_conv_p, lru_conv_s = stack_state(sp, 1), stack_state(ss, 1)
    conv_p, conv_s = stack_state(sp, 2), stack_state(ss, 2)
    mla_p, mla_s = stack_state(sp, 3), stack_state(ss, 3)
    nsa_p, nsa_s = stack_state(sp, 4), stack_state(ss, 4)
    win_p, win_s = stack_state(sp, 5), stack_state(ss, 5)
    return (xp, xs, lru_h_p, lru_h_s, lru_conv_p, lru_conv_s, conv_p, conv_s,
            mla_p, mla_s, nsa_p, nsa_s, win_p, win_s)
```

```python
import functools

import numpy as np
import jax
import jax.numpy as jnp
from jax import lax
from jax.experimental import pallas as pl
from jax.experimental.pallas import tpu as pltpu

F32 = jnp.float32
BF16 = jnp.bfloat16

D_MODEL = 1024
D_LRU = 256
LRU_BLOCKS = 4
LRU_CONV = 4
LRU_C = 8.0
D_CONF = 256
CONF_KERNEL = 31
MLA_HEADS = 4
MLA_NOPE = 64
MLA_ROPE = 32
MLA_V = 64
MLA_Q_LORA = 256
MLA_KV_LORA = 128
MLA_CACHE_W = MLA_KV_LORA + MLA_ROPE
ROPE_THETA = 10000.0
NSA_HEADS = 4
NSA_HD = 64
NSA_CMP_BLOCK = 32
NSA_CMP_STRIDE = 16
NSA_SEL_BLOCK = 64
NSA_N_SEL = 16
NSA_WINDOW = 512
NSA_FORCE = 1.0e4
D_FF = 4 * D_MODEL
EPS = 1e-6
IN_SPLITS = (D_LRU, D_LRU, 2 * D_CONF, MLA_Q_LORA, MLA_KV_LORA, MLA_ROPE, NSA_HEADS * NSA_HD, 6 * NSA_HD, 3 * NSA_HEADS)

C_LRU = 0
C_CONF = 512
C_CQ = 1024
C_CKV = 1280
C_NQ = 1408
C_NKV = 1664
C_MISC = 2048
D_INP = 2176
MISC_NG = 64

NEG = -1e30
VMEM_LIMIT_V7X = 56 * 1024 * 1024
Q_TILE = 128
K_TILE = 512
PAGES_PER_STEP = 16
SAMPLE_R = 128


def _cparams(sem):
    return pltpu.CompilerParams(dimension_semantics=sem, vmem_limit_bytes=VMEM_LIMIT_V7X)


def _rms(x, g):
    return x * lax.rsqrt(jnp.mean(x * x, axis=-1, keepdims=True) + EPS) * g


def _bdot(a, b):
    return jnp.dot(a.astype(BF16), b.astype(BF16), preferred_element_type=F32)


def _dot_nt(a, b):
    return lax.dot_general(a.astype(BF16), b.astype(BF16), (((1,), (1,)), ((), ())), preferred_element_type=F32)


def _dot_tn(a, b):
    return lax.dot_general(a.astype(BF16), b.astype(BF16), (((0,), (0,)), ((), ())), preferred_element_type=F32)


def _mod_kernel(c_ref, w_ref, b_ref, o_ref):
    o_ref[...] = _bdot(jax.nn.silu(c_ref[...]), w_ref[...]) + b_ref[...]


def _modulation(c_all, w_mod, b_mod):
    L, D, D6 = w_mod.shape
    rows = c_all.shape[0]
    tn = 1536
    return pl.pallas_call(
        _mod_kernel,
        grid=(L, D6 // tn),
        in_specs=[pl.BlockSpec((rows, D), lambda l, j: (0, 0)),
                  pl.BlockSpec((None, D, tn), lambda l, j: (l, 0, j)),
                  pl.BlockSpec((None, 1, tn), lambda l, j: (l, 0, j))],
        out_specs=pl.BlockSpec((None, rows, tn), lambda l, j: (l, 0, j)),
        out_shape=jax.ShapeDtypeStruct((L, rows, D6), F32),
        compiler_params=_cparams(("arbitrary", "arbitrary")),
        name="modulation",
    )(c_all, w_mod, b_mod.reshape(L, 1, D6))


def _pe_kernel(pe_ref, w_ref, o_ref):
    acc = jnp.zeros((8, 256), F32)
    for r in range(NSA_CMP_STRIDE):
        acc = acc + _bdot(pe_ref[r], w_ref[r])
    o_ref[...] = jnp.broadcast_to(acc[0:1, 0:128] + acc[1:2, 128:256], (8, 128))


def _pe_term(pe2, w4):
    L = pe2.shape[0]
    return pl.pallas_call(
        _pe_kernel,
        grid=(L,),
        in_specs=[pl.BlockSpec((None, NSA_CMP_STRIDE, 8, 128), lambda l: (l, 0, 0, 0)),
                  pl.BlockSpec((None, NSA_CMP_STRIDE, 128, 256), lambda l: (l, 0, 0, 0))],
        out_specs=pl.BlockSpec((None, 8, 128), lambda l: (l, 0, 0)),
        out_shape=jax.ShapeDtypeStruct((L, 8, 128), F32),
        compiler_params=_cparams(("arbitrary",)),
        name="pe_term",
    )(pe2, w4)


def _in_proj_kernel(x_ref, sh_ref, sc_ref, g_ref, w_ref, gq_ref, wuq_ref, wuk_ref, gkv_ref, cos_ref, sin_ref,
                    lru_ref, glu_ref, qm_ref, row_ref, nq_ref, nsa4_ref, win_ref, gate_ref):
    h = _rms(x_ref[...], g_ref[...]) * (1.0 + sc_ref[...]) + sh_ref[...]
    u = _bdot(h, w_ref[...])
    lru_ref[...] = u[:, C_LRU:C_LRU + 512]
    glu_ref[...] = u[:, C_CONF:C_CONF + 256] * jax.nn.sigmoid(u[:, C_CONF + 256:C_CONF + 512])
    nsa4_ref[...] = u[:, C_NKV:C_NKV + 256]
    win_ref[...] = u[:, C_NKV + 256:C_NKV + 384]
    misc = u[:, C_MISC:C_MISC + 128]
    gate_ref[...] = jax.nn.sigmoid(misc)
    cosv = cos_ref[...]
    sinv = sin_ref[...]
    qr = _bdot(_rms(u[:, C_CQ:C_CQ + 256], gq_ref[...]), wuq_ref[...])
    q_rope = qr[:, 256:384] * cosv + qr[:, 384:512] * sinv
    q_lat = _bdot(qr[:, 0:256], wuk_ref[...])
    scale = (MLA_NOPE + MLA_ROPE) ** -0.5
    for hd in range(MLA_HEADS):
        qm_ref[hd, :, 0:128] = q_lat[:, 128 * hd:128 * (hd + 1)] * scale
        qm_ref[hd, :, 128:160] = q_rope[:, 32 * hd:32 * (hd + 1)] * scale
    row_ref[:, 0:128] = _rms(u[:, C_CKV:C_CKV + 128], gkv_ref[...])
    row_ref[:, 128:160] = misc[:, 0:32] * cosv[:, 0:32] + misc[:, 32:64] * sinv[:, 0:32]
    nq = u[:, C_NQ:C_NQ + 256] * (NSA_HD ** -0.5)
    for hd in range(NSA_HEADS):
        nq_ref[hd] = nq[:, 64 * hd:64 * (hd + 1)]


def _in_proj(x, sh, sc, g, w_in, gq, wuq, wuk, gkv, cosv, sinv, *, per_token, tm):
    N, D = x.shape
    nt = N // tm
    if per_token:
        mod_spec = pl.BlockSpec((tm, D), lambda i: (i, 0))
        tab_spec = pl.BlockSpec((tm, 128), lambda i: (i, 0))
    else:
        per_b = nt // sh.shape[0]
        mod_spec = pl.BlockSpec((None, 1, D), lambda i: (i // per_b, 0, 0))
        tab_spec = pl.BlockSpec((tm, 128), lambda i: (i % per_b, 0))
    const = lambda shape: pl.BlockSpec(shape, lambda i: tuple(0 for _ in shape))
    row = lambda w: pl.BlockSpec((tm, w), lambda i: (i, 0))
    head = lambda w: pl.BlockSpec((4, tm, w), lambda i: (0, i, 0))
    return pl.pallas_call(
        _in_proj_kernel,
        grid=(nt,),
        in_specs=[row(D), mod_spec, mod_spec, const((1, D)), const((D, D_INP)), const((1, 256)),
                  const((256, 512)), const((256, 512)), const((1, 128)), tab_spec, tab_spec],
        out_specs=[row(512), row(256), head(MLA_CACHE_W), row(MLA_CACHE_W), head(NSA_HD), row(256), row(128), row(128)],
        out_shape=[jax.ShapeDtypeStruct((N, 512), F32), jax.ShapeDtypeStruct((N, 256), F32),
                   jax.ShapeDtypeStruct((4, N, MLA_CACHE_W), F32), jax.ShapeDtypeStruct((N, MLA_CACHE_W), F32),
                   jax.ShapeDtypeStruct((4, N, NSA_HD), F32), jax.ShapeDtypeStruct((N, 256), F32),
                   jax.ShapeDtypeStruct((N, 128), F32), jax.ShapeDtypeStruct((N, 128), F32)],
        compiler_params=_cparams(("parallel",)),
        name="in_proj",
    )(x, sh, sc, g, w_in, gq, wuq, wuk, gkv, cosv, sinv)


def _post_kernel(x_ref, a_ref, b_ref, c_ref, d_ref, gt1_ref, sh2_ref, sc2_ref, gt2_ref,
                 wo_ref, gpm_ref, gpf_ref, gqf_ref, w1_ref, w2_ref, o_ref):
    y = _bdot(a_ref[...], wo_ref[0:256, :])
    y = y + _bdot(b_ref[...], wo_ref[256:512, :])
    y = y + _bdot(c_ref[...], wo_ref[512:768, :])
    y = y + _bdot(d_ref[...], wo_ref[768:1024, :])
    x = x_ref[...] + gt1_ref[...] * _rms(y, gpm_ref[...])
    h = (_rms(x, gpf_ref[...]) * (1.0 + sc2_ref[...]) + sh2_ref[...]).astype(BF16)
    f = jnp.zeros(x.shape, F32)
    fc = 1024
    for j in range(D_FF // fc):
        t = jnp.maximum(jnp.dot(h, w1_ref[:, j * fc:(j + 1) * fc], preferred_element_type=F32), 0.0)
        f = f + _bdot(t * t, w2_ref[j * fc:(j + 1) * fc, :])
    o_ref[...] = x + gt2_ref[...] * _rms(f, gqf_ref[...])


def _post(x, a, b, c, d, gt1, sh2, sc2, gt2, wo, gpm, gpf, gqf, w1, w2, *, per_token, tm):
    N, D = x.shape
    nt = N // tm
    if per_token:
        mod_spec = pl.BlockSpec((tm, D), lambda i: (i, 0))
    else:
        per_b = nt // gt1.shape[0]
        mod_spec = pl.BlockSpec((None, 1, D), lambda i: (i // per_b, 0, 0))
    const = lambda shape: pl.BlockSpec(shape, lambda i: tuple(0 for _ in shape))
    row = lambda w: pl.BlockSpec((tm, w), lambda i: (i, 0))
    return pl.pallas_call(
        _post_kernel,
        grid=(nt,),
        in_specs=[row(D), row(256), row(256), row(256), row(256), mod_spec, mod_spec, mod_spec, mod_spec,
                  const((D, D)), const((1, D)), const((1, D)), const((1, D)), const((D, D_FF)), const((D_FF, D))],
        out_specs=row(D),
        out_shape=jax.ShapeDtypeStruct((N, D), F32),
        compiler_params=_cparams(("parallel",)),
        name="post",
    )(x, a, b, c, d, gt1, sh2, sc2, gt2, wo, gpm, gpf, gqf, w1, w2)


def _lru_gates(y, gate_w, lam):
    wa, ba, wx, bx = gate_w
    r = jax.nn.sigmoid(_bdot(y, wa) + ba)
    i = jax.nn.sigmoid(_bdot(y, wx) + bx)
    log_a = -LRU_C * r * jax.nn.softplus(-lam)
    return jnp.exp(log_a), jnp.sqrt(1.0 - jnp.exp(2.0 * log_a)) * i * y


def _layernorm(y, g, b):
    mu = jnp.mean(y, axis=-1, keepdims=True)
    var = jnp.mean(jnp.square(y - mu), axis=-1, keepdims=True)
    return (y - mu) * lax.rsqrt(var + EPS) * g + b


def _seq_prompt_kernel(xg_ref, glu_ref, cw_ref, cb_ref, wa_ref, ba_ref, wx_ref, bx_ref, lam_ref,
                       dw_ref, db_ref, lng_ref, lnb_ref,
                       a_ref, b_ref, hl_ref, xbuf, gbuf, abuf, dbuf, hbuf):
    i = pl.program_id(0)
    nb, tc, _ = glu_ref.shape

    @pl.when(i == 0)
    def _():
        xbuf[:, 0:8, :] = jnp.zeros((nb, 8, D_LRU), F32)
        gbuf[:, 0:32, :] = jnp.zeros((nb, 32, D_CONF), F32)
        hbuf[...] = jnp.zeros(hbuf.shape, F32)

    xbuf[:, 8:8 + tc, :] = xg_ref[:, :, 0:D_LRU]
    y = jnp.zeros((nb, tc, D_LRU), F32) + cb_ref[...]
    for k in range(LRU_CONV):
        y = y + xbuf[:, 8 - (LRU_CONV - 1) + k:8 - (LRU_CONV - 1) + k + tc, :] * cw_ref[k:k + 1, :]
    a, drive = _lru_gates(y.reshape(nb * tc, D_LRU),
                          (wa_ref[...], ba_ref[...], wx_ref[...], bx_ref[...]), lam_ref[...])
    abuf[...] = a.reshape(nb, tc, D_LRU)
    dbuf[...] = drive.reshape(nb, tc, D_LRU)

    def step(t, h):
        h = abuf[:, pl.ds(t, 1), :] * h + dbuf[:, pl.ds(t, 1), :]
        dbuf[:, pl.ds(t, 1), :] = h
        return h

    h = lax.fori_loop(0, tc, step, hbuf[...], unroll=8)
    hbuf[...] = h
    hl_ref[...] = jnp.broadcast_to(h, hl_ref.shape)
    a_ref[...] = dbuf[...] * jax.nn.gelu(xg_ref[:, :, D_LRU:2 * D_LRU])
    xbuf[:, 0:8, :] = xbuf[:, tc:tc + 8, :]

    gbuf[:, 32:32 + tc, :] = glu_ref[...]
    z = jnp.zeros((nb, tc, D_CONF), F32) + db_ref[...]
    for k in range(CONF_KERNEL):
        o = 32 - (CONF_KERNEL - 1) + k
        z = z + gbuf[:, o:o + tc, :] * dw_ref[k:k + 1, :]
    b_ref[...] = jax.nn.silu(_layernorm(z, lng_ref[...], lnb_ref[...]))
    gbuf[:, 0:32, :] = gbuf[:, tc:tc + 32, :]


def _seq_prompt(xg, glu, P, *, tc):
    nb, S, _ = glu.shape
    const = lambda shape: pl.BlockSpec(shape, lambda i: tuple(0 for _ in shape))
    seq = lambda w: pl.BlockSpec((nb, tc, w), lambda i: (0, i, 0))
    return pl.pallas_call(
        _seq_prompt_kernel,
        grid=(S // tc,),
        in_specs=[seq(512), seq(256), const((LRU_CONV, 256)), const((1, 256)), const((256, 256)), const((1, 256)),
                  const((256, 256)), const((1, 256)), const((1, 256)), const((CONF_KERNEL, 256)), const((1, 256)),
                  const((1, 256)), const((1, 256))],
        out_specs=[seq(256), seq(256), const((nb, 8, 256))],
        out_shape=[jax.ShapeDtypeStruct((nb, S, 256), F32), jax.ShapeDtypeStruct((nb, S, 256), F32),
                   jax.ShapeDtypeStruct((nb, 8, 256), F32)],
        scratch_shapes=[pltpu.VMEM((nb, tc + 8, 256), F32), pltpu.VMEM((nb, tc + 32, 256), F32),
                        pltpu.VMEM((nb, tc, 256), F32), pltpu.VMEM((nb, tc, 256), F32),
                        pltpu.VMEM((nb, 1, 256), F32)],
        compiler_params=_cparams(("arbitrary",)),
        name="seq_prompt",
    )(xg, glu, P['lru_conv_w'], P['lru_conv_b'], P['lru_wa_bd'], P['lru_ba'], P['lru_wx_bd'], P['lru_bx'],
      P['lru_lam'], P['conf_dw_w'], P['conf_dw_b'], P['conf_ln_g'], P['conf_ln_b'])


def _seq_sample_kernel(xg_ref, glu_ref, cs_ref, gs_ref, h0_ref, cw_ref, cb_ref, wa_ref, ba_ref, wx_ref, bx_ref,
                       lam_ref, dw_ref, db_ref, lng_ref, lnb_ref,
                       a_ref, b_ref, hl_ref, ncs_ref, ngs_ref, xbuf, gbuf):
    T, nb, _ = glu_ref.shape
    nc, ng = LRU_CONV - 1, CONF_KERNEL - 1
    xbuf[0:nc] = cs_ref[...]
    xbuf[nc:nc + T] = xg_ref[:, :, 0:D_LRU]
    y = jnp.zeros((T, nb, D_LRU), F32) + cb_ref[...]
    for k in range(LRU_CONV):
        y = y + xbuf[k:k + T] * cw_ref[k:k + 1, :]
    a, drive = _lru_gates(y.reshape(T * nb, D_LRU),
                          (wa_ref[...], ba_ref[...], wx_ref[...], bx_ref[...]), lam_ref[...])
    a = a.reshape(T, nb, D_LRU)
    drive = drive.reshape(T, nb, D_LRU)
    h = h0_ref[...]
    for t in range(T):
        h = a[t] * h + drive[t]
        a_ref[t] = h * jax.nn.gelu(xg_ref[t, :, D_LRU:2 * D_LRU])
    hl_ref[...] = h
    ncs_ref[...] = xbuf[T:T + nc]

    gbuf[0:ng] = gs_ref[...]
    gbuf[ng:ng + T] = glu_ref[...]
    z = jnp.zeros((T, nb, D_CONF), F32) + db_ref[...]
    for k in range(CONF_KERNEL):
        z = z + gbuf[k:k + T] * dw_ref[k:k + 1, :]
    b_ref[...] = jax.nn.silu(_layernorm(z, lng_ref[...], lnb_ref[...]))
    ngs_ref[...] = gbuf[T:T + ng]


def _seq_sample(xg_t, glu_t, cs_t, gs_t, h0, P):
    T, nb, _ = glu_t.shape
    nc, ng = LRU_CONV - 1, CONF_KERNEL - 1
    full = lambda a: pl.BlockSpec(a.shape, lambda i: tuple(0 for _ in a.shape))
    args = (xg_t, glu_t, cs_t, gs_t, h0, P['lru_conv_w'], P['lru_conv_b'], P['lru_wa_bd'], P['lru_ba'],
            P['lru_wx_bd'], P['lru_bx'], P['lru_lam'], P['conf_dw_w'], P['conf_dw_b'], P['conf_ln_g'], P['conf_ln_b'])
    outs = [jax.ShapeDtypeStruct((T, nb, 256), F32), jax.ShapeDtypeStruct((T, nb, 256), F32),
            jax.ShapeDtypeStruct((nb, 256), F32), jax.ShapeDtypeStruct((nc, nb, 256), F32),
            jax.ShapeDtypeStruct((ng, nb, 256), F32)]
    return pl.pallas_call(
        _seq_sample_kernel,
        grid=(1,),
        in_specs=[full(a) for a in args],
        out_specs=[full(o) for o in outs],
        out_shape=outs,
        scratch_shapes=[pltpu.VMEM((T + nc, nb, 256), F32), pltpu.VMEM((T + ng, nb, 256), F32)],
        compiler_params=_cparams(("arbitrary",)),
        name="seq_sample",
    )(*args)


def _online_update(s, v_t, m_ref, l_ref, acc_ref):
    m_prev = m_ref[...]
    m_new = jnp.maximum(m_prev, jnp.max(s, axis=0, keepdims=True))
    alpha = jnp.exp(m_prev - m_new)
    p = jnp.exp(s - m_new)
    l_ref[...] = alpha * l_ref[...] + jnp.sum(p, axis=0, keepdims=True)
    acc_ref[...] = alpha * acc_ref[...] + jnp.dot(v_t, p.astype(BF16), preferred_element_type=F32)
    m_ref[...] = m_new


def _online_update_tn(s, v, m_ref, l_ref, acc_ref):
    m_prev = m_ref[...]
    m_new = jnp.maximum(m_prev, jnp.max(s, axis=0, keepdims=True))
    alpha = jnp.exp(m_prev - m_new)
    p = jnp.exp(s - m_new)
    l_ref[...] = alpha * l_ref[...] + jnp.sum(p, axis=0, keepdims=True)
    acc_ref[...] = alpha * acc_ref[...] + _dot_tn(v, p)
    m_ref[...] = m_new


def _init_stats(m_ref, l_ref, acc_ref):
    m_ref[...] = jnp.full(m_ref.shape, NEG, F32)
    l_ref[...] = jnp.zeros(l_ref.shape, F32)
    acc_ref[...] = jnp.zeros(acc_ref.shape, F32)


def _finish(l_ref, acc_ref):
    return acc_ref[...] / jnp.maximum(l_ref[...], 1e-30)


def _compressed_kv(p_ref, pe_ref, nc):
    ncp = p_ref.shape[0]
    top = p_ref[:, 0:128]
    bot = pltpu.roll(p_ref[:, 128:256], ncp - 1, 0)
    n = lax.broadcasted_iota(jnp.int32, (ncp, 128), 0)
    return jnp.where(n < nc, top + bot + pe_ref[0:1, :], 0.0)


def _compressed_branch(q, kcv, q_pos, nc):
    ncp = kcv.shape[0]
    s = _dot_nt(kcv[:, 0:64], q)
    n = lax.broadcasted_iota(jnp.int32, s.shape, 0)
    mask = (n * NSA_CMP_STRIDE + (NSA_CMP_BLOCK - 1) <= q_pos) & (n < nc)
    s = jnp.where(mask, s, NEG)
    e = jnp.where(mask, jnp.exp(s - jnp.max(s, axis=0, keepdims=True)), 0.0)
    p = e / jnp.maximum(jnp.sum(e, axis=0, keepdims=True), 1e-30)
    o = _dot_tn(kcv[:, 64:128], p)
    return o, p


def _select_blocks(imp, q_pos, n_sel):
    blk = lax.broadcasted_iota(jnp.int32, imp.shape, 0)
    cur = q_pos // NSA_SEL_BLOCK
    valid = blk * NSA_SEL_BLOCK <= q_pos
    forced = (blk == 0) | (blk == cur) | (blk == cur - 1)
    score = jnp.where(valid, imp + jnp.where(forced, NSA_FORCE, 0.0), -1.0)
    big = jnp.int32(2 ** 30)
    sel = jnp.zeros(imp.shape, jnp.bool_)
    for _ in range(n_sel):
        mx = jnp.max(score, axis=0, keepdims=True)
        first = jnp.min(jnp.where(score == mx, blk, big), axis=0, keepdims=True)
        hit = blk == first
        sel = sel | hit
        score = jnp.where(hit, -3e38, score)
    return jnp.where(sel, 0.0, NEG)


def _mla_prompt_kernel(q_ref, k_ref, vt_ref, wuvt_ref, o_ref, m_ref, l_ref, acc_ref):
    i = pl.program_id(1)
    _, tq, dk = q_ref.shape
    tk = k_ref.shape[1]
    R = MLA_HEADS * tq
    q = q_ref[...].reshape(R, dk).astype(BF16)
    _init_stats(m_ref, l_ref, acc_ref)
    n_full = (i * tq) // tk

    def full_step(j, c):
        _online_update(_dot_nt(k_ref[j], q), vt_ref[j], m_ref, l_ref, acc_ref)
        return c

    lax.fori_loop(0, n_full, full_step, 0)
    s = _dot_nt(k_ref[n_full], q)
    k_pos = n_full * tk + lax.broadcasted_iota(jnp.int32, (tk, R), 0)
    q_pos = i * tq + lax.broadcasted_iota(jnp.int32, (tk, R), 1) % tq
    _online_update(jnp.where(k_pos <= q_pos, s, NEG), vt_ref[n_full], m_ref, l_ref, acc_ref)
    o_t = _finish(l_ref, acc_ref)
    c_t = jnp.concatenate([_bdot(wuvt_ref[hd], o_t[:, hd * tq:(hd + 1) * tq]) for hd in range(MLA_HEADS)], axis=0)
    o_ref[...] = c_t.T


def _mla_prompt(q, k3, vt3, wuvt, *, tq):
    nb, nk, tk, dk = k3.shape
    N = q.shape[1]
    S = N // nb
    nq = S // tq
    return pl.pallas_call(
        _mla_prompt_kernel,
        grid=(nb, nq),
        in_specs=[pl.BlockSpec((4, tq, dk), lambda b, i: (0, b * nq + i, 0)),
                  pl.BlockSpec((None, nk, tk, dk), lambda b, i: (b, 0, 0, 0)),
                  pl.BlockSpec((None, nk, MLA_KV_LORA, tk), lambda b, i: (b, 0, 0, 0)),
                  pl.BlockSpec((4, MLA_V, MLA_KV_LORA), lambda b, i: (0, 0, 0))],
        out_specs=pl.BlockSpec((tq, 256), lambda b, i: (b * nq + i, 0)),
        out_shape=jax.ShapeDtypeStruct((N, 256), F32),
        scratch_shapes=[pltpu.VMEM((1, 4 * tq), F32), pltpu.VMEM((1, 4 * tq), F32),
                        pltpu.VMEM((MLA_KV_LORA, 4 * tq), F32)],
        compiler_params=_cparams(("arbitrary", "arbitrary")),
        name="mla_prompt",
    )(q, k3, vt3, wuvt)


def _cmp_prompt_kernel(x_ref, w_ref, p_ref):
    n_chunk = p_ref.shape[0]
    acc = jnp.zeros(p_ref.shape, F32)
    for r in range(NSA_CMP_STRIDE):
        acc = acc + _bdot(x_ref[pl.ds(r, n_chunk, stride=NSA_CMP_STRIDE), :], w_ref[r])
    p_ref[...] = acc


def _cmp_prompt(nsa4, w4, nb):
    N = nsa4.shape[0]
    S = N // nb
    n_chunk = S // NSA_CMP_STRIDE
    return pl.pallas_call(
        _cmp_prompt_kernel,
        grid=(nb,),
        in_specs=[pl.BlockSpec((S, 128), lambda b: (b, 0)),
                  pl.BlockSpec((NSA_CMP_STRIDE, 128, 256), lambda b: (0, 0, 0))],
        out_specs=pl.BlockSpec((None, n_chunk, 256), lambda b: (b, 0, 0)),
        out_shape=jax.ShapeDtypeStruct((nb, n_chunk, 256), F32),
        compiler_params=_cparams(("parallel",)),
        name="cmp_prompt",
    )(nsa4, w4)


def _cmp_sample_kernel(pt_ref, *refs):
    pages, w_ref, p_ref = refs[:-2], refs[-2], refs[-1]
    rows = p_ref.shape[0] // len(pages)
    acc = jnp.zeros(p_ref.shape, F32)
    for r in range(NSA_CMP_STRIDE):
        x = jnp.concatenate([pg[pl.ds(r, rows, stride=NSA_CMP_STRIDE), :] for pg in pages], axis=0)
        acc = acc + _bdot(x, w_ref[r])
    p_ref[...] = acc


def _page_specs(layer, n_pages, pps, width, lane_block):
    def spec(j):
        return pl.BlockSpec((None, None, 128, width),
                            lambda b, c, pt: (layer, pt[b * n_pages + c * pps + j], 0, lane_block))
    return [spec(j) for j in range(pps)]


def _cmp_sample(cache_nsa, pt_flat, w4, layer, nb, n_pages, pps):
    page = cache_nsa.shape[2]
    cpp = page // NSA_CMP_STRIDE
    nch = n_pages // pps
    grid_spec = pltpu.PrefetchScalarGridSpec(
        num_scalar_prefetch=1,
        grid=(nb, nch),
        in_specs=_page_specs(layer, n_pages, pps, 128, 0)
        + [pl.BlockSpec((NSA_CMP_STRIDE, 128, 256), lambda b, c, pt: (0, 0, 0))],
        out_specs=pl.BlockSpec((None, pps * cpp, 256), lambda b, c, pt: (b, c, 0)),
    )
    return pl.pallas_call(
        _cmp_sample_kernel,
        grid_spec=grid_spec,
        out_shape=jax.ShapeDtypeStruct((nb, n_pages * cpp, 256), F32),
        compiler_params=_cparams(("parallel", "arbitrary")),
        name="cmp_sample",
    )(pt_flat, *([cache_nsa] * pps), w4)


def _nsa_prompt_kernel(q_ref, g_ref, p_ref, pe_ref, ov_ref, ks_ref, vst_ref, kw_ref, vwt_ref, o_ref,
                       bias_ref, m_ref, l_ref, acc_ref, *, nc, n_sel):
    i = pl.program_id(1)
    _, tq, _ = q_ref.shape
    tk = ks_ref.shape[1]
    tw = kw_ref.shape[1]
    R = NSA_HEADS * tq
    q = q_ref[...].reshape(R, NSA_HD).astype(BF16)
    q_pos_q = i * tq + lax.broadcasted_iota(jnp.int32, (1, tq), 1)
    q_pos = jnp.concatenate([q_pos_q] * NSA_HEADS, axis=1)

    kcv = _compressed_kv(p_ref, pe_ref, nc)
    o_c, p_c = _compressed_branch(q, kcv, q_pos, nc)
    imp4 = jnp.dot(ov_ref[...], p_c.astype(BF16), preferred_element_type=F32)
    imp = imp4[:, 0:tq]
    for hd in range(1, NSA_HEADS):
        imp = imp + imp4[:, hd * tq:(hd + 1) * tq]
    bias_ref[...] = _select_blocks(imp, q_pos_q, n_sel)

    bpt = tk // NSA_SEL_BLOCK
    n_full = (i * tq) // tk

    def tile_bias(j):
        rows = [jnp.broadcast_to(bias_ref[pl.ds(j * bpt + c, 1), :], (NSA_SEL_BLOCK, tq)) for c in range(bpt)]
        b = jnp.concatenate(rows, axis=0)
        return jnp.concatenate([b] * NSA_HEADS, axis=1)

    _init_stats(m_ref, l_ref, acc_ref)

    def full_step(j, c):
        _online_update(_dot_nt(ks_ref[j], q) + tile_bias(j), vst_ref[j], m_ref, l_ref, acc_ref)
        return c

    lax.fori_loop(0, n_full, full_step, 0)
    s = _dot_nt(ks_ref[n_full], q) + tile_bias(n_full)
    k_pos = n_full * tk + lax.broadcasted_iota(jnp.int32, (tk, R), 0)
    _online_update(jnp.where(k_pos <= q_pos, s, NEG), vst_ref[n_full], m_ref, l_ref, acc_ref)
    o_s = _finish(l_ref, acc_ref)

    _init_stats(m_ref, l_ref, acc_ref)
    q0 = i * tq

    def win_step(j, c):
        k_pos = j * tw + lax.broadcasted_iota(jnp.int32, (tw, R), 0)
        ok = (k_pos <= q_pos) & (k_pos > q_pos - NSA_WINDOW)
        _online_update(jnp.where(ok, _dot_nt(kw_ref[j], q), NEG), vwt_ref[j], m_ref, l_ref, acc_ref)
        return c

    lax.fori_loop(jnp.maximum(q0 - NSA_WINDOW, 0) // tw, (q0 + tq - 1) // tw + 1, win_step, 0)
    o_w = _finish(l_ref, acc_ref)

    g_t = g_ref[...].T
    outs = []
    for hd in range(NSA_HEADS):
        sl = slice(hd * tq, (hd + 1) * tq)
        r0 = MISC_NG + 3 * hd
        outs.append(g_t[r0:r0 + 1, :] * o_c[:, sl] + g_t[r0 + 1:r0 + 2, :] * o_s[:, sl]
                    + g_t[r0 + 2:r0 + 3, :] * o_w[:, sl])
    o_ref[...] = jnp.concatenate(outs, axis=0).T


def _nsa_prompt(q, gates, P, pe, ov, ks3, vst3, kw3, vwt3, *, tq, nc, n_sel):
    nb, nk, tk, _ = ks3.shape
    nw, tw = kw3.shape[1:3]
    N = q.shape[1]
    nq = N // nb // tq
    ncp = P.shape[1]
    nblk = ov.shape[0]
    R = 4 * tq
    return pl.pallas_call(
        functools.partial(_nsa_prompt_kernel, nc=nc, n_sel=n_sel),
        grid=(nb, nq),
        in_specs=[pl.BlockSpec((4, tq, NSA_HD), lambda b, i: (0, b * nq + i, 0)),
                  pl.BlockSpec((tq, 128), lambda b, i: (b * nq + i, 0)),
                  pl.BlockSpec((None, ncp, 256), lambda b, i: (b, 0, 0)),
                  pl.BlockSpec((8, 128), lambda b, i: (0, 0)),
                  pl.BlockSpec((nblk, ncp), lambda b, i: (0, 0)),
                  pl.BlockSpec((None, nk, tk, NSA_HD), lambda b, i: (b, 0, 0, 0)),
                  pl.BlockSpec((None, nk, NSA_HD, tk), lambda b, i: (b, 0, 0, 0)),
                  pl.BlockSpec((None, nw, tw, NSA_HD), lambda b, i: (b, 0, 0, 0)),
                  pl.BlockSpec((None, nw, NSA_HD, tw), lambda b, i: (b, 0, 0, 0))],
        out_specs=pl.BlockSpec((tq, 256), lambda b, i: (b * nq + i, 0)),
        out_shape=jax.ShapeDtypeStruct((N, 256), F32),
        scratch_shapes=[pltpu.VMEM((nblk, tq), F32), pltpu.VMEM((1, R), F32), pltpu.VMEM((1, R), F32),
                        pltpu.VMEM((NSA_HD, R), F32)],
        compiler_params=_cparams(("arbitrary", "arbitrary")),
        name="nsa_prompt",
    )(q, gates, P, pe, ov, ks3, vst3, kw3, vwt3)


def _sample_queries(q_ref):
    nh, T, d = q_ref.shape
    q = q_ref[...].reshape(nh * T, d)
    return jnp.concatenate([q, jnp.zeros((SAMPLE_R - nh * T, d), F32)], axis=0).astype(BF16)


def _mla_sample_kernel(pt_ref, *refs, pps):
    pages = refs[:pps]
    q_ref, new_ref, wuv_ref, o_ref, m_ref, l_ref, acc_ref = refs[pps:]
    c = pl.program_id(1)
    _, T, dk = q_ref.shape
    R = SAMPLE_R
    q = _sample_queries(q_ref)

    @pl.when(c == 0)
    def _():
        _init_stats(m_ref, l_ref, acc_ref)

    kv = jnp.concatenate([pg[...].astype(BF16) for pg in pages], axis=0)
    _online_update_tn(_dot_nt(kv, q), kv[:, 0:MLA_KV_LORA], m_ref, l_ref, acc_ref)

    @pl.when(c == pl.num_programs(1) - 1)
    def _():
        new = new_ref[...].astype(BF16)
        k_t = lax.broadcasted_iota(jnp.int32, (T, R), 0)
        q_t = lax.broadcasted_iota(jnp.int32, (T, R), 1) % T
        _online_update_tn(jnp.where(k_t <= q_t, _dot_nt(new, q), NEG), new[:, 0:MLA_KV_LORA], m_ref, l_ref, acc_ref)
        o = _finish(l_ref, acc_ref).T
        o_ref[...] = jnp.concatenate([_bdot(o[hd * T:(hd + 1) * T, :], wuv_ref[hd]) for hd in range(MLA_HEADS)],
                                     axis=1)


def _mla_sample(cache_mla, pt_flat, q, rows, wuv, layer, nb, n_pages, pps):
    T = q.shape[1] // nb
    N = q.shape[1]
    nch = n_pages // pps
    R = SAMPLE_R
    grid_spec = pltpu.PrefetchScalarGridSpec(
        num_scalar_prefetch=1,
        grid=(nb, nch),
        in_specs=_page_specs(layer, n_pages, pps, MLA_CACHE_W, 0)
        + [pl.BlockSpec((4, T, MLA_CACHE_W), lambda b, c, pt: (0, b, 0)),
           pl.BlockSpec((T, MLA_CACHE_W), lambda b, c, pt: (b, 0)),
           pl.BlockSpec((4, MLA_KV_LORA, MLA_V), lambda b, c, pt: (0, 0, 0))],
        out_specs=pl.BlockSpec((T, 256), lambda b, c, pt: (b, 0)),
        scratch_shapes=[pltpu.VMEM((1, R), F32), pltpu.VMEM((1, R), F32), pltpu.VMEM((MLA_KV_LORA, R), F32)],
    )
    return pl.pallas_call(
        functools.partial(_mla_sample_kernel, pps=pps),
        grid_spec=grid_spec,
        out_shape=jax.ShapeDtypeStruct((N, 256), F32),
        compiler_params=_cparams(("parallel", "arbitrary")),
        name="mla_sample",
    )(pt_flat, *([cache_mla] * pps), q, rows, wuv)


def _nsa_sample_kernel(pt_ref, *refs, pps, nc, n_sel, past):
    pages = refs[:pps]
    (q_ref, g_ref, p_ref, pe_ref, ov_ref, new_ref, wc_ref, wn_ref, o_ref,
     bias_ref, oc_ref, ow_ref, m_ref, l_ref, acc_ref) = refs[pps:]
    c = pl.program_id(1)
    _, T, _ = q_ref.shape
    R = SAMPLE_R
    HT = NSA_HEADS * T
    q = _sample_queries(q_ref)
    lane = lax.broadcasted_iota(jnp.int32, (1, R), 1)
    q_pos = past + lane % T

    @pl.when(c == 0)
    def _():
        kcv = _compressed_kv(p_ref, pe_ref, nc)
        o_c, p_c = _compressed_branch(q, kcv, q_pos, nc)
        oc_ref[...] = o_c
        imp = jnp.dot(ov_ref[...], p_c.astype(BF16), preferred_element_type=F32)
        imp = jnp.where(lane < HT, imp, 0.0)
        tot = imp
        for k in range(1, NSA_HEADS):
            tot = tot + pltpu.roll(imp, k * T, 1) + pltpu.roll(imp, R - HT + k * T, 1)
        bias_ref[...] = _select_blocks(tot, q_pos, n_sel)
        _init_stats(m_ref, l_ref, acc_ref)
        wc = wc_ref[...].astype(BF16)
        n_buf = wc.shape[0]
        k_pos = past - n_buf + lax.broadcasted_iota(jnp.int32, (n_buf, R), 0)
        ok = (k_pos >= 0) & (k_pos <= q_pos) & (k_pos > q_pos - NSA_WINDOW)
        _online_update_tn(jnp.where(ok, _dot_nt(wc[:, 0:64], q), NEG), wc[:, 64:128], m_ref, l_ref, acc_ref)
        wn = wn_ref[...].astype(BF16)
        k_pos = past + lax.broadcasted_iota(jnp.int32, (T, R), 0)
        ok = (k_pos <= q_pos) & (k_pos > q_pos - NSA_WINDOW)
        _online_update_tn(jnp.where(ok, _dot_nt(wn[:, 0:64], q), NEG), wn[:, 64:128], m_ref, l_ref, acc_ref)
        ow_ref[...] = _finish(l_ref, acc_ref)
        _init_stats(m_ref, l_ref, acc_ref)

    kv = jnp.concatenate([pg[...].astype(BF16) for pg in pages], axis=0)
    bpp = pages[0].shape[0] // NSA_SEL_BLOCK
    nbk = pps * bpp
    rows = [jnp.broadcast_to(bias_ref[pl.ds(c * nbk + j, 1), :], (NSA_SEL_BLOCK, R)) for j in range(nbk)]
    _online_update_tn(_dot_nt(kv[:, 0:64], q) + jnp.concatenate(rows, axis=0), kv[:, 64:128], m_ref, l_ref, acc_ref)

    @pl.when(c == pl.num_programs(1) - 1)
    def _():
        new = new_ref[...].astype(BF16)
        k_pos = past + lax.broadcasted_iota(jnp.int32, (T, R), 0)
        blk_bias = bias_ref[pl.ds(past // NSA_SEL_BLOCK, 1), :]
        s = jnp.where(k_pos <= q_pos, _dot_nt(new[:, 128:192], q) + blk_bias, NEG)
        _online_update_tn(s, new[:, 192:256], m_ref, l_ref, acc_ref)
        o_c = oc_ref[...].T
        o_s = _finish(l_ref, acc_ref).T
        o_w = ow_ref[...].T
        g = g_ref[...]
        outs = []
        for hd in range(NSA_HEADS):
            sl = slice(hd * T, (hd + 1) * T)
            r0 = MISC_NG + 3 * hd
            outs.append(g[:, r0:r0 + 1] * o_c[sl] + g[:, r0 + 1:r0 + 2] * o_s[sl] + g[:, r0 + 2:r0 + 3] * o_w[sl])
        o_ref[...] = jnp.concatenate(outs, axis=1)


def _nsa_sample(cache_nsa, pt_flat, q, gates, P, pe, ov, new4, win_cache, win_new, layer, nb, n_pages, pps,
                *, nc, n_sel, past):
    N = q.shape[1]
    T = N // nb
    nch = n_pages // pps
    R = SAMPLE_R
    ncp = P.shape[1]
    nblk = ov.shape[0]
    n_buf = win_cache.shape[2]
    grid_spec = pltpu.PrefetchScalarGridSpec(
        num_scalar_prefetch=1,
        grid=(nb, nch),
        in_specs=_page_specs(layer, n_pages, pps, 128, 1)
        + [pl.BlockSpec((4, T, NSA_HD), lambda b, c, pt: (0, b, 0)),
           pl.BlockSpec((T, 128), lambda b, c, pt: (b, 0)),
           pl.BlockSpec((None, ncp, 256), lambda b, c, pt: (b, 0, 0)),
           pl.BlockSpec((8, 128), lambda b, c, pt: (0, 0)),
           pl.BlockSpec((nblk, ncp), lambda b, c, pt: (0, 0)),
           pl.BlockSpec((T, 256), lambda b, c, pt: (b, 0)),
           pl.BlockSpec((None, None, n_buf, 128), lambda b, c, pt: (layer, b, 0, 0)),
           pl.BlockSpec((T, 128), lambda b, c, pt: (b, 0))],
        out_specs=pl.BlockSpec((T, 256), lambda b, c, pt: (b, 0)),
        scratch_shapes=[pltpu.VMEM((nblk, R), F32), pltpu.VMEM((NSA_HD, R), F32), pltpu.VMEM((NSA_HD, R), F32),
                        pltpu.VMEM((1, R), F32), pltpu.VMEM((1, R), F32), pltpu.VMEM((NSA_HD, R), F32)],
    )
    return pl.pallas_call(
        functools.partial(_nsa_sample_kernel, pps=pps, nc=nc, n_sel=n_sel, past=past),
        grid_spec=grid_spec,
        out_shape=jax.ShapeDtypeStruct((N, 256), F32),
        compiler_params=_cparams(("parallel", "arbitrary")),
        name="nsa_sample",
    )(pt_flat, *([cache_nsa] * pps), q, gates, P, pe, ov, new4, win_cache, win_new)


def _block_diag(w):
    n, a, b = w.shape
    out = jnp.zeros((n * a, n * b), w.dtype)
    for j in range(n):
        out = out.at[j * a:(j + 1) * a, j * b:(j + 1) * b].set(w[j])
    return out


def _rot_cols(w):
    half = w.shape[-1] // 2
    return jnp.concatenate([-w[..., half:], w[..., :half]], axis=-1)


def _layer_params(l, W):
    idx = np.cumsum(IN_SPLITS)[:-1].tolist()
    lru_x, lru_g, conf_u, cq, ckv, kr, nq, nkv, ng = jnp.split(W['w_in'][l], idx, axis=-1)
    pad = jnp.zeros((D_MODEL, D_INP - C_MISC - 2 * MLA_ROPE - 3 * NSA_HEADS), F32)
    w_in = jnp.concatenate([lru_x, lru_g, conf_u, cq, ckv, nq, nkv, kr, _rot_cols(kr), ng, pad], axis=-1)
    wuq = W['mla_wuq'][l].reshape(MLA_Q_LORA, MLA_HEADS, MLA_NOPE + MLA_ROPE)
    wuq_rope = wuq[:, :, MLA_NOPE:]
    wuq = jnp.concatenate([wuq[:, :, :MLA_NOPE].reshape(MLA_Q_LORA, -1), wuq_rope.reshape(MLA_Q_LORA, -1),
                           _rot_cols(wuq_rope).reshape(MLA_Q_LORA, -1)], axis=-1)
    wuk = _block_diag(jnp.transpose(W['mla_wuk'][l], (1, 2, 0)))
    wuv = jnp.transpose(W['mla_wuv'][l], (1, 0, 2))
    wk = W['nsa_wc_k'][l].reshape(NSA_CMP_BLOCK, NSA_HD, NSA_HD)
    wv = W['nsa_wc_v'][l].reshape(NSA_CMP_BLOCK, NSA_HD, NSA_HD)
    z = jnp.zeros((NSA_CMP_STRIDE, NSA_HD, NSA_HD), F32)
    w4 = jnp.concatenate([jnp.concatenate([wk[:16], z, wk[16:], z], axis=2),
                          jnp.concatenate([z, wv[:16], z, wv[16:]], axis=2)], axis=1)
    pe = jnp.concatenate([W['nsa_pe_k'][l], W['nsa_pe_v'][l]], axis=-1)
    pe2 = jnp.concatenate([pe[:16, None], pe[16:, None], jnp.zeros((16, 6, 128), F32)], axis=1)
    r1 = lambda a: a.reshape(1, -1)
    return dict(
        w_in=w_in.astype(BF16), g_pre_mix=r1(W['g_pre_mix'][l]), g_post_mix=r1(W['g_post_mix'][l]),
        g_pre_ffn=r1(W['g_pre_ffn'][l]), g_post_ffn=r1(W['g_post_ffn'][l]),
        mla_gq=r1(W['mla_gq'][l]), mla_gkv=r1(W['mla_gkv'][l]),
        wuq=wuq.astype(BF16), wuk=wuk.astype(BF16), wuv=wuv.astype(BF16),
        wuvt=jnp.transpose(wuv, (0, 2, 1)).astype(BF16),
        w4=w4.astype(BF16), pe2=pe2,
        lru_conv_w=W['lru_conv_w'][l], lru_conv_b=r1(W['lru_conv_b'][l]),
        lru_wa_bd=_block_diag(W['lru_wa'][l]).astype(BF16), lru_ba=r1(W['lru_ba'][l]),
        lru_wx_bd=_block_diag(W['lru_wx'][l]).astype(BF16), lru_bx=r1(W['lru_bx'][l]),
        lru_lam=r1(W['lru_lam'][l]),
        conf_dw_w=W['conf_dw_w'][l], conf_dw_b=r1(W['conf_dw_b'][l]),
        conf_ln_g=r1(W['conf_ln_g'][l]), conf_ln_b=r1(W['conf_ln_b'][l]),
        w_out=W['w_out'][l].astype(BF16), w_ff1=W['w_ff1'][l].astype(BF16), w_ff2=W['w_ff2'][l].astype(BF16),
    )


def _rope_tables(pos):
    half = MLA_ROPE // 2
    freq = jnp.power(ROPE_THETA, -jnp.arange(half, dtype=F32) / half)
    ang = pos.astype(F32)[:, None] * freq
    cosv = jnp.tile(jnp.cos(ang), (1, 2 * MLA_HEADS))
    sinv = jnp.tile(jnp.sin(ang), (1, 2 * MLA_HEADS))
    return cosv, sinv


def _overlap_t(nblk_pad, ncp, nc):
    n = np.arange(ncp)[None, :]
    j = np.arange(nblk_pad)[:, None]
    ov = (n * NSA_CMP_STRIDE < j * NSA_SEL_BLOCK + NSA_SEL_BLOCK) & (n * NSA_CMP_STRIDE + NSA_CMP_BLOCK > j * NSA_SEL_BLOCK)
    return jnp.asarray((ov & (n < nc)).astype(np.float32), dtype=BF16)


def _tiles(a, nb, t):
    k = a.reshape(nb, -1, t, a.shape[-1])
    return k, jnp.swapaxes(k, 2, 3)


def kernel(x_prompt, x_sample, c_prompt, c_sample, state_lru_h, state_lru_conv, state_conv, cache_mla, cache_nsa,
           cache_nsa_win, page_table, w_mod, b_mod, g_pre_mix, g_post_mix, g_pre_ffn, g_post_ffn, w_in, lru_conv_w,
           lru_conv_b, lru_wa, lru_ba, lru_wx, lru_bx, lru_lam, conf_dw_w, conf_dw_b, conf_ln_g, conf_ln_b, mla_gq,
           mla_wuq, mla_gkv, mla_wuk, mla_wuv, nsa_pe_k, nsa_wc_k, nsa_pe_v, nsa_wc_v, w_out, w_ff1, w_ff2):
    W = dict(w_in=w_in, g_pre_mix=g_pre_mix, g_post_mix=g_post_mix, g_pre_ffn=g_pre_ffn, g_post_ffn=g_post_ffn,
             lru_conv_w=lru_conv_w, lru_conv_b=lru_conv_b, lru_wa=lru_wa, lru_ba=lru_ba, lru_wx=lru_wx, lru_bx=lru_bx,
             lru_lam=lru_lam, conf_dw_w=conf_dw_w, conf_dw_b=conf_dw_b, conf_ln_g=conf_ln_g, conf_ln_b=conf_ln_b,
             mla_gq=mla_gq, mla_wuq=mla_wuq, mla_gkv=mla_gkv, mla_wuk=mla_wuk, mla_wuv=mla_wuv,
             nsa_pe_k=nsa_pe_k, nsa_wc_k=nsa_wc_k, nsa_pe_v=nsa_pe_v, nsa_wc_v=nsa_wc_v,
             w_out=w_out, w_ff1=w_ff1, w_ff2=w_ff2)
    L = w_in.shape[0]
    nbp, S, D = x_prompt.shape
    nbs, T, _ = x_sample.shape
    n_pages = page_table.shape[1]
    page = cache_mla.shape[2]
    past = n_pages * page
    Np, Ns = nbp * S, nbs * T
    pps = min(PAGES_PER_STEP, n_pages)
    tq = min(Q_TILE, S)
    tk = min(K_TILE, S)
    tm_p = min(256, Np)
    tm_s = min(256, Ns)
    tc = min(512, S)

    nc_p = S // NSA_CMP_STRIDE - 1
    nblk_p = -(-S // NSA_SEL_BLOCK)
    ov_p = _overlap_t(nblk_p, S // NSA_CMP_STRIDE, nc_p)
    len_s = past + T
    assert len_s // NSA_CMP_STRIDE == past // NSA_CMP_STRIDE and past % NSA_SEL_BLOCK == 0 and T <= NSA_SEL_BLOCK
    nc_s = len_s // NSA_CMP_STRIDE - 1
    nblk_s = -(-len_s // NSA_SEL_BLOCK)
    nblk_s_pad = -(-nblk_s // 8) * 8
    ov_s = _overlap_t(nblk_s_pad, past // NSA_CMP_STRIDE, nc_s)

    rows = nbp + nbs
    rows_pad = -(-rows // 8) * 8
    c_all = jnp.concatenate([c_prompt, c_sample, jnp.zeros((rows_pad - rows, D), F32)], axis=0)
    mod = _modulation(c_all, w_mod.astype(BF16), b_mod)

    params = [_layer_params(l, W) for l in range(L)]
    pe_all = _pe_term(jnp.stack([p['pe2'] for p in params]), jnp.stack([p['w4'] for p in params]))

    cos_p, sin_p = _rope_tables(jnp.arange(S, dtype=jnp.int32))
    cos_s, sin_s = _rope_tables(past + jnp.arange(T, dtype=jnp.int32))
    cos_s = jnp.tile(cos_s, (nbs, 1))
    sin_s = jnp.tile(sin_s, (nbs, 1))
    pt_flat = page_table.reshape(-1).astype(jnp.int32)
    cache_nsa2 = cache_nsa.reshape(L, cache_nsa.shape[1], page, 4 * NSA_HD)
    win_cache = cache_nsa_win.reshape(L, nbs, cache_nsa_win.shape[2], 2 * NSA_HD)

    xp = x_prompt.reshape(Np, D)
    xs = x_sample.reshape(Ns, D)
    outs_p, outs_s = [], []
    for l in range(L):
        P = params[l]
        pe = pe_all[l]
        mp = [m.reshape(nbp, 1, D) for m in jnp.split(mod[l, :nbp], 6, axis=-1)]
        ms = [jnp.repeat(m, T, axis=0) for m in jnp.split(mod[l, nbp:rows], 6, axis=-1)]

        lru, glu, qm, row, nq, nsa4, win, gates = _in_proj(
            xp, mp[0], mp[1], P['g_pre_mix'], P['w_in'], P['mla_gq'], P['wuq'], P['wuk'], P['mla_gkv'],
            cos_p, sin_p, per_token=False, tm=tm_p)
        a_out, b_out, h_last = _seq_prompt(lru.reshape(nbp, S, 512), glu.reshape(nbp, S, 256), P, tc=tc)
        rowb = row.astype(BF16)
        k3 = rowb.reshape(nbp, S // tk, tk, MLA_CACHE_W)
        _, vt3 = _tiles(rowb[:, :MLA_KV_LORA], nbp, tk)
        c_out = _mla_prompt(qm, k3, vt3, P['wuvt'], tq=tq)
        Pp = _cmp_prompt(nsa4, P['w4'], nbp)
        nsab = nsa4.astype(BF16)
        ks3, _ = _tiles(nsab[:, 128:192], nbp, tk)
        _, vst3 = _tiles(nsab[:, 192:256], nbp, tk)
        winb = win.astype(BF16)
        kw3, _ = _tiles(winb[:, 0:64], nbp, tq)
        _, vwt3 = _tiles(winb[:, 64:128], nbp, tq)
        d_out = _nsa_prompt(nq, gates, Pp, pe, ov_p, ks3, vst3, kw3, vwt3, tq=tq, nc=nc_p,
                            n_sel=min(NSA_N_SEL, nblk_p))
        xp = _post(xp, a_out.reshape(Np, 256), b_out.reshape(Np, 256), c_out, d_out, mp[2], mp[3], mp[4], mp[5],
                   P['w_out'], P['g_post_mix'], P['g_pre_ffn'], P['g_post_ffn'], P['w_ff1'], P['w_ff2'],
                   per_token=False, tm=tm_p)
        n_win = min(NSA_WINDOW, S)
        outs_p.append((h_last[:, 0], lru.reshape(nbp, S, 512)[:, S - (LRU_CONV - 1):, :D_LRU],
                       glu.reshape(nbp, S, 256)[:, S - (CONF_KERNEL - 1):],
                       row.reshape(nbp, S, MLA_CACHE_W), nsa4.reshape(nbp, S, 4, NSA_HD),
                       win.reshape(nbp, S, 2, NSA_HD)[:, S - n_win:]))

        lru, glu, qm, row, nq, nsa4, win, gates = _in_proj(
            xs, ms[0], ms[1], P['g_pre_mix'], P['w_in'], P['mla_gq'], P['wuq'], P['wuk'], P['mla_gkv'],
            cos_s, sin_s, per_token=True, tm=tm_s)
        tmaj = lambda a: jnp.swapaxes(a, 0, 1)
        a_t, b_t, h_last, ncs, ngs = _seq_sample(
            tmaj(lru.reshape(nbs, T, 512)), tmaj(glu.reshape(nbs, T, 256)), tmaj(state_lru_conv[l]),
            tmaj(state_conv[l]), state_lru_h[l], P)
        c_out = _mla_sample(cache_mla, pt_flat, qm, row, P['wuv'], l, nbs, n_pages, pps)
        Ps = _cmp_sample(cache_nsa2, pt_flat, P['w4'], l, nbs, n_pages, pps)
        d_out = _nsa_sample(cache_nsa2, pt_flat, nq, gates, Ps, pe, ov_s, nsa4, win_cache, win, l, nbs, n_pages, pps,
                            nc=nc_s, n_sel=min(NSA_N_SEL, nblk_s), past=past)
        xs = _post(xs, tmaj(a_t).reshape(Ns, 256), tmaj(b_t).reshape(Ns, 256), c_out, d_out, ms[2], ms[3], ms[4],
                   ms[5], P['w_out'], P['g_post_mix'], P['g_pre_ffn'], P['g_post_ffn'], P['w_ff1'], P['w_ff2'],
                   per_token=True, tm=tm_s)
        wkv = jnp.concatenate([cache_nsa_win[l], win.reshape(nbs, T, 2, NSA_HD)], axis=1)
        n_win = min(NSA_WINDOW, past + T)
        outs_s.append((h_last, tmaj(ncs), tmaj(ngs), row.reshape(nbs, T, MLA_CACHE_W),
                       nsa4.reshape(nbs, T, 4, NSA_HD), wkv[:, -n_win:]))

    stack = lambda outs, i: jnp.stack([o[i] for o in outs])
    return (xp.reshape(nbp, S, D), xs.reshape(nbs, T, D),
            stack(outs_p, 0), stack(outs_s, 0), stack(outs_p, 1), stack(outs_s, 1), stack(outs_p, 2), stack(outs_s, 2),
            stack(outs_p, 3), stack(outs_s, 3), stack(outs_p, 4), stack(outs_s, 4), stack(outs_p, 5), stack(outs_s, 5))
```

```python
import functools

import numpy as np
import jax
import jax.numpy as jnp
from jax import lax
from jax.experimental import pallas as pl
from jax.experimental.pallas import tpu as pltpu

F32 = jnp.float32
BF16 = jnp.bfloat16

D_MODEL = 1024
D_LRU = 256
LRU_BLOCKS = 4
LRU_CONV = 4
LRU_C = 8.0
D_CONF = 256
CONF_KERNEL = 31
MLA_HEADS = 4
MLA_NOPE = 64
MLA_ROPE = 32
MLA_V = 64
MLA_Q_LORA = 256
MLA_KV_LORA = 128
MLA_CACHE_W = MLA_KV_LORA + MLA_ROPE
ROPE_THETA = 10000.0
NSA_HEADS = 4
NSA_HD = 64
NSA_CMP_BLOCK = 32
NSA_CMP_STRIDE = 16
NSA_SEL_BLOCK = 64
NSA_N_SEL = 16
NSA_WINDOW = 512
NSA_FORCE = 1.0e4
D_FF = 4 * D_MODEL
EPS = 1e-6
IN_SPLITS = (D_LRU, D_LRU, 2 * D_CONF, MLA_Q_LORA, MLA_KV_LORA, MLA_ROPE, NSA_HEADS * NSA_HD, 6 * NSA_HD, 3 * NSA_HEADS)

C_LRU = 0
C_CONF = 512
C_CQ = 1024
C_CKV = 1280
C_NQ = 1408
C_NKV = 1664
C_MISC = 2048
D_INP = 2176
MISC_NG = 64

NEG = -1e30
LOG2E = 1.4426950408889634
VSUM_ROWS = 8
VMEM_LIMIT_V7X = 56 * 1024 * 1024
Q_TILE = 128
K_TILE = 512
PAGES_PER_STEP = 64
SAMPLE_R = 128


def _cparams(sem):
    return pltpu.CompilerParams(dimension_semantics=sem, vmem_limit_bytes=VMEM_LIMIT_V7X)


def _rms(x, g):
    return x * lax.rsqrt(jnp.mean(x * x, axis=-1, keepdims=True) + EPS) * g


def _bdot(a, b):
    return jnp.dot(a.astype(BF16), b.astype(BF16), preferred_element_type=F32)


def _dot_nt(a, b):
    return lax.dot_general(a.astype(BF16), b.astype(BF16), (((1,), (1,)), ((), ())), preferred_element_type=F32)


def _dot_tn(a, b):
    return lax.dot_general(a.astype(BF16), b.astype(BF16), (((0,), (0,)), ((), ())), preferred_element_type=F32)


def _mod_kernel(c_ref, w_ref, b_ref, o_ref):
    o_ref[...] = _bdot(jax.nn.silu(c_ref[...]), w_ref[...]) + b_ref[...]


def _modulation(c_all, w_mod, b_mod):
    L, D, D6 = w_mod.shape
    rows = c_all.shape[0]
    tn = 1536
    return pl.pallas_call(
        _mod_kernel,
        grid=(L, D6 // tn),
        in_specs=[pl.BlockSpec((rows, D), lambda l, j: (0, 0)),
                  pl.BlockSpec((None, D, tn), lambda l, j: (l, 0, j)),
                  pl.BlockSpec((None, 1, tn), lambda l, j: (l, 0, j))],
        out_specs=pl.BlockSpec((None, rows, tn), lambda l, j: (l, 0, j)),
        out_shape=jax.ShapeDtypeStruct((L, rows, D6), F32),
        compiler_params=_cparams(("arbitrary", "arbitrary")),
        name="modulation",
    )(c_all, w_mod, b_mod.reshape(L, 1, D6))


def _pe_kernel(pe_ref, w_ref, o_ref):
    acc = jnp.zeros((8, 256), F32)
    for r in range(NSA_CMP_STRIDE):
        acc = acc + _bdot(pe_ref[r], w_ref[r])
    o_ref[...] = jnp.broadcast_to(acc[0:1, 0:128] + acc[1:2, 128:256], (8, 128))


def _pe_term(pe2, w4):
    L = pe2.shape[0]
    return pl.pallas_call(
        _pe_kernel,
        grid=(L,),
        in_specs=[pl.BlockSpec((None, NSA_CMP_STRIDE, 8, 128), lambda l: (l, 0, 0, 0)),
                  pl.BlockSpec((None, NSA_CMP_STRIDE, 128, 256), lambda l: (l, 0, 0, 0))],
        out_specs=pl.BlockSpec((None, 8, 128), lambda l: (l, 0, 0)),
        out_shape=jax.ShapeDtypeStruct((L, 8, 128), F32),
        compiler_params=_cparams(("arbitrary",)),
        name="pe_term",
    )(pe2, w4)


def _in_proj_kernel(x_ref, sh_ref, sc_ref, g_ref, w_ref, gq_ref, wuq_ref, wuk_ref, gkv_ref, cos_ref, sin_ref,
                    lru_ref, glu_ref, qm_ref, row_ref, nq_ref, nsa4_ref, win_ref, gate_ref):
    h = _rms(x_ref[...], g_ref[...]) * (1.0 + sc_ref[...]) + sh_ref[...]
    u = _bdot(h, w_ref[...])
    lru_ref[...] = u[:, C_LRU:C_LRU + 512]
    glu_ref[...] = u[:, C_CONF:C_CONF + 256] * jax.nn.sigmoid(u[:, C_CONF + 256:C_CONF + 512])
    nsa4_ref[...] = u[:, C_NKV:C_NKV + 256]
    win_ref[...] = u[:, C_NKV + 256:C_NKV + 384]
    misc = u[:, C_MISC:C_MISC + 128]
    gate_ref[...] = jax.nn.sigmoid(misc)
    cosv = cos_ref[...]
    sinv = sin_ref[...]
    qr = _bdot(_rms(u[:, C_CQ:C_CQ + 256], gq_ref[...]), wuq_ref[...])
    q_rope = qr[:, 256:384] * cosv + qr[:, 384:512] * sinv
    q_lat = _bdot(qr[:, 0:256], wuk_ref[...])
    scale = (MLA_NOPE + MLA_ROPE) ** -0.5 * LOG2E
    for hd in range(MLA_HEADS):
        qm_ref[hd, :, 0:128] = q_lat[:, 128 * hd:128 * (hd + 1)] * scale
        qm_ref[hd, :, 128:160] = q_rope[:, 32 * hd:32 * (hd + 1)] * scale
    row_ref[:, 0:128] = _rms(u[:, C_CKV:C_CKV + 128], gkv_ref[...])
    row_ref[:, 128:160] = misc[:, 0:32] * cosv[:, 0:32] + misc[:, 32:64] * sinv[:, 0:32]
    nq = u[:, C_NQ:C_NQ + 256] * (NSA_HD ** -0.5 * LOG2E)
    for hd in range(NSA_HEADS):
        nq_ref[hd] = nq[:, 64 * hd:64 * (hd + 1)]


def _in_proj(x, sh, sc, g, w_in, gq, wuq, wuk, gkv, cosv, sinv, *, per_token, tm):
    N, D = x.shape
    nt = N // tm
    if per_token:
        mod_spec = pl.BlockSpec((tm, D), lambda i: (i, 0))
        tab_spec = pl.BlockSpec((tm, 128), lambda i: (i, 0))
    else:
        per_b = nt // sh.shape[0]
        mod_spec = pl.BlockSpec((None, 1, D), lambda i: (i // per_b, 0, 0))
        tab_spec = pl.BlockSpec((tm, 128), lambda i: (i % per_b, 0))
    const = lambda shape: pl.BlockSpec(shape, lambda i: tuple(0 for _ in shape))
    row = lambda w: pl.BlockSpec((tm, w), lambda i: (i, 0))
    head = lambda w: pl.BlockSpec((4, tm, w), lambda i: (0, i, 0))
    return pl.pallas_call(
        _in_proj_kernel,
        grid=(nt,),
        in_specs=[row(D), mod_spec, mod_spec, const((1, D)), const((D, D_INP)), const((1, 256)),
                  const((256, 512)), const((256, 512)), const((1, 128)), tab_spec, tab_spec],
        out_specs=[row(512), row(256), head(MLA_CACHE_W), row(MLA_CACHE_W), head(NSA_HD), row(256), row(128), row(128)],
        out_shape=[jax.ShapeDtypeStruct((N, 512), F32), jax.ShapeDtypeStruct((N, 256), F32),
                   jax.ShapeDtypeStruct((4, N, MLA_CACHE_W), F32), jax.ShapeDtypeStruct((N, MLA_CACHE_W), F32),
                   jax.ShapeDtypeStruct((4, N, NSA_HD), F32), jax.ShapeDtypeStruct((N, 256), F32),
                   jax.ShapeDtypeStruct((N, 128), F32), jax.ShapeDtypeStruct((N, 128), F32)],
        compiler_params=_cparams(("parallel",)),
        name="in_proj",
    )(x, sh, sc, g, w_in, gq, wuq, wuk, gkv, cosv, sinv)


def _post_kernel(x_ref, a_ref, b_ref, c_ref, d_ref, gt1_ref, sh2_ref, sc2_ref, gt2_ref,
                 wo_ref, gpm_ref, gpf_ref, gqf_ref, w1_ref, w2_ref, o_ref):
    y = _bdot(a_ref[...], wo_ref[0:256, :])
    y = y + _bdot(b_ref[...], wo_ref[256:512, :])
    y = y + _bdot(c_ref[...], wo_ref[512:768, :])
    y = y + _bdot(d_ref[...], wo_ref[768:1024, :])
    x = x_ref[...] + gt1_ref[...] * _rms(y, gpm_ref[...])
    h = (_rms(x, gpf_ref[...]) * (1.0 + sc2_ref[...]) + sh2_ref[...]).astype(BF16)
    f = jnp.zeros(x.shape, F32)
    fc = 1024
    for j in range(D_FF // fc):
        t = jnp.maximum(jnp.dot(h, w1_ref[:, j * fc:(j + 1) * fc], preferred_element_type=F32), 0.0)
        f = f + _bdot(t * t, w2_ref[j * fc:(j + 1) * fc, :])
    o_ref[...] = x + gt2_ref[...] * _rms(f, gqf_ref[...])


def _post(x, a, b, c, d, gt1, sh2, sc2, gt2, wo, gpm, gpf, gqf, w1, w2, *, per_token, tm):
    N, D = x.shape
    nt = N // tm
    if per_token:
        mod_spec = pl.BlockSpec((tm, D), lambda i: (i, 0))
    else:
        per_b = nt // gt1.shape[0]
        mod_spec = pl.BlockSpec((None, 1, D), lambda i: (i // per_b, 0, 0))
    const = lambda shape: pl.BlockSpec(shape, lambda i: tuple(0 for _ in shape))
    row = lambda w: pl.BlockSpec((tm, w), lambda i: (i, 0))
    return pl.pallas_call(
        _post_kernel,
        grid=(nt,),
        in_specs=[row(D), row(256), row(256), row(256), row(256), mod_spec, mod_spec, mod_spec, mod_spec,
                  const((D, D)), const((1, D)), const((1, D)), const((1, D)), const((D, D_FF)), const((D_FF, D))],
        out_specs=row(D),
        out_shape=jax.ShapeDtypeStruct((N, D), F32),
        compiler_params=_cparams(("parallel",)),
        name="post",
    )(x, a, b, c, d, gt1, sh2, sc2, gt2, wo, gpm, gpf, gqf, w1, w2)


def _lru_gates(y, gate_w, lam):
    wa, ba, wx, bx = gate_w
    r = jax.nn.sigmoid(_bdot(y, wa) + ba)
    i = jax.nn.sigmoid(_bdot(y, wx) + bx)
    log_a = -LRU_C * r * jax.nn.softplus(-lam)
    return jnp.exp(log_a), jnp.sqrt(1.0 - jnp.exp(2.0 * log_a)) * i * y


def _layernorm(y, g, b):
    mu = jnp.mean(y, axis=-1, keepdims=True)
    var = jnp.mean(jnp.square(y - mu), axis=-1, keepdims=True)
    return (y - mu) * lax.rsqrt(var + EPS) * g + b


def _seq_prompt_kernel(xg_ref, glu_ref, cw_ref, cb_ref, wa_ref, ba_ref, wx_ref, bx_ref, lam_ref,
                       dw_ref, db_ref, lng_ref, lnb_ref,
                       a_ref, b_ref, hl_ref, xbuf, gbuf, abuf, dbuf, hbuf):
    i = pl.program_id(0)
    nb, tc, _ = glu_ref.shape

    @pl.when(i == 0)
    def _():
        xbuf[:, 0:8, :] = jnp.zeros((nb, 8, D_LRU), F32)
        gbuf[:, 0:32, :] = jnp.zeros((nb, 32, D_CONF), F32)
        hbuf[...] = jnp.zeros(hbuf.shape, F32)

    xbuf[:, 8:8 + tc, :] = xg_ref[:, :, 0:D_LRU]
    y = jnp.zeros((nb, tc, D_LRU), F32) + cb_ref[...]
    for k in range(LRU_CONV):
        y = y + xbuf[:, 8 - (LRU_CONV - 1) + k:8 - (LRU_CONV - 1) + k + tc, :] * cw_ref[k:k + 1, :]
    a, drive = _lru_gates(y.reshape(nb * tc, D_LRU),
                          (wa_ref[...], ba_ref[...], wx_ref[...], bx_ref[...]), lam_ref[...])
    abuf[...] = a.reshape(nb, tc, D_LRU)
    dbuf[...] = drive.reshape(nb, tc, D_LRU)

    def step(t, h):
        h = abuf[:, pl.ds(t, 1), :] * h + dbuf[:, pl.ds(t, 1), :]
        dbuf[:, pl.ds(t, 1), :] = h
        return h

    h = lax.fori_loop(0, tc, step, hbuf[...], unroll=8)
    hbuf[...] = h
    hl_ref[...] = jnp.broadcast_to(h, hl_ref.shape)
    a_ref[...] = dbuf[...] * jax.nn.gelu(xg_ref[:, :, D_LRU:2 * D_LRU])
    xbuf[:, 0:8, :] = xbuf[:, tc:tc + 8, :]

    gbuf[:, 32:32 + tc, :] = glu_ref[...]
    z = jnp.zeros((nb, tc, D_CONF), F32) + db_ref[...]
    for k in range(CONF_KERNEL):
        o = 32 - (CONF_KERNEL - 1) + k
        z = z + gbuf[:, o:o + tc, :] * dw_ref[k:k + 1, :]
    b_ref[...] = jax.nn.silu(_layernorm(z, lng_ref[...], lnb_ref[...]))
    gbuf[:, 0:32, :] = gbuf[:, tc:tc + 32, :]


def _seq_prompt(xg, glu, P, *, tc):
    nb, S, _ = glu.shape
    const = lambda shape: pl.BlockSpec(shape, lambda i: tuple(0 for _ in shape))
    seq = lambda w: pl.BlockSpec((nb, tc, w), lambda i: (0, i, 0))
    return pl.pallas_call(
        _seq_prompt_kernel,
        grid=(S // tc,),
        in_specs=[seq(512), seq(256), const((LRU_CONV, 256)), const((1, 256)), const((256, 256)), const((1, 256)),
                  const((256, 256)), const((1, 256)), const((1, 256)), const((CONF_KERNEL, 256)), const((1, 256)),
                  const((1, 256)), const((1, 256))],
        out_specs=[seq(256), seq(256), const((nb, 8, 256))],
        out_shape=[jax.ShapeDtypeStruct((nb, S, 256), F32), jax.ShapeDtypeStruct((nb, S, 256), F32),
                   jax.ShapeDtypeStruct((nb, 8, 256), F32)],
        scratch_shapes=[pltpu.VMEM((nb, tc + 8, 256), F32), pltpu.VMEM((nb, tc + 32, 256), F32),
                        pltpu.VMEM((nb, tc, 256), F32), pltpu.VMEM((nb, tc, 256), F32),
                        pltpu.VMEM((nb, 1, 256), F32)],
        compiler_params=_cparams(("arbitrary",)),
        name="seq_prompt",
    )(xg, glu, P['lru_conv_w'], P['lru_conv_b'], P['lru_wa_bd'], P['lru_ba'], P['lru_wx_bd'], P['lru_bx'],
      P['lru_lam'], P['conf_dw_w'], P['conf_dw_b'], P['conf_ln_g'], P['conf_ln_b'])


def _seq_sample_kernel(xg_ref, glu_ref, cs_ref, gs_ref, h0_ref, cw_ref, cb_ref, wa_ref, ba_ref, wx_ref, bx_ref,
                       lam_ref, dw_ref, db_ref, lng_ref, lnb_ref,
                       a_ref, b_ref, hl_ref, ncs_ref, ngs_ref, xbuf, gbuf):
    T, nb, _ = glu_ref.shape
    nc, ng = LRU_CONV - 1, CONF_KERNEL - 1
    xbuf[0:nc] = cs_ref[...]
    xbuf[nc:nc + T] = xg_ref[:, :, 0:D_LRU]
    y = jnp.zeros((T, nb, D_LRU), F32) + cb_ref[...]
    for k in range(LRU_CONV):
        y = y + xbuf[k:k + T] * cw_ref[k:k + 1, :]
    a, drive = _lru_gates(y.reshape(T * nb, D_LRU),
                          (wa_ref[...], ba_ref[...], wx_ref[...], bx_ref[...]), lam_ref[...])
    a = a.reshape(T, nb, D_LRU)
    drive = drive.reshape(T, nb, D_LRU)
    h = h0_ref[...]
    for t in range(T):
        h = a[t] * h + drive[t]
        a_ref[t] = h * jax.nn.gelu(xg_ref[t, :, D_LRU:2 * D_LRU])
    hl_ref[...] = h
    ncs_ref[...] = xbuf[T:T + nc]

    gbuf[0:ng] = gs_ref[...]
    gbuf[ng:ng + T] = glu_ref[...]
    z = jnp.zeros((T, nb, D_CONF), F32) + db_ref[...]
    for k in range(CONF_KERNEL):
        z = z + gbuf[k:k + T] * dw_ref[k:k + 1, :]
    b_ref[...] = jax.nn.silu(_layernorm(z, lng_ref[...], lnb_ref[...]))
    ngs_ref[...] = gbuf[T:T + ng]


def _seq_sample(xg_t, glu_t, cs_t, gs_t, h0, P):
    T, nb, _ = glu_t.shape
    nc, ng = LRU_CONV - 1, CONF_KERNEL - 1
    full = lambda a: pl.BlockSpec(a.shape, lambda i: tuple(0 for _ in a.shape))
    args = (xg_t, glu_t, cs_t, gs_t, h0, P['lru_conv_w'], P['lru_conv_b'], P['lru_wa_bd'], P['lru_ba'],
            P['lru_wx_bd'], P['lru_bx'], P['lru_lam'], P['conf_dw_w'], P['conf_dw_b'], P['conf_ln_g'], P['conf_ln_b'])
    outs = [jax.ShapeDtypeStruct((T, nb, 256), F32), jax.ShapeDtypeStruct((T, nb, 256), F32),
            jax.ShapeDtypeStruct((nb, 256), F32), jax.ShapeDtypeStruct((nc, nb, 256), F32),
            jax.ShapeDtypeStruct((ng, nb, 256), F32)]
    return pl.pallas_call(
        _seq_sample_kernel,
        grid=(1,),
        in_specs=[full(a) for a in args],
        out_specs=[full(o) for o in outs],
        out_shape=outs,
        scratch_shapes=[pltpu.VMEM((T + nc, nb, 256), F32), pltpu.VMEM((T + ng, nb, 256), F32)],
        compiler_params=_cparams(("arbitrary",)),
        name="seq_sample",
    )(*args)


def _online_update(s, v_t, m_ref, acc_ref):
    m_prev = m_ref[...]
    m_new = jnp.maximum(m_prev, jnp.max(s, axis=0, keepdims=True))
    p = jnp.exp2(s - m_new)
    acc_ref[...] = jnp.exp2(m_prev - m_new) * acc_ref[...] + jnp.dot(v_t, p.astype(BF16), preferred_element_type=F32)
    m_ref[...] = m_new


def _softmax_pv(s, v_t):
    p = jnp.exp2(s - jnp.max(s, axis=0, keepdims=True))
    acc = jnp.dot(v_t, p.astype(BF16), preferred_element_type=F32)
    dv = acc.shape[0] - VSUM_ROWS
    return acc[0:dv] / jnp.maximum(acc[dv:dv + 1], 1e-30)


def _init_stats(m_ref, acc_ref, l_ref=None):
    m_ref[...] = jnp.full(m_ref.shape, NEG, F32)
    acc_ref[...] = jnp.zeros(acc_ref.shape, F32)
    if l_ref is not None:
        l_ref[...] = jnp.zeros(l_ref.shape, F32)


def _finish_t(acc_ref):
    dv = acc_ref.shape[0] - VSUM_ROWS
    return acc_ref[0:dv, :] / jnp.maximum(acc_ref[dv:dv + 1, :], 1e-30)


def _finish(l_ref, acc_ref):
    return acc_ref[...] / jnp.maximum(l_ref[...], 1e-30)


def _compressed_kv(p_ref, pe_ref, nc):
    ncp = p_ref.shape[0]
    top = p_ref[:, 0:128]
    bot = pltpu.roll(p_ref[:, 128:256], ncp - 1, 0)
    n = lax.broadcasted_iota(jnp.int32, (ncp, 128), 0)
    return jnp.where(n < nc, top + bot + pe_ref[0:1, :], 0.0)


def _compressed_branch(q, kcv, q_pos, nc):
    ncp = kcv.shape[0]
    s = _dot_nt(kcv[:, 0:64], q)
    n = lax.broadcasted_iota(jnp.int32, s.shape, 0)
    mask = (n * NSA_CMP_STRIDE + (NSA_CMP_BLOCK - 1) <= q_pos) & (n < nc)
    s = jnp.where(mask, s, NEG)
    e = jnp.where(mask, jnp.exp2(s - jnp.max(s, axis=0, keepdims=True)), 0.0)
    p = e / jnp.maximum(jnp.sum(e, axis=0, keepdims=True), 1e-30)
    o = _dot_tn(kcv[:, 64:128], p)
    return o, p


def _select_blocks(imp, q_pos, n_sel):
    blk = lax.broadcasted_iota(jnp.int32, imp.shape, 0)
    cur = q_pos // NSA_SEL_BLOCK
    valid = blk * NSA_SEL_BLOCK <= q_pos
    forced = (blk == 0) | (blk == cur) | (blk == cur - 1)
    score = jnp.where(valid, imp + jnp.where(forced, NSA_FORCE, 0.0), -1.0)
    big = jnp.int32(2 ** 30)
    sel = jnp.zeros(imp.shape, jnp.bool_)
    for _ in range(n_sel):
        mx = jnp.max(score, axis=0, keepdims=True)
        first = jnp.min(jnp.where(score == mx, blk, big), axis=0, keepdims=True)
        hit = blk == first
        sel = sel | hit
        score = jnp.where(hit, -3e38, score)
    return jnp.where(sel, 0.0, NEG)


def _cmp_prompt_kernel(x_ref, w_ref, p_ref):
    n_chunk = p_ref.shape[0]
    acc = jnp.zeros(p_ref.shape, F32)
    for r in range(NSA_CMP_STRIDE):
        acc = acc + _bdot(x_ref[pl.ds(r, n_chunk, stride=NSA_CMP_STRIDE), :], w_ref[r])
    p_ref[...] = acc


def _cmp_prompt(nsa4, w4, nb):
    N = nsa4.shape[0]
    S = N // nb
    n_chunk = S // NSA_CMP_STRIDE
    return pl.pallas_call(
        _cmp_prompt_kernel,
        grid=(nb,),
        in_specs=[pl.BlockSpec((S, 128), lambda b: (b, 0)),
                  pl.BlockSpec((NSA_CMP_STRIDE, 128, 256), lambda b: (0, 0, 0))],
        out_specs=pl.BlockSpec((None, n_chunk, 256), lambda b: (b, 0, 0)),
        out_shape=jax.ShapeDtypeStruct((nb, n_chunk, 256), F32),
        compiler_params=_cparams(("parallel",)),
        name="cmp_prompt",
    )(nsa4, w4)


def _cmp_sample_kernel(pt_ref, *refs):
    pages, w_ref, p_ref, x_ref = refs[:-3], refs[-3], refs[-2], refs[-1]
    page = pages[0].shape[1]
    rows = page // NSA_CMP_STRIDE
    for j, pg in enumerate(pages):
        x_ref[j * page:(j + 1) * page, :] = pg[...].T
    acc = jnp.zeros(p_ref.shape, F32)
    for r in range(NSA_CMP_STRIDE):
        acc = acc + _bdot(x_ref[pl.ds(r, rows * len(pages), stride=NSA_CMP_STRIDE), :], w_ref[r])
    p_ref[...] = acc


def _page_specs(layer, n_pages, pps, rows, row_block):
    def spec(j):
        return pl.BlockSpec((None, None, rows, 128),
                            lambda b, c, pt: (layer, pt[b * n_pages + c * pps + j], row_block, 0))
    return [spec(j) for j in range(pps)]


def _cmp_sample(cache_nsa_t, pt_flat, w4, layer, nb, n_pages, pps):
    page = cache_nsa_t.shape[3]
    cpp = page // NSA_CMP_STRIDE
    nch = n_pages // pps
    grid_spec = pltpu.PrefetchScalarGridSpec(
        num_scalar_prefetch=1,
        grid=(nb, nch),
        in_specs=_page_specs(layer, n_pages, pps, 128, 0)
        + [pl.BlockSpec((NSA_CMP_STRIDE, 128, 256), lambda b, c, pt: (0, 0, 0))],
        out_specs=pl.BlockSpec((None, pps * cpp, 256), lambda b, c, pt: (b, c, 0)),
        scratch_shapes=[pltpu.VMEM((pps * page, 128), F32)],
    )
    return pl.pallas_call(
        _cmp_sample_kernel,
        grid_spec=grid_spec,
        out_shape=jax.ShapeDtypeStruct((nb, n_pages * cpp, 256), F32),
        compiler_params=_cparams(("parallel", "arbitrary")),
        name="cmp_sample",
    )(pt_flat, *([cache_nsa_t] * pps), w4)


def _attn_prompt_kernel(qm_ref, km_ref, vmt_ref, wuvt_ref, q_ref, g_ref, p_ref, pe_ref, ov_ref, ks_ref, vst_ref,
                        kw_ref, vwt_ref, c_ref, o_ref,
                        m_ref, acc_ref, mm_ref, accm_ref,
                        sma_ref, smb_ref, ssa_ref, ssb_ref, *, nc, n_sel):
    i = pl.program_id(1)
    _, tq, _ = q_ref.shape
    tk = ks_ref.shape[1]
    tw = kw_ref.shape[1]
    R = NSA_HEADS * tq
    q_f32 = q_ref[...].reshape(R, NSA_HD)
    q = q_f32.astype(BF16)
    qm = qm_ref[...].reshape(R, MLA_CACHE_W).astype(BF16)
    q_pos_q = i * tq + lax.broadcasted_iota(jnp.int32, (1, tq), 1)
    q_pos = jnp.concatenate([q_pos_q] * NSA_HEADS, axis=1)

    kcv = _compressed_kv(p_ref, pe_ref, nc)
    o_c, p_c = _compressed_branch(q, kcv, q_pos, nc)
    imp4 = jnp.dot(ov_ref[...], p_c.astype(BF16), preferred_element_type=F32)
    imp = imp4[:, 0:tq]
    for hd in range(1, NSA_HEADS):
        imp = imp + imp4[:, hd * tq:(hd + 1) * tq]
    bias = _select_blocks(imp, q_pos_q, n_sel).T
    q_sel = jnp.concatenate([jnp.concatenate([bias] * NSA_HEADS, axis=0), q_f32], axis=1).astype(BF16)

    n_full = (i * tq) // tk
    _init_stats(m_ref, acc_ref)
    _init_stats(mm_ref, accm_ref)

    def score_m(j):
        return _dot_nt(km_ref[j], qm)

    def score_s(j):
        return _dot_nt(ks_ref[j], q_sel)

    def use_m(s, j):
        _online_update(s, vmt_ref[j], mm_ref, accm_ref)

    def use_s(s, j):
        _online_update(s, vst_ref[j], m_ref, acc_ref)

    causal = n_full * tk + lax.broadcasted_iota(jnp.int32, (tk, R), 0) <= q_pos
    use_m(jnp.where(causal, score_m(n_full), NEG), n_full)
    use_s(jnp.where(causal, score_s(n_full), NEG), n_full)

    @pl.when(n_full > 0)
    def _():
        sma_ref[...] = score_m(0)
        ssa_ref[...] = score_s(0)

    def pair(jj, c):
        j = 2 * jj
        smb_ref[...] = score_m(j + 1)
        ssb_ref[...] = score_s(j + 1)
        use_m(sma_ref[...], j)
        use_s(ssa_ref[...], j)
        nxt = jnp.minimum(j + 2, n_full - 1)
        sma_ref[...] = score_m(nxt)
        ssa_ref[...] = score_s(nxt)
        use_m(smb_ref[...], j + 1)
        use_s(ssb_ref[...], j + 1)
        return c

    lax.fori_loop(0, n_full // 2, pair, 0)

    @pl.when(n_full % 2 == 1)
    def _():
        use_m(sma_ref[...], n_full - 1)
        use_s(ssa_ref[...], n_full - 1)

    o_s = _finish_t(acc_ref)
    o_m = _finish_t(accm_ref)
    c_t = jnp.concatenate([_bdot(wuvt_ref[hd], o_m[:, hd * tq:(hd + 1) * tq]) for hd in range(MLA_HEADS)], axis=0)
    c_ref[...] = c_t.T

    n_wt = NSA_WINDOW // tw
    kw = jnp.concatenate([kw_ref[i + c] for c in range(n_wt + 1)], axis=0)
    vw_t = jnp.concatenate([vwt_ref[i + c] for c in range(n_wt + 1)], axis=1)
    k_pos = (i - n_wt) * tw + lax.broadcasted_iota(jnp.int32, (kw.shape[0], R), 0)
    ok = (k_pos >= 0) & (k_pos <= q_pos) & (k_pos > q_pos - NSA_WINDOW)
    o_w = _softmax_pv(jnp.where(ok, _dot_nt(kw, q), NEG), vw_t)

    g_t = g_ref[...].T
    outs = []
    for hd in range(NSA_HEADS):
        sl = slice(hd * tq, (hd + 1) * tq)
        r0 = MISC_NG + 3 * hd
        outs.append(g_t[r0:r0 + 1, :] * o_c[:, sl] + g_t[r0 + 1:r0 + 2, :] * o_s[:, sl]
                    + g_t[r0 + 2:r0 + 3, :] * o_w[:, sl])
    o_ref[...] = jnp.concatenate(outs, axis=0).T


def _attn_prompt(qm, km3, vmt3, wuvt, q, gates, P, pe, ov, ks3, vst3, kw3, vwt3, *, tq, nc, n_sel):
    nb, nk, tk, dks = ks3.shape
    nw, tw = kw3.shape[1:3]
    assert tw == tq
    N = q.shape[1]
    nq = N // nb // tq
    ncp = P.shape[1]
    nblk = ov.shape[0]
    R = 4 * tq
    out = pl.BlockSpec((tq, 256), lambda b, i: (b * nq + i, 0))
    whole = lambda a: pl.BlockSpec((None,) + a.shape[1:], lambda b, i: (b,) + (0,) * (a.ndim - 1))
    return pl.pallas_call(
        functools.partial(_attn_prompt_kernel, nc=nc, n_sel=n_sel),
        grid=(nb, nq),
        in_specs=[pl.BlockSpec((4, tq, MLA_CACHE_W), lambda b, i: (0, b * nq + i, 0)),
                  whole(km3), whole(vmt3),
                  pl.BlockSpec((4, MLA_V, MLA_KV_LORA), lambda b, i: (0, 0, 0)),
                  pl.BlockSpec((4, tq, NSA_HD), lambda b, i: (0, b * nq + i, 0)),
                  pl.BlockSpec((tq, 128), lambda b, i: (b * nq + i, 0)),
                  pl.BlockSpec((None, ncp, 256), lambda b, i: (b, 0, 0)),
                  pl.BlockSpec((8, 128), lambda b, i: (0, 0)),
                  pl.BlockSpec((nblk, ncp), lambda b, i: (0, 0)),
                  whole(ks3), whole(vst3), whole(kw3), whole(vwt3)],
        out_specs=[out, out],
        out_shape=[jax.ShapeDtypeStruct((N, 256), F32), jax.ShapeDtypeStruct((N, 256), F32)],
        scratch_shapes=[pltpu.VMEM((1, R), F32), pltpu.VMEM((NSA_HD + VSUM_ROWS, R), F32),
                        pltpu.VMEM((1, R), F32), pltpu.VMEM((MLA_KV_LORA + VSUM_ROWS, R), F32)]
        + [pltpu.VMEM((tk, R), F32) for _ in range(4)],
        compiler_params=_cparams(("arbitrary", "arbitrary")),
        name="attn_prompt",
    )(qm, km3, vmt3, wuvt, q, gates, P, pe, ov, ks3, vst3, kw3, vwt3)


def _sample_queries(q_ref):
    nh, T, d = q_ref.shape
    q = q_ref[...].reshape(nh * T, d)
    return jnp.concatenate([q, jnp.zeros((SAMPLE_R - nh * T, d), F32)], axis=0).astype(BF16)


def _row_update(s, pv, m_ref, l_ref, acc_ref):
    m_prev = m_ref[...]
    m_new = jnp.maximum(m_prev, jnp.max(s, axis=1, keepdims=True))
    alpha = jnp.exp2(m_prev - m_new)
    p = jnp.exp2(s - m_new)
    l_ref[...] = alpha * l_ref[...] + jnp.sum(p, axis=1, keepdims=True)
    acc_ref[...] = alpha * acc_ref[...] + pv(p.astype(BF16))
    m_ref[...] = m_new


def _mla_sample_kernel(pt_ref, *refs, pps):
    pages = refs[:pps]
    q_ref, new_ref, wuv_ref, o_ref, m_ref, l_ref, acc_ref = refs[pps:]
    c = pl.program_id(1)
    _, T, dk = q_ref.shape
    R = MLA_HEADS * T
    q = q_ref[...].reshape(R, dk).astype(BF16)

    @pl.when(c == 0)
    def _():
        _init_stats(m_ref, acc_ref, l_ref)

    kt = jnp.concatenate([pg[...].astype(BF16) for pg in pages], axis=1)
    vt = kt[0:MLA_KV_LORA]
    _row_update(jnp.dot(q, kt, preferred_element_type=F32), lambda p: _dot_nt(p, vt), m_ref, l_ref, acc_ref)

    @pl.when(c == pl.num_programs(1) - 1)
    def _():
        new = new_ref[...].astype(BF16)
        k_t = lax.broadcasted_iota(jnp.int32, (R, T), 1)
        q_t = lax.broadcasted_iota(jnp.int32, (R, T), 0) % T
        _row_update(jnp.where(k_t <= q_t, _dot_nt(q, new), NEG),
                    lambda p: jnp.dot(p, new[:, 0:MLA_KV_LORA], preferred_element_type=F32), m_ref, l_ref, acc_ref)
        o = _finish(l_ref, acc_ref)
        o_ref[...] = jnp.concatenate([_bdot(o[hd * T:(hd + 1) * T, :], wuv_ref[hd]) for hd in range(MLA_HEADS)],
                                     axis=1)


def _mla_sample(cache_mla_t, pt_flat, q, rows, wuv, layer, nb, n_pages, pps):
    T = q.shape[1] // nb
    N = q.shape[1]
    nch = n_pages // pps
    R = MLA_HEADS * T
    grid_spec = pltpu.PrefetchScalarGridSpec(
        num_scalar_prefetch=1,
        grid=(nb, nch),
        in_specs=_page_specs(layer, n_pages, pps, MLA_CACHE_W, 0)
        + [pl.BlockSpec((4, T, MLA_CACHE_W), lambda b, c, pt: (0, b, 0)),
           pl.BlockSpec((T, MLA_CACHE_W), lambda b, c, pt: (b, 0)),
           pl.BlockSpec((4, MLA_KV_LORA, MLA_V), lambda b, c, pt: (0, 0, 0))],
        out_specs=pl.BlockSpec((T, 256), lambda b, c, pt: (b, 0)),
        scratch_shapes=[pltpu.VMEM((R, 1), F32), pltpu.VMEM((R, 1), F32), pltpu.VMEM((R, MLA_KV_LORA), F32)],
    )
    return pl.pallas_call(
        functools.partial(_mla_sample_kernel, pps=pps),
        grid_spec=grid_spec,
        out_shape=jax.ShapeDtypeStruct((N, 256), F32),
        compiler_params=_cparams(("parallel", "arbitrary")),
        name="mla_sample",
    )(pt_flat, *([cache_mla_t] * pps), q, rows, wuv)


def _nsa_sample_kernel(pt_ref, *refs, pps, nc, n_sel, past):
    pages = refs[:pps]
    (q_ref, g_ref, p_ref, pe_ref, ov_ref, e_ref, new_ref, wc_ref, wn_ref, o_ref,
     qa_ref, nb_ref, oc_ref, ow_ref, m_ref, l_ref, acc_ref) = refs[pps:]
    c = pl.program_id(1)
    nch = pl.num_programs(1)
    _, T, _ = q_ref.shape
    R = NSA_HEADS * T
    q_f32 = q_ref[...].reshape(R, NSA_HD)
    q = q_f32.astype(BF16)
    q_pos = past + lax.broadcasted_iota(jnp.int32, (R, 1), 0) % T
    bpc = pps * (pages[0].shape[1] // NSA_SEL_BLOCK)
    ka = qa_ref.shape[2]

    @pl.when(c == 0)
    def _():
        lane = lax.broadcasted_iota(jnp.int32, (1, SAMPLE_R), 1)
        q_pos_l = past + lane % T
        kcv = _compressed_kv(p_ref, pe_ref, nc)
        o_c, p_c = _compressed_branch(_sample_queries(q_ref), kcv, q_pos_l, nc)
        oc_ref[...] = o_c.T[0:R]
        imp = jnp.dot(ov_ref[...], p_c.astype(BF16), preferred_element_type=F32)
        imp = jnp.where(lane < R, imp, 0.0)
        tot = imp
        for k in range(1, NSA_HEADS):
            tot = tot + pltpu.roll(imp, k * T, 1) + pltpu.roll(imp, SAMPLE_R - R + k * T, 1)
        n_real = -(-(-(-(past + T) // NSA_SEL_BLOCK)) // 8) * 8
        bias = jnp.concatenate([_select_blocks(tot[0:n_real], q_pos_l, n_sel),
                                jnp.full((tot.shape[0] - n_real, SAMPLE_R), NEG, F32)], axis=0).T
        for cc in range(qa_ref.shape[0]):
            qa_ref[cc] = jnp.concatenate([q_f32, bias[0:R, cc * bpc:(cc + 1) * bpc],
                                          jnp.zeros((R, ka - NSA_HD - bpc), F32)], axis=1)
        nb_ref[...] = bias[0:R, past // NSA_SEL_BLOCK:past // NSA_SEL_BLOCK + 1]
        _init_stats(m_ref, acc_ref, l_ref)
        wc = wc_ref[...].astype(BF16)
        n_buf = wc.shape[1]
        k_pos = past - n_buf + lax.broadcasted_iota(jnp.int32, (R, n_buf), 1)
        ok = (k_pos >= 0) & (k_pos <= q_pos) & (k_pos > q_pos - NSA_WINDOW)
        _row_update(jnp.where(ok, jnp.dot(q, wc[0:64], preferred_element_type=F32), NEG),
                    lambda p: _dot_nt(p, wc[64:128]), m_ref, l_ref, acc_ref)
        wn = wn_ref[...].astype(BF16)
        k_pos = past + lax.broadcasted_iota(jnp.int32, (R, T), 1)
        ok = (k_pos <= q_pos) & (k_pos > q_pos - NSA_WINDOW)
        _row_update(jnp.where(ok, _dot_nt(q, wn[:, 0:64]), NEG),
                    lambda p: jnp.dot(p, wn[:, 64:128], preferred_element_type=F32), m_ref, l_ref, acc_ref)
        ow_ref[...] = _finish(l_ref, acc_ref)
        _init_stats(m_ref, acc_ref, l_ref)

    kvt = jnp.concatenate([pg[...].astype(BF16) for pg in pages], axis=1)
    k_aug = jnp.concatenate([kvt[0:64], e_ref[...]], axis=0)
    vst = kvt[64:128]
    _row_update(jnp.dot(qa_ref[c].astype(BF16), k_aug, preferred_element_type=F32),
                lambda p: _dot_nt(p, vst), m_ref, l_ref, acc_ref)

    @pl.when(c == nch - 1)
    def _():
        new = new_ref[...].astype(BF16)
        k_pos = past + lax.broadcasted_iota(jnp.int32, (R, T), 1)
        s = jnp.where(k_pos <= q_pos, _dot_nt(q, new[:, 128:192]) + nb_ref[...], NEG)
        _row_update(s, lambda p: jnp.dot(p, new[:, 192:256], preferred_element_type=F32), m_ref, l_ref, acc_ref)
        o_c = oc_ref[...]
        o_s = _finish(l_ref, acc_ref)
        o_w = ow_ref[...]
        g = g_ref[...]
        outs = []
        for hd in range(NSA_HEADS):
            sl = slice(hd * T, (hd + 1) * T)
            r0 = MISC_NG + 3 * hd
            outs.append(g[:, r0:r0 + 1] * o_c[sl] + g[:, r0 + 1:r0 + 2] * o_s[sl] + g[:, r0 + 2:r0 + 3] * o_w[sl])
        o_ref[...] = jnp.concatenate(outs, axis=1)


def _nsa_sample(cache_nsa_t, pt_flat, q, gates, P, pe, ov, new4, win_cache_t, win_new, layer, nb, n_pages, pps,
                *, nc, n_sel, past):
    N = q.shape[1]
    T = N // nb
    nch = n_pages // pps
    R = NSA_HEADS * T
    ncp = P.shape[1]
    nblk = ov.shape[0]
    n_buf = win_cache_t.shape[3]
    page = cache_nsa_t.shape[3]
    bpc = pps * (page // NSA_SEL_BLOCK)
    ka = -(-(NSA_HD + bpc) // 128) * 128
    keys = np.arange(pps * page)[None, :] // NSA_SEL_BLOCK
    onehot = jnp.asarray((keys == np.arange(ka - NSA_HD)[:, None]).astype(np.float32), dtype=BF16)
    grid_spec = pltpu.PrefetchScalarGridSpec(
        num_scalar_prefetch=1,
        grid=(nb, nch),
        in_specs=_page_specs(layer, n_pages, pps, 128, 1)
        + [pl.BlockSpec((4, T, NSA_HD), lambda b, c, pt: (0, b, 0)),
           pl.BlockSpec((T, 128), lambda b, c, pt: (b, 0)),
           pl.BlockSpec((None, ncp, 256), lambda b, c, pt: (b, 0, 0)),
           pl.BlockSpec((8, 128), lambda b, c, pt: (0, 0)),
           pl.BlockSpec((nblk, ncp), lambda b, c, pt: (0, 0)),
           pl.BlockSpec((ka - NSA_HD, pps * page), lambda b, c, pt: (0, 0)),
           pl.BlockSpec((T, 256), lambda b, c, pt: (b, 0)),
           pl.BlockSpec((None, None, 128, n_buf), lambda b, c, pt: (layer, b, 0, 0)),
           pl.BlockSpec((T, 128), lambda b, c, pt: (b, 0))],
        out_specs=pl.BlockSpec((T, 256), lambda b, c, pt: (b, 0)),
        scratch_shapes=[pltpu.VMEM((nch, R, ka), F32), pltpu.VMEM((R, 1), F32), pltpu.VMEM((R, NSA_HD), F32),
                        pltpu.VMEM((R, NSA_HD), F32), pltpu.VMEM((R, 1), F32), pltpu.VMEM((R, 1), F32),
                        pltpu.VMEM((R, NSA_HD), F32)],
    )
    return pl.pallas_call(
        functools.partial(_nsa_sample_kernel, pps=pps, nc=nc, n_sel=n_sel, past=past),
        grid_spec=grid_spec,
        out_shape=jax.ShapeDtypeStruct((N, 256), F32),
        compiler_params=_cparams(("parallel", "arbitrary")),
        name="nsa_sample",
    )(pt_flat, *([cache_nsa_t] * pps), q, gates, P, pe, ov, onehot, new4, win_cache_t, win_new)


def _block_diag(w):
    n, a, b = w.shape
    out = jnp.zeros((n * a, n * b), w.dtype)
    for j in range(n):
        out = out.at[j * a:(j + 1) * a, j * b:(j + 1) * b].set(w[j])
    return out


def _rot_cols(w):
    half = w.shape[-1] // 2
    return jnp.concatenate([-w[..., half:], w[..., :half]], axis=-1)


def _layer_params(l, W):
    idx = np.cumsum(IN_SPLITS)[:-1].tolist()
    lru_x, lru_g, conf_u, cq, ckv, kr, nq, nkv, ng = jnp.split(W['w_in'][l], idx, axis=-1)
    pad = jnp.zeros((D_MODEL, D_INP - C_MISC - 2 * MLA_ROPE - 3 * NSA_HEADS), F32)
    w_in = jnp.concatenate([lru_x, lru_g, conf_u, cq, ckv, nq, nkv, kr, _rot_cols(kr), ng, pad], axis=-1)
    wuq = W['mla_wuq'][l].reshape(MLA_Q_LORA, MLA_HEADS, MLA_NOPE + MLA_ROPE)
    wuq_rope = wuq[:, :, MLA_NOPE:]
    wuq = jnp.concatenate([wuq[:, :, :MLA_NOPE].reshape(MLA_Q_LORA, -1), wuq_rope.reshape(MLA_Q_LORA, -1),
                           _rot_cols(wuq_rope).reshape(MLA_Q_LORA, -1)], axis=-1)
    wuk = _block_diag(jnp.transpose(W['mla_wuk'][l], (1, 2, 0)))
    wuv = jnp.transpose(W['mla_wuv'][l], (1, 0, 2))
    wk = W['nsa_wc_k'][l].reshape(NSA_CMP_BLOCK, NSA_HD, NSA_HD)
    wv = W['nsa_wc_v'][l].reshape(NSA_CMP_BLOCK, NSA_HD, NSA_HD)
    z = jnp.zeros((NSA_CMP_STRIDE, NSA_HD, NSA_HD), F32)
    w4 = jnp.concatenate([jnp.concatenate([wk[:16], z, wk[16:], z], axis=2),
                          jnp.concatenate([z, wv[:16], z, wv[16:]], axis=2)], axis=1)
    pe = jnp.concatenate([W['nsa_pe_k'][l], W['nsa_pe_v'][l]], axis=-1)
    pe2 = jnp.concatenate([pe[:16, None], pe[16:, None], jnp.zeros((16, 6, 128), F32)], axis=1)
    r1 = lambda a: a.reshape(1, -1)
    return dict(
        w_in=w_in.astype(BF16), g_pre_mix=r1(W['g_pre_mix'][l]), g_post_mix=r1(W['g_post_mix'][l]),
        g_pre_ffn=r1(W['g_pre_ffn'][l]), g_post_ffn=r1(W['g_post_ffn'][l]),
        mla_gq=r1(W['mla_gq'][l]), mla_gkv=r1(W['mla_gkv'][l]),
        wuq=wuq.astype(BF16), wuk=wuk.astype(BF16), wuv=wuv.astype(BF16),
        wuvt=jnp.transpose(wuv, (0, 2, 1)).astype(BF16),
        w4=w4.astype(BF16), pe2=pe2,
        lru_conv_w=W['lru_conv_w'][l], lru_conv_b=r1(W['lru_conv_b'][l]),
        lru_wa_bd=_block_diag(W['lru_wa'][l]).astype(BF16), lru_ba=r1(W['lru_ba'][l]),
        lru_wx_bd=_block_diag(W['lru_wx'][l]).astype(BF16), lru_bx=r1(W['lru_bx'][l]),
        lru_lam=r1(W['lru_lam'][l]),
        conf_dw_w=W['conf_dw_w'][l], conf_dw_b=r1(W['conf_dw_b'][l]),
        conf_ln_g=r1(W['conf_ln_g'][l]), conf_ln_b=r1(W['conf_ln_b'][l]),
        w_out=W['w_out'][l].astype(BF16), w_ff1=W['w_ff1'][l].astype(BF16), w_ff2=W['w_ff2'][l].astype(BF16),
    )


def _rope_tables(pos):
    half = MLA_ROPE // 2
    freq = jnp.power(ROPE_THETA, -jnp.arange(half, dtype=F32) / half)
    ang = pos.astype(F32)[:, None] * freq
    cosv = jnp.tile(jnp.cos(ang), (1, 2 * MLA_HEADS))
    sinv = jnp.tile(jnp.sin(ang), (1, 2 * MLA_HEADS))
    return cosv, sinv


def _overlap_t(nblk_pad, ncp, nc):
    n = np.arange(ncp)[None, :]
    j = np.arange(nblk_pad)[:, None]
    ov = (n * NSA_CMP_STRIDE < j * NSA_SEL_BLOCK + NSA_SEL_BLOCK) & (n * NSA_CMP_STRIDE + NSA_CMP_BLOCK > j * NSA_SEL_BLOCK)
    return jnp.asarray((ov & (n < nc)).astype(np.float32), dtype=BF16)


def _key_tiles(a, nb, t, front=0):
    k = a.reshape(nb, -1, t, a.shape[-1])
    return jnp.pad(k, ((0, 0), (front, 0), (0, 0), (0, 0)))


def _value_tiles_t(a, nb, t, front=0):
    v = jnp.swapaxes(_key_tiles(a, nb, t, front), 2, 3)
    ones = jnp.ones(v.shape[:2] + (1, t), v.dtype)
    return jnp.concatenate([v, ones, jnp.zeros(v.shape[:2] + (VSUM_ROWS - 1, t), v.dtype)], axis=2)


def kernel(x_prompt, x_sample, c_prompt, c_sample, state_lru_h, state_lru_conv, state_conv, cache_mla, cache_nsa,
           cache_nsa_win, page_table, w_mod, b_mod, g_pre_mix, g_post_mix, g_pre_ffn, g_post_ffn, w_in, lru_conv_w,
           lru_conv_b, lru_wa, lru_ba, lru_wx, lru_bx, lru_lam, conf_dw_w, conf_dw_b, conf_ln_g, conf_ln_b, mla_gq,
           mla_wuq, mla_gkv, mla_wuk, mla_wuv, nsa_pe_k, nsa_wc_k, nsa_pe_v, nsa_wc_v, w_out, w_ff1, w_ff2):
    W = dict(w_in=w_in, g_pre_mix=g_pre_mix, g_post_mix=g_post_mix, g_pre_ffn=g_pre_ffn, g_post_ffn=g_post_ffn,
             lru_conv_w=lru_conv_w, lru_conv_b=lru_conv_b, lru_wa=lru_wa, lru_ba=lru_ba, lru_wx=lru_wx, lru_bx=lru_bx,
             lru_lam=lru_lam, conf_dw_w=conf_dw_w, conf_dw_b=conf_dw_b, conf_ln_g=conf_ln_g, conf_ln_b=conf_ln_b,
             mla_gq=mla_gq, mla_wuq=mla_wuq, mla_gkv=mla_gkv, mla_wuk=mla_wuk, mla_wuv=mla_wuv,
             nsa_pe_k=nsa_pe_k, nsa_wc_k=nsa_wc_k, nsa_pe_v=nsa_pe_v, nsa_wc_v=nsa_wc_v,
             w_out=w_out, w_ff1=w_ff1, w_ff2=w_ff2)
    L = w_in.shape[0]
    nbp, S, D = x_prompt.shape
    nbs, T, _ = x_sample.shape
    n_pages = page_table.shape[1]
    page = cache_mla.shape[2]
    past = n_pages * page
    Np, Ns = nbp * S, nbs * T
    pps = min(PAGES_PER_STEP, n_pages)
    tq = min(Q_TILE, S)
    tk = min(K_TILE, S)
    tm_p = min(256, Np)
    tm_s = min(256, Ns)
    tc = min(512, S)

    nc_p = S // NSA_CMP_STRIDE - 1
    nblk_p = -(-S // NSA_SEL_BLOCK)
    ov_p = _overlap_t(nblk_p, S // NSA_CMP_STRIDE, nc_p)
    assert NSA_WINDOW % tq == 0 and S % tk == 0 and tk % tq == 0
    blk_of_pos = jnp.arange(S, dtype=jnp.int32)[:, None] // NSA_SEL_BLOCK
    blk_onehot_p = jnp.tile((blk_of_pos == jnp.arange(nblk_p, dtype=jnp.int32)[None, :]).astype(BF16), (nbp, 1))
    len_s = past + T
    assert len_s // NSA_CMP_STRIDE == past // NSA_CMP_STRIDE and past % NSA_SEL_BLOCK == 0 and T <= NSA_SEL_BLOCK
    nc_s = len_s // NSA_CMP_STRIDE - 1
    nblk_s = -(-len_s // NSA_SEL_BLOCK)
    nblk_s_pad = -(-nblk_s // 128) * 128
    ov_s = _overlap_t(nblk_s_pad, past // NSA_CMP_STRIDE, nc_s)

    rows = nbp + nbs
    rows_pad = -(-rows // 8) * 8
    c_all = jnp.concatenate([c_prompt, c_sample, jnp.zeros((rows_pad - rows, D), F32)], axis=0)
    mod = _modulation(c_all, w_mod.astype(BF16), b_mod)

    params = [_layer_params(l, W) for l in range(L)]
    pe_all = _pe_term(jnp.stack([p['pe2'] for p in params]), jnp.stack([p['w4'] for p in params]))

    cos_p, sin_p = _rope_tables(jnp.arange(S, dtype=jnp.int32))
    cos_s, sin_s = _rope_tables(past + jnp.arange(T, dtype=jnp.int32))
    cos_s = jnp.tile(cos_s, (nbs, 1))
    sin_s = jnp.tile(sin_s, (nbs, 1))
    pt_flat = page_table.reshape(-1).astype(jnp.int32)
    cache_mla_t = jnp.swapaxes(cache_mla, 2, 3)
    cache_nsa_t = jnp.transpose(cache_nsa, (0, 1, 3, 4, 2)).reshape(L, cache_nsa.shape[1], 4 * NSA_HD, page)
    win_cache_t = jnp.transpose(cache_nsa_win, (0, 1, 3, 4, 2)).reshape(L, nbs, 2 * NSA_HD, cache_nsa_win.shape[2])

    xp = x_prompt.reshape(Np, D)
    xs = x_sample.reshape(Ns, D)
    outs_p, outs_s = [], []
    for l in range(L):
        P = params[l]
        pe = pe_all[l]
        mp = [m.reshape(nbp, 1, D) for m in jnp.split(mod[l, :nbp], 6, axis=-1)]
        ms = [jnp.repeat(m, T, axis=0) for m in jnp.split(mod[l, nbp:rows], 6, axis=-1)]

        lru, glu, qm, row, nq, nsa4, win, gates = _in_proj(
            xp, mp[0], mp[1], P['g_pre_mix'], P['w_in'], P['mla_gq'], P['wuq'], P['wuk'], P['mla_gkv'],
            cos_p, sin_p, per_token=False, tm=tm_p)
        a_out, b_out, h_last = _seq_prompt(lru.reshape(nbp, S, 512), glu.reshape(nbp, S, 256), P, tc=tc)
        rowb = row.astype(BF16)
        k3 = _key_tiles(rowb, nbp, tk)
        vt3 = _value_tiles_t(rowb[:, :MLA_KV_LORA], nbp, tk)
        Pp = _cmp_prompt(nsa4, P['w4'], nbp)
        nsab = nsa4.astype(BF16)
        ks3 = _key_tiles(jnp.concatenate([blk_onehot_p, nsab[:, 128:192]], axis=1), nbp, tk)
        vst3 = _value_tiles_t(nsab[:, 192:256], nbp, tk)
        winb = win.astype(BF16)
        kw3 = _key_tiles(winb[:, 0:64], nbp, tq, front=NSA_WINDOW // tq)
        vwt3 = _value_tiles_t(winb[:, 64:128], nbp, tq, front=NSA_WINDOW // tq)
        c_out, d_out = _attn_prompt(qm, k3, vt3, P['wuvt'], nq, gates, Pp, pe, ov_p, ks3, vst3, kw3, vwt3, tq=tq,
                                    nc=nc_p, n_sel=min(NSA_N_SEL, nblk_p))
        xp = _post(xp, a_out.reshape(Np, 256), b_out.reshape(Np, 256), c_out, d_out, mp[2], mp[3], mp[4], mp[5],
                   P['w_out'], P['g_post_mix'], P['g_pre_ffn'], P['g_post_ffn'], P['w_ff1'], P['w_ff2'],
                   per_token=False, tm=tm_p)
        n_win = min(NSA_WINDOW, S)
        outs_p.append((h_last[:, 0], lru.reshape(nbp, S, 512)[:, S - (LRU_CONV - 1):, :D_LRU],
                       glu.reshape(nbp, S, 256)[:, S - (CONF_KERNEL - 1):],
                       row.reshape(nbp, S, MLA_CACHE_W), nsa4.reshape(nbp, S, 4, NSA_HD),
                       win.reshape(nbp, S, 2, NSA_HD)[:, S - n_win:]))

        lru, glu, qm, row, nq, nsa4, win, gates = _in_proj(
            xs, ms[0], ms[1], P['g_pre_mix'], P['w_in'], P['mla_gq'], P['wuq'], P['wuk'], P['mla_gkv'],
            cos_s, sin_s, per_token=True, tm=tm_s)
        tmaj = lambda a: jnp.swapaxes(a, 0, 1)
        a_t, b_t, h_last, ncs, ngs = _seq_sample(
            tmaj(lru.reshape(nbs, T, 512)), tmaj(glu.reshape(nbs, T, 256)), tmaj(state_lru_conv[l]),
            tmaj(state_conv[l]), state_lru_h[l], P)
        c_out = _mla_sample(cache_mla_t, pt_flat, qm, row, P['wuv'], l, nbs, n_pages, pps)
        Ps = _cmp_sample(cache_nsa_t, pt_flat, P['w4'], l, nbs, n_pages, pps)
        d_out = _nsa_sample(cache_nsa_t, pt_flat, nq, gates, Ps, pe, ov_s, nsa4, win_cache_t, win, l, nbs, n_pages, pps,
                            nc=nc_s, n_sel=min(NSA_N_SEL, nblk_s), past=past)
        xs = _post(xs, tmaj(a_t).reshape(Ns, 256), tmaj(b_t).reshape(Ns, 256), c_out, d_out, ms[2], ms[3], ms[4],
                   ms[5], P['w_out'], P['g_post_mix'], P['g_pre_ffn'], P['g_post_ffn'], P['w_ff1'], P['w_ff2'],
                   per_token=True, tm=tm_s)
        wkv = jnp.concatenate([cache_nsa_win[l], win.reshape(nbs, T, 2, NSA_HD)], axis=1)
        n_win = min(NSA_WINDOW, past + T)
        outs_s.append((h_last, tmaj(ncs), tmaj(ngs), row.reshape(nbs, T, MLA_CACHE_W),
                       nsa4.reshape(nbs, T, 4, NSA_HD), wkv[:, -n_win:]))

    stack = lambda outs, i: jnp.stack([o[i] for o in outs])
    return (xp.reshape(nbp, S, D), xs.reshape(nbs, T, D),
            stack(outs_p, 0), stack(outs_s, 0), stack(outs_p, 1), stack(outs_s, 1), stack(outs_p, 2), stack(outs_s, 2),
            stack(outs_p, 3), stack(outs_s, 3), stack(outs_p, 4), stack(outs_s, 4), stack(outs_p, 5), stack(outs_s, 5))
```

```python
import functools

import numpy as np
import jax
import jax.numpy as jnp
from jax import lax
from jax.experimental import pallas as pl
from jax.experimental.pallas import tpu as pltpu

F32 = jnp.float32
BF16 = jnp.bfloat16

D_MODEL = 1024
D_LRU = 256
LRU_BLOCKS = 4
LRU_CONV = 4
LRU_C = 8.0
D_CONF = 256
CONF_KERNEL = 31
MLA_HEADS = 4
MLA_NOPE = 64
MLA_ROPE = 32
MLA_V = 64
MLA_Q_LORA = 256
MLA_KV_LORA = 128
MLA_CACHE_W = MLA_KV_LORA + MLA_ROPE
ROPE_THETA = 10000.0
NSA_HEADS = 4
NSA_HD = 64
NSA_CMP_BLOCK = 32
NSA_CMP_STRIDE = 16
NSA_SEL_BLOCK = 64
NSA_N_SEL = 16
NSA_WINDOW = 512
NSA_FORCE = 1.0e4
D_FF = 4 * D_MODEL
EPS = 1e-6
IN_SPLITS = (D_LRU, D_LRU, 2 * D_CONF, MLA_Q_LORA, MLA_KV_LORA, MLA_ROPE, NSA_HEADS * NSA_HD, 6 * NSA_HD, 3 * NSA_HEADS)

C_LRU = 0
C_CONF = 512
C_CQ = 1024
C_CKV = 1280
C_NQ = 1408
C_NKV = 1664
C_MISC = 2048
D_INP = 2176
MISC_NG = 64

NEG = -1e30
LOG2E = 1.4426950408889634
VSUM_ROWS = 8
VMEM_LIMIT_V7X = 56 * 1024 * 1024
Q_TILE = 128
K_TILE = 512
SAMPLE_R = 128


def _cparams(sem):
    return pltpu.CompilerParams(dimension_semantics=sem, vmem_limit_bytes=VMEM_LIMIT_V7X)


def _rms(x, g):
    return x * lax.rsqrt(jnp.mean(x * x, axis=-1, keepdims=True) + EPS) * g


def _bdot(a, b):
    return jnp.dot(a.astype(BF16), b.astype(BF16), preferred_element_type=F32)


def _dot_nt(a, b):
    return lax.dot_general(a.astype(BF16), b.astype(BF16), (((1,), (1,)), ((), ())), preferred_element_type=F32)


def _dot_tn(a, b):
    return lax.dot_general(a.astype(BF16), b.astype(BF16), (((0,), (0,)), ((), ())), preferred_element_type=F32)


def _mod_kernel(c_ref, w_ref, b_ref, o_ref):
    o_ref[...] = _bdot(jax.nn.silu(c_ref[...]), w_ref[...]) + b_ref[...]


def _modulation(c_all, w_mod, b_mod):
    L, D, D6 = w_mod.shape
    rows = c_all.shape[0]
    tn = 1536
    return pl.pallas_call(
        _mod_kernel,
        grid=(L, D6 // tn),
        in_specs=[pl.BlockSpec((rows, D), lambda l, j: (0, 0)),
                  pl.BlockSpec((None, D, tn), lambda l, j: (l, 0, j)),
                  pl.BlockSpec((None, 1, tn), lambda l, j: (l, 0, j))],
        out_specs=pl.BlockSpec((None, rows, tn), lambda l, j: (l, 0, j)),
        out_shape=jax.ShapeDtypeStruct((L, rows, D6), F32),
        compiler_params=_cparams(("arbitrary", "arbitrary")),
        name="modulation",
    )(c_all, w_mod, b_mod.reshape(L, 1, D6))


def _pe_kernel(pe_ref, w_ref, o_ref):
    acc = jnp.zeros((8, 256), F32)
    for r in range(NSA_CMP_STRIDE):
        acc = acc + _bdot(pe_ref[r], w_ref[r])
    o_ref[...] = jnp.broadcast_to(acc[0:1, 0:128] + acc[1:2, 128:256], (8, 128))


def _pe_term(pe2, w4):
    L = pe2.shape[0]
    return pl.pallas_call(
        _pe_kernel,
        grid=(L,),
        in_specs=[pl.BlockSpec((None, NSA_CMP_STRIDE, 8, 128), lambda l: (l, 0, 0, 0)),
                  pl.BlockSpec((None, NSA_CMP_STRIDE, 128, 256), lambda l: (l, 0, 0, 0))],
        out_specs=pl.BlockSpec((None, 8, 128), lambda l: (l, 0, 0)),
        out_shape=jax.ShapeDtypeStruct((L, 8, 128), F32),
        compiler_params=_cparams(("arbitrary",)),
        name="pe_term",
    )(pe2, w4)


def _in_proj_kernel(x_ref, sh_ref, sc_ref, g_ref, w_ref, gq_ref, wuq_ref, wuk_ref, gkv_ref, cos_ref, sin_ref,
                    lru_ref, glu_ref, qm_ref, row_ref, nq_ref, nsa4_ref, win_ref, gate_ref):
    h = _rms(x_ref[...], g_ref[...]) * (1.0 + sc_ref[...]) + sh_ref[...]
    u = _bdot(h, w_ref[...])
    lru_ref[...] = u[:, C_LRU:C_LRU + 512]
    glu_ref[...] = u[:, C_CONF:C_CONF + 256] * jax.nn.sigmoid(u[:, C_CONF + 256:C_CONF + 512])
    nsa4_ref[...] = u[:, C_NKV:C_NKV + 256]
    win_ref[...] = u[:, C_NKV + 256:C_NKV + 384]
    misc = u[:, C_MISC:C_MISC + 128]
    gate_ref[...] = jax.nn.sigmoid(misc)
    cosv = cos_ref[...]
    sinv = sin_ref[...]
    qr = _bdot(_rms(u[:, C_CQ:C_CQ + 256], gq_ref[...]), wuq_ref[...])
    q_rope = qr[:, 256:384] * cosv + qr[:, 384:512] * sinv
    q_lat = _bdot(qr[:, 0:256], wuk_ref[...])
    scale = (MLA_NOPE + MLA_ROPE) ** -0.5 * LOG2E
    for hd in range(MLA_HEADS):
        qm_ref[hd, :, 0:128] = q_lat[:, 128 * hd:128 * (hd + 1)] * scale
        qm_ref[hd, :, 128:160] = q_rope[:, 32 * hd:32 * (hd + 1)] * scale
    row_ref[:, 0:128] = _rms(u[:, C_CKV:C_CKV + 128], gkv_ref[...])
    row_ref[:, 128:160] = misc[:, 0:32] * cosv[:, 0:32] + misc[:, 32:64] * sinv[:, 0:32]
    nq = u[:, C_NQ:C_NQ + 256] * (NSA_HD ** -0.5 * LOG2E)
    for hd in range(NSA_HEADS):
        nq_ref[hd] = nq[:, 64 * hd:64 * (hd + 1)]


def _in_proj(x, sh, sc, g, w_in, gq, wuq, wuk, gkv, cosv, sinv, *, per_token, tm):
    N, D = x.shape
    nt = N // tm
    if per_token:
        mod_spec = pl.BlockSpec((tm, D), lambda i: (i, 0))
        tab_spec = pl.BlockSpec((tm, 128), lambda i: (i, 0))
    else:
        per_b = nt // sh.shape[0]
        mod_spec = pl.BlockSpec((None, 1, D), lambda i: (i // per_b, 0, 0))
        tab_spec = pl.BlockSpec((tm, 128), lambda i: (i % per_b, 0))
    const = lambda shape: pl.BlockSpec(shape, lambda i: tuple(0 for _ in shape))
    row = lambda w: pl.BlockSpec((tm, w), lambda i: (i, 0))
    head = lambda w: pl.BlockSpec((4, tm, w), lambda i: (0, i, 0))
    return pl.pallas_call(
        _in_proj_kernel,
        grid=(nt,),
        in_specs=[row(D), mod_spec, mod_spec, const((1, D)), const((D, D_INP)), const((1, 256)),
                  const((256, 512)), const((256, 512)), const((1, 128)), tab_spec, tab_spec],
        out_specs=[row(512), row(256), head(MLA_CACHE_W), row(MLA_CACHE_W), head(NSA_HD), row(256), row(128), row(128)],
        out_shape=[jax.ShapeDtypeStruct((N, 512), F32), jax.ShapeDtypeStruct((N, 256), F32),
                   jax.ShapeDtypeStruct((4, N, MLA_CACHE_W), F32), jax.ShapeDtypeStruct((N, MLA_CACHE_W), F32),
                   jax.ShapeDtypeStruct((4, N, NSA_HD), F32), jax.ShapeDtypeStruct((N, 256), F32),
                   jax.ShapeDtypeStruct((N, 128), F32), jax.ShapeDtypeStruct((N, 128), F32)],
        compiler_params=_cparams(("parallel",)),
        name="in_proj",
    )(x, sh, sc, g, w_in, gq, wuq, wuk, gkv, cosv, sinv)


def _post_kernel(x_ref, a_ref, b_ref, c_ref, d_ref, gt1_ref, sh2_ref, sc2_ref, gt2_ref,
                 wo_ref, gpm_ref, gpf_ref, gqf_ref, w1_ref, w2_ref, o_ref):
    y = _bdot(a_ref[...], wo_ref[0:256, :])
    y = y + _bdot(b_ref[...], wo_ref[256:512, :])
    y = y + _bdot(c_ref[...], wo_ref[512:768, :])
    y = y + _bdot(d_ref[...], wo_ref[768:1024, :])
    x = x_ref[...] + gt1_ref[...] * _rms(y, gpm_ref[...])
    h = (_rms(x, gpf_ref[...]) * (1.0 + sc2_ref[...]) + sh2_ref[...]).astype(BF16)
    f = jnp.zeros(x.shape, F32)
    fc = 1024
    for j in range(D_FF // fc):
        t = jnp.maximum(jnp.dot(h, w1_ref[:, j * fc:(j + 1) * fc], preferred_element_type=F32), 0.0)
        f = f + _bdot(t * t, w2_ref[j * fc:(j + 1) * fc, :])
    o_ref[...] = x + gt2_ref[...] * _rms(f, gqf_ref[...])


def _post(x, a, b, c, d, gt1, sh2, sc2, gt2, wo, gpm, gpf, gqf, w1, w2, *, per_token, tm):
    N, D = x.shape
    nt = N // tm
    if per_token:
        mod_spec = pl.BlockSpec((tm, D), lambda i: (i, 0))
    else:
        per_b = nt // gt1.shape[0]
        mod_spec = pl.BlockSpec((None, 1, D), lambda i: (i // per_b, 0, 0))
    const = lambda shape: pl.BlockSpec(shape, lambda i: tuple(0 for _ in shape))
    row = lambda w: pl.BlockSpec((tm, w), lambda i: (i, 0))
    return pl.pallas_call(
        _post_kernel,
        grid=(nt,),
        in_specs=[row(D), row(256), row(256), row(256), row(256), mod_spec, mod_spec, mod_spec, mod_spec,
                  const((D, D)), const((1, D)), const((1, D)), const((1, D)), const((D, D_FF)), const((D_FF, D))],
        out_specs=row(D),
        out_shape=jax.ShapeDtypeStruct((N, D), F32),
        compiler_params=_cparams(("parallel",)),
        name="post",
    )(x, a, b, c, d, gt1, sh2, sc2, gt2, wo, gpm, gpf, gqf, w1, w2)


def _lru_gates(y, gate_w, lam):
    wa, ba, wx, bx = gate_w
    r = jax.nn.sigmoid(_bdot(y, wa) + ba)
    i = jax.nn.sigmoid(_bdot(y, wx) + bx)
    log_a = -LRU_C * r * jax.nn.softplus(-lam)
    return jnp.exp(log_a), jnp.sqrt(1.0 - jnp.exp(2.0 * log_a)) * i * y


def _layernorm(y, g, b):
    mu = jnp.mean(y, axis=-1, keepdims=True)
    var = jnp.mean(jnp.square(y - mu), axis=-1, keepdims=True)
    return (y - mu) * lax.rsqrt(var + EPS) * g + b


def _seq_prompt_kernel(xg_ref, glu_ref, cw_ref, cb_ref, wa_ref, ba_ref, wx_ref, bx_ref, lam_ref,
                       dw_ref, db_ref, lng_ref, lnb_ref,
                       a_ref, b_ref, hl_ref, xbuf, gbuf, abuf, dbuf, hbuf):
    i = pl.program_id(0)
    nb, tc, _ = glu_ref.shape

    @pl.when(i == 0)
    def _():
        xbuf[:, 0:8, :] = jnp.zeros((nb, 8, D_LRU), F32)
        gbuf[:, 0:32, :] = jnp.zeros((nb, 32, D_CONF), F32)
        hbuf[...] = jnp.zeros(hbuf.shape, F32)

    xbuf[:, 8:8 + tc, :] = xg_ref[:, :, 0:D_LRU]
    y = jnp.zeros((nb, tc, D_LRU), F32) + cb_ref[...]
    for k in range(LRU_CONV):
        y = y + xbuf[:, 8 - (LRU_CONV - 1) + k:8 - (LRU_CONV - 1) + k + tc, :] * cw_ref[k:k + 1, :]
    a, drive = _lru_gates(y.reshape(nb * tc, D_LRU),
                          (wa_ref[...], ba_ref[...], wx_ref[...], bx_ref[...]), lam_ref[...])
    abuf[...] = a.reshape(nb, tc, D_LRU)
    dbuf[...] = drive.reshape(nb, tc, D_LRU)

    def step(t, h):
        h = abuf[:, pl.ds(t, 1), :] * h + dbuf[:, pl.ds(t, 1), :]
        dbuf[:, pl.ds(t, 1), :] = h
        return h

    h = lax.fori_loop(0, tc, step, hbuf[...], unroll=8)
    hbuf[...] = h
    hl_ref[...] = jnp.broadcast_to(h, hl_ref.shape)
    a_ref[...] = dbuf[...] * jax.nn.gelu(xg_ref[:, :, D_LRU:2 * D_LRU])
    xbuf[:, 0:8, :] = xbuf[:, tc:tc + 8, :]

    gbuf[:, 32:32 + tc, :] = glu_ref[...]
    z = jnp.zeros((nb, tc, D_CONF), F32) + db_ref[...]
    for k in range(CONF_KERNEL):
        o = 32 - (CONF_KERNEL - 1) + k
        z = z + gbuf[:, o:o + tc, :] * dw_ref[k:k + 1, :]
    b_ref[...] = jax.nn.silu(_layernorm(z, lng_ref[...], lnb_ref[...]))
    gbuf[:, 0:32, :] = gbuf[:, tc:tc + 32, :]


def _seq_prompt(xg, glu, P, *, tc):
    nb, S, _ = glu.shape
    const = lambda shape: pl.BlockSpec(shape, lambda i: tuple(0 for _ in shape))
    seq = lambda w: pl.BlockSpec((nb, tc, w), lambda i: (0, i, 0))
    return pl.pallas_call(
        _seq_prompt_kernel,
        grid=(S // tc,),
        in_specs=[seq(512), seq(256), const((LRU_CONV, 256)), const((1, 256)), const((256, 256)), const((1, 256)),
                  const((256, 256)), const((1, 256)), const((1, 256)), const((CONF_KERNEL, 256)), const((1, 256)),
                  const((1, 256)), const((1, 256))],
        out_specs=[seq(256), seq(256), const((nb, 8, 256))],
        out_shape=[jax.ShapeDtypeStruct((nb, S, 256), F32), jax.ShapeDtypeStruct((nb, S, 256), F32),
                   jax.ShapeDtypeStruct((nb, 8, 256), F32)],
        scratch_shapes=[pltpu.VMEM((nb, tc + 8, 256), F32), pltpu.VMEM((nb, tc + 32, 256), F32),
                        pltpu.VMEM((nb, tc, 256), F32), pltpu.VMEM((nb, tc, 256), F32),
                        pltpu.VMEM((nb, 1, 256), F32)],
        compiler_params=_cparams(("arbitrary",)),
        name="seq_prompt",
    )(xg, glu, P['lru_conv_w'], P['lru_conv_b'], P['lru_wa_bd'], P['lru_ba'], P['lru_wx_bd'], P['lru_bx'],
      P['lru_lam'], P['conf_dw_w'], P['conf_dw_b'], P['conf_ln_g'], P['conf_ln_b'])


def _seq_sample_kernel(xg_ref, glu_ref, cs_ref, gs_ref, h0_ref, cw_ref, cb_ref, wa_ref, ba_ref, wx_ref, bx_ref,
                       lam_ref, dw_ref, db_ref, lng_ref, lnb_ref,
                       a_ref, b_ref, hl_ref, ncs_ref, ngs_ref, xbuf, gbuf):
    T, nb, _ = glu_ref.shape
    nc, ng = LRU_CONV - 1, CONF_KERNEL - 1
    xbuf[0:nc] = cs_ref[...]
    xbuf[nc:nc + T] = xg_ref[:, :, 0:D_LRU]
    y = jnp.zeros((T, nb, D_LRU), F32) + cb_ref[...]
    for k in range(LRU_CONV):
        y = y + xbuf[k:k + T] * cw_ref[k:k + 1, :]
    a, drive = _lru_gates(y.reshape(T * nb, D_LRU),
                          (wa_ref[...], ba_ref[...], wx_ref[...], bx_ref[...]), lam_ref[...])
    a = a.reshape(T, nb, D_LRU)
    drive = drive.reshape(T, nb, D_LRU)
    h = h0_ref[...]
    for t in range(T):
        h = a[t] * h + drive[t]
        a_ref[t] = h * jax.nn.gelu(xg_ref[t, :, D_LRU:2 * D_LRU])
    hl_ref[...] = h
    ncs_ref[...] = xbuf[T:T + nc]

    gbuf[0:ng] = gs_ref[...]
    gbuf[ng:ng + T] = glu_ref[...]
    z = jnp.zeros((T, nb, D_CONF), F32) + db_ref[...]
    for k in range(CONF_KERNEL):
        z = z + gbuf[k:k + T] * dw_ref[k:k + 1, :]
    b_ref[...] = jax.nn.silu(_layernorm(z, lng_ref[...], lnb_ref[...]))
    ngs_ref[...] = gbuf[T:T + ng]


def _seq_sample(xg_t, glu_t, cs_t, gs_t, h0, P):
    T, nb, _ = glu_t.shape
    nc, ng = LRU_CONV - 1, CONF_KERNEL - 1
    full = lambda a: pl.BlockSpec(a.shape, lambda i: tuple(0 for _ in a.shape))
    args = (xg_t, glu_t, cs_t, gs_t, h0, P['lru_conv_w'], P['lru_conv_b'], P['lru_wa_bd'], P['lru_ba'],
            P['lru_wx_bd'], P['lru_bx'], P['lru_lam'], P['conf_dw_w'], P['conf_dw_b'], P['conf_ln_g'], P['conf_ln_b'])
    outs = [jax.ShapeDtypeStruct((T, nb, 256), F32), jax.ShapeDtypeStruct((T, nb, 256), F32),
            jax.ShapeDtypeStruct((nb, 256), F32), jax.ShapeDtypeStruct((nc, nb, 256), F32),
            jax.ShapeDtypeStruct((ng, nb, 256), F32)]
    return pl.pallas_call(
        _seq_sample_kernel,
        grid=(1,),
        in_specs=[full(a) for a in args],
        out_specs=[full(o) for o in outs],
        out_shape=outs,
        scratch_shapes=[pltpu.VMEM((T + nc, nb, 256), F32), pltpu.VMEM((T + ng, nb, 256), F32)],
        compiler_params=_cparams(("arbitrary",)),
        name="seq_sample",
    )(*args)


def _online_update(s, v_t, m_ref, acc_ref):
    m_prev = m_ref[...]
    m_new = jnp.maximum(m_prev, jnp.max(s, axis=0, keepdims=True))
    p = jnp.exp2(s - m_new)
    acc_ref[...] = jnp.exp2(m_prev - m_new) * acc_ref[...] + jnp.dot(v_t, p.astype(BF16), preferred_element_type=F32)
    m_ref[...] = m_new


def _softmax_pv(s, v_t):
    p = jnp.exp2(s - jnp.max(s, axis=0, keepdims=True))
    acc = jnp.dot(v_t, p.astype(BF16), preferred_element_type=F32)
    dv = acc.shape[0] - VSUM_ROWS
    return acc[0:dv] / jnp.maximum(acc[dv:dv + 1], 1e-30)


def _init_stats(m_ref, acc_ref, l_ref=None):
    m_ref[...] = jnp.full(m_ref.shape, NEG, F32)
    acc_ref[...] = jnp.zeros(acc_ref.shape, F32)
    if l_ref is not None:
        l_ref[...] = jnp.zeros(l_ref.shape, F32)


def _finish_t(acc_ref):
    dv = acc_ref.shape[0] - VSUM_ROWS
    return acc_ref[0:dv, :] / jnp.maximum(acc_ref[dv:dv + 1, :], 1e-30)


def _finish(l_ref, acc_ref):
    return acc_ref[...] / jnp.maximum(l_ref[...], 1e-30)


def _compressed_kv(p_ref, pe_ref, nc):
    ncp = p_ref.shape[0]
    top = p_ref[:, 0:128]
    bot = pltpu.roll(p_ref[:, 128:256], ncp - 1, 0)
    n = lax.broadcasted_iota(jnp.int32, (ncp, 128), 0)
    return jnp.where(n < nc, top + bot + pe_ref[0:1, :], 0.0)


def _compressed_branch(q, kcv, q_pos, nc):
    ncp = kcv.shape[0]
    s = _dot_nt(kcv[:, 0:64], q)
    n = lax.broadcasted_iota(jnp.int32, s.shape, 0)
    mask = (n * NSA_CMP_STRIDE + (NSA_CMP_BLOCK - 1) <= q_pos) & (n < nc)
    s = jnp.where(mask, s, NEG)
    e = jnp.where(mask, jnp.exp2(s - jnp.max(s, axis=0, keepdims=True)), 0.0)
    p = e / jnp.maximum(jnp.sum(e, axis=0, keepdims=True), 1e-30)
    o = _dot_tn(kcv[:, 64:128], p)
    return o, p


def _select_blocks(imp, q_pos, n_sel):
    blk = lax.broadcasted_iota(jnp.int32, imp.shape, 0)
    cur = q_pos // NSA_SEL_BLOCK
    valid = blk * NSA_SEL_BLOCK <= q_pos
    forced = (blk == 0) | (blk == cur) | (blk == cur - 1)
    score = jnp.where(valid, imp + jnp.where(forced, NSA_FORCE, 0.0), -1.0)
    big = jnp.int32(2 ** 30)
    sel = jnp.zeros(imp.shape, jnp.bool_)
    for _ in range(n_sel):
        mx = jnp.max(score, axis=0, keepdims=True)
        first = jnp.min(jnp.where(score == mx, blk, big), axis=0, keepdims=True)
        hit = blk == first
        sel = sel | hit
        score = jnp.where(hit, -3e38, score)
    return jnp.where(sel, 0.0, NEG)


def _cmp_prompt_kernel(x_ref, w_ref, p_ref):
    n_chunk = p_ref.shape[0]
    acc = jnp.zeros(p_ref.shape, F32)
    for r in range(NSA_CMP_STRIDE):
        acc = acc + _bdot(x_ref[pl.ds(r, n_chunk, stride=NSA_CMP_STRIDE), :], w_ref[r])
    p_ref[...] = acc


def _cmp_prompt(nsa4, w4, nb):
    N = nsa4.shape[0]
    S = N // nb
    n_chunk = S // NSA_CMP_STRIDE
    return pl.pallas_call(
        _cmp_prompt_kernel,
        grid=(nb,),
        in_specs=[pl.BlockSpec((S, 128), lambda b: (b, 0)),
                  pl.BlockSpec((NSA_CMP_STRIDE, 128, 256), lambda b: (0, 0, 0))],
        out_specs=pl.BlockSpec((None, n_chunk, 256), lambda b: (b, 0, 0)),
        out_shape=jax.ShapeDtypeStruct((nb, n_chunk, 256), F32),
        compiler_params=_cparams(("parallel",)),
        name="cmp_prompt",
    )(nsa4, w4)


def _attn_prompt_kernel(qm_ref, km_ref, vmt_ref, wuvt_ref, q_ref, g_ref, p_ref, pe_ref, ov_ref, ks_ref, vst_ref,
                        kw_ref, vwt_ref, c_ref, o_ref,
                        m_ref, acc_ref, mm_ref, accm_ref,
                        sma_ref, smb_ref, ssa_ref, ssb_ref, *, nc, n_sel):
    i = pl.program_id(1)
    _, tq, _ = q_ref.shape
    tk = ks_ref.shape[1]
    tw = kw_ref.shape[1]
    R = NSA_HEADS * tq
    q_f32 = q_ref[...].reshape(R, NSA_HD)
    q = q_f32.astype(BF16)
    qm = qm_ref[...].reshape(R, MLA_CACHE_W).astype(BF16)
    q_pos_q = i * tq + lax.broadcasted_iota(jnp.int32, (1, tq), 1)
    q_pos = jnp.concatenate([q_pos_q] * NSA_HEADS, axis=1)

    kcv = _compressed_kv(p_ref, pe_ref, nc)
    o_c, p_c = _compressed_branch(q, kcv, q_pos, nc)
    imp4 = jnp.dot(ov_ref[...], p_c.astype(BF16), preferred_element_type=F32)
    imp = imp4[:, 0:tq]
    for hd in range(1, NSA_HEADS):
        imp = imp + imp4[:, hd * tq:(hd + 1) * tq]
    bias = _select_blocks(imp, q_pos_q, n_sel).T
    q_sel = jnp.concatenate([jnp.concatenate([bias] * NSA_HEADS, axis=0), q_f32], axis=1).astype(BF16)

    n_full = (i * tq) // tk
    _init_stats(m_ref, acc_ref)
    _init_stats(mm_ref, accm_ref)

    def score_m(j):
        return _dot_nt(km_ref[j], qm)

    def score_s(j):
        return _dot_nt(ks_ref[j], q_sel)

    def use_m(s, j):
        _online_update(s, vmt_ref[j], mm_ref, accm_ref)

    def use_s(s, j):
        _online_update(s, vst_ref[j], m_ref, acc_ref)

    causal = n_full * tk + lax.broadcasted_iota(jnp.int32, (tk, R), 0) <= q_pos
    use_m(jnp.where(causal, score_m(n_full), NEG), n_full)
    use_s(jnp.where(causal, score_s(n_full), NEG), n_full)

    @pl.when(n_full > 0)
    def _():
        sma_ref[...] = score_m(0)
        ssa_ref[...] = score_s(0)

    def pair(jj, c):
        j = 2 * jj
        smb_ref[...] = score_m(j + 1)
        ssb_ref[...] = score_s(j + 1)
        use_m(sma_ref[...], j)
        use_s(ssa_ref[...], j)
        nxt = jnp.minimum(j + 2, n_full - 1)
        sma_ref[...] = score_m(nxt)
        ssa_ref[...] = score_s(nxt)
        use_m(smb_ref[...], j + 1)
        use_s(ssb_ref[...], j + 1)
        return c

    lax.fori_loop(0, n_full // 2, pair, 0)

    @pl.when(n_full % 2 == 1)
    def _():
        use_m(sma_ref[...], n_full - 1)
        use_s(ssa_ref[...], n_full - 1)

    o_s = _finish_t(acc_ref)
    o_m = _finish_t(accm_ref)
    c_t = jnp.concatenate([_bdot(wuvt_ref[hd], o_m[:, hd * tq:(hd + 1) * tq]) for hd in range(MLA_HEADS)], axis=0)
    c_ref[...] = c_t.T

    n_wt = NSA_WINDOW // tw
    kw = jnp.concatenate([kw_ref[i + c] for c in range(n_wt + 1)], axis=0)
    vw_t = jnp.concatenate([vwt_ref[i + c] for c in range(n_wt + 1)], axis=1)
    k_pos = (i - n_wt) * tw + lax.broadcasted_iota(jnp.int32, (kw.shape[0], R), 0)
    ok = (k_pos >= 0) & (k_pos <= q_pos) & (k_pos > q_pos - NSA_WINDOW)
    o_w = _softmax_pv(jnp.where(ok, _dot_nt(kw, q), NEG), vw_t)

    g_t = g_ref[...].T
    outs = []
    for hd in range(NSA_HEADS):
        sl = slice(hd * tq, (hd + 1) * tq)
        r0 = MISC_NG + 3 * hd
        outs.append(g_t[r0:r0 + 1, :] * o_c[:, sl] + g_t[r0 + 1:r0 + 2, :] * o_s[:, sl]
                    + g_t[r0 + 2:r0 + 3, :] * o_w[:, sl])
    o_ref[...] = jnp.concatenate(outs, axis=0).T


def _attn_prompt(qm, km3, vmt3, wuvt, q, gates, P, pe, ov, ks3, vst3, kw3, vwt3, *, tq, nc, n_sel):
    nb, nk, tk, dks = ks3.shape
    nw, tw = kw3.shape[1:3]
    assert tw == tq
    N = q.shape[1]
    nq = N // nb // tq
    ncp = P.shape[1]
    nblk = ov.shape[0]
    R = 4 * tq
    out = pl.BlockSpec((tq, 256), lambda b, i: (b * nq + i, 0))
    whole = lambda a: pl.BlockSpec((None,) + a.shape[1:], lambda b, i: (b,) + (0,) * (a.ndim - 1))
    return pl.pallas_call(
        functools.partial(_attn_prompt_kernel, nc=nc, n_sel=n_sel),
        grid=(nb, nq),
        in_specs=[pl.BlockSpec((4, tq, MLA_CACHE_W), lambda b, i: (0, b * nq + i, 0)),
                  whole(km3), whole(vmt3),
                  pl.BlockSpec((4, MLA_V, MLA_KV_LORA), lambda b, i: (0, 0, 0)),
                  pl.BlockSpec((4, tq, NSA_HD), lambda b, i: (0, b * nq + i, 0)),
                  pl.BlockSpec((tq, 128), lambda b, i: (b * nq + i, 0)),
                  pl.BlockSpec((None, ncp, 256), lambda b, i: (b, 0, 0)),
                  pl.BlockSpec((8, 128), lambda b, i: (0, 0)),
                  pl.BlockSpec((nblk, ncp), lambda b, i: (0, 0)),
                  whole(ks3), whole(vst3), whole(kw3), whole(vwt3)],
        out_specs=[out, out],
        out_shape=[jax.ShapeDtypeStruct((N, 256), F32), jax.ShapeDtypeStruct((N, 256), F32)],
        scratch_shapes=[pltpu.VMEM((1, R), F32), pltpu.VMEM((NSA_HD + VSUM_ROWS, R), F32),
                        pltpu.VMEM((1, R), F32), pltpu.VMEM((MLA_KV_LORA + VSUM_ROWS, R), F32)]
        + [pltpu.VMEM((tk, R), F32) for _ in range(4)],
        compiler_params=_cparams(("arbitrary", "arbitrary")),
        name="attn_prompt",
    )(qm, km3, vmt3, wuvt, q, gates, P, pe, ov, ks3, vst3, kw3, vwt3)


def _sample_queries(q_ref):
    nh, T, d = q_ref.shape
    q = q_ref[...].reshape(nh * T, d)
    return jnp.concatenate([q, jnp.zeros((SAMPLE_R - nh * T, d), F32)], axis=0).astype(BF16)


def _rows_attend(parts):
    m = functools.reduce(jnp.maximum, [jnp.max(s, axis=1, keepdims=True) for s, _ in parts])
    num, den = 0.0, 0.0
    for s, pv in parts:
        p = jnp.exp2(s - m)
        den = den + jnp.sum(p, axis=1, keepdims=True)
        num = num + pv(p.astype(BF16))
    return num / jnp.maximum(den, 1e-30)


def _page_copy(cache_ref, pt_ref, buf_ref, sem_ref, layer, n_pages, b, slot, j):
    return pltpu.make_async_copy(cache_ref.at[layer, pt_ref[b * n_pages + j]], buf_ref.at[slot, j], sem_ref.at[slot])


def _fetch_pages(cache_ref, pt_ref, buf_ref, sem_ref, layer, n_pages):
    b = pl.program_id(0)
    slot = b % 2
    copy = functools.partial(_page_copy, cache_ref, pt_ref, buf_ref, sem_ref, layer, n_pages)

    @pl.when(b == 0)
    def _():
        for j in range(n_pages):
            copy(b, slot, j).start()

    @pl.when(b + 1 < pl.num_programs(0))
    def _():
        for j in range(n_pages):
            copy(b + 1, 1 - slot, j).start()

    for j in range(n_pages):
        copy(b, slot, j).wait()
    return slot


def _mla_sample_kernel(pt_ref, cache_ref, q_ref, new_ref, wuv_ref, o_ref, buf_ref, sem_ref, *, layer, n_pages):
    slot = _fetch_pages(cache_ref, pt_ref, buf_ref, sem_ref, layer, n_pages)
    _, T, dk = q_ref.shape
    R = MLA_HEADS * T
    q = q_ref[...].reshape(R, dk).astype(BF16)
    kt = jnp.concatenate([buf_ref[slot, j].astype(BF16) for j in range(n_pages)], axis=1)
    vt = kt[0:MLA_KV_LORA]
    new = new_ref[...].astype(BF16)
    k_t = lax.broadcasted_iota(jnp.int32, (R, T), 1)
    q_t = lax.broadcasted_iota(jnp.int32, (R, T), 0) % T
    o = _rows_attend([
        (jnp.dot(q, kt, preferred_element_type=F32), lambda p: _dot_nt(p, vt)),
        (jnp.where(k_t <= q_t, _dot_nt(q, new), NEG),
         lambda p: jnp.dot(p, new[:, 0:MLA_KV_LORA], preferred_element_type=F32))])
    o_ref[...] = jnp.concatenate([_bdot(o[hd * T:(hd + 1) * T, :], wuv_ref[hd]) for hd in range(MLA_HEADS)], axis=1)


def _mla_sample(cache_mla_t, pt_flat, q, rows, wuv, layer, nb, n_pages):
    T = q.shape[1] // nb
    N = q.shape[1]
    page = cache_mla_t.shape[3]
    grid_spec = pltpu.PrefetchScalarGridSpec(
        num_scalar_prefetch=1,
        grid=(nb,),
        in_specs=[pl.BlockSpec(memory_space=pl.ANY),
                  pl.BlockSpec((4, T, MLA_CACHE_W), lambda b, pt: (0, b, 0)),
                  pl.BlockSpec((T, MLA_CACHE_W), lambda b, pt: (b, 0)),
                  pl.BlockSpec((4, MLA_KV_LORA, MLA_V), lambda b, pt: (0, 0, 0))],
        out_specs=pl.BlockSpec((T, 256), lambda b, pt: (b, 0)),
        scratch_shapes=[pltpu.VMEM((2, n_pages, MLA_CACHE_W, page), F32), pltpu.SemaphoreType.DMA((2,))],
    )
    return pl.pallas_call(
        functools.partial(_mla_sample_kernel, layer=layer, n_pages=n_pages),
        grid_spec=grid_spec,
        out_shape=jax.ShapeDtypeStruct((N, 256), F32),
        compiler_params=_cparams(("arbitrary",)),
        name="mla_sample",
    )(pt_flat, cache_mla_t, q, rows, wuv)


def _nsa_sample_kernel(pt_ref, cache_ref, q_ref, g_ref, w4_ref, perm_ref, pe_ref, ov_ref, e_ref, new_ref, wc_ref,
                       wn_ref, o_ref, buf_ref, sem_ref, y_ref, *, layer, n_pages, nc, n_sel, past):
    slot = _fetch_pages(cache_ref, pt_ref, buf_ref, sem_ref, layer, n_pages)
    _, T, _ = q_ref.shape
    R = NSA_HEADS * T
    half = n_pages // 2
    cpp = y_ref.shape[1] // NSA_CMP_STRIDE
    q_f32 = q_ref[...].reshape(R, NSA_HD)
    q = q_f32.astype(BF16)
    q_pos = past + lax.broadcasted_iota(jnp.int32, (R, 1), 0) % T

    for jp in range(half):
        pair = jnp.concatenate([buf_ref[slot, jp, 0:128, :], buf_ref[slot, jp + half, 0:128, :]], axis=0)
        y_ref[jp] = _dot_nt(perm_ref[...], pair)
    slabs = []
    for r in range(NSA_CMP_STRIDE):
        yr = y_ref[:, r * cpp:(r + 1) * cpp, :].reshape(half * cpp, 256)
        slabs.append(jnp.concatenate([yr[:, 0:128], yr[:, 128:256]], axis=0))
    P = _bdot(jnp.concatenate(slabs, axis=1), w4_ref[...].reshape(NSA_CMP_STRIDE * 128, 256))

    lane = lax.broadcasted_iota(jnp.int32, (1, SAMPLE_R), 1)
    q_pos_l = past + lane % T
    kcv = _compressed_kv(P, pe_ref, nc)
    o_c, p_c = _compressed_branch(_sample_queries(q_ref), kcv, q_pos_l, nc)
    o_c = o_c.T[0:R]
    imp = jnp.dot(ov_ref[...], p_c.astype(BF16), preferred_element_type=F32)
    imp = jnp.where(lane < R, imp, 0.0)
    tot = imp
    for k in range(1, NSA_HEADS):
        tot = tot + pltpu.roll(imp, k * T, 1) + pltpu.roll(imp, SAMPLE_R - R + k * T, 1)
    n_real = -(-(-(-(past + T) // NSA_SEL_BLOCK)) // 8) * 8
    bias = jnp.concatenate([_select_blocks(tot[0:n_real], q_pos_l, n_sel),
                            jnp.full((tot.shape[0] - n_real, SAMPLE_R), NEG, F32)], axis=0).T[0:R]

    n_blk_past = e_ref.shape[0]
    kvt = jnp.concatenate([buf_ref[slot, j, 128:256, :].astype(BF16) for j in range(n_pages)], axis=1)
    k_sel = jnp.concatenate([e_ref[...], kvt[0:64]], axis=0)
    q_sel = jnp.concatenate([bias[:, 0:n_blk_past], q_f32], axis=1).astype(BF16)
    vs_t = kvt[64:128]
    new = new_ref[...].astype(BF16)
    k_new = past + lax.broadcasted_iota(jnp.int32, (R, T), 1)
    s_new = _dot_nt(q, new[:, 128:192]) + bias[:, n_blk_past:n_blk_past + 1]
    o_s = _rows_attend([
        (jnp.dot(q_sel, k_sel, preferred_element_type=F32), lambda p: _dot_nt(p, vs_t)),
        (jnp.where(k_new <= q_pos, s_new, NEG), lambda p: jnp.dot(p, new[:, 192:256], preferred_element_type=F32))])

    wc = wc_ref[...].astype(BF16)
    n_buf = wc.shape[1]
    k_buf = past - n_buf + lax.broadcasted_iota(jnp.int32, (R, n_buf), 1)
    ok_buf = (k_buf >= 0) & (k_buf <= q_pos) & (k_buf > q_pos - NSA_WINDOW)
    wn = wn_ref[...].astype(BF16)
    ok_new = (k_new <= q_pos) & (k_new > q_pos - NSA_WINDOW)
    o_w = _rows_attend([
        (jnp.where(ok_buf, jnp.dot(q, wc[0:64], preferred_element_type=F32), NEG), lambda p: _dot_nt(p, wc[64:128])),
        (jnp.where(ok_new, _dot_nt(q, wn[:, 0:64]), NEG),
         lambda p: jnp.dot(p, wn[:, 64:128], preferred_element_type=F32))])

    g = g_ref[...]
    outs = []
    for hd in range(NSA_HEADS):
        sl = slice(hd * T, (hd + 1) * T)
        r0 = MISC_NG + 3 * hd
        outs.append(g[:, r0:r0 + 1] * o_c[sl] + g[:, r0 + 1:r0 + 2] * o_s[sl] + g[:, r0 + 2:r0 + 3] * o_w[sl])
    o_ref[...] = jnp.concatenate(outs, axis=1)


def _nsa_sample(cache_nsa_t, pt_flat, q, gates, w4, pe, ov, new4, win_cache_t, win_new, layer, nb, n_pages,
                *, nc, n_sel, past):
    N = q.shape[1]
    T = N // nb
    nblk, ncp = ov.shape
    n_buf = win_cache_t.shape[3]
    page = cache_nsa_t.shape[3]
    cpp = page // NSA_CMP_STRIDE
    assert n_pages % 2 == 0 and cpp == 8 and ncp == n_pages * cpp
    n_blk_past = n_pages * page // NSA_SEL_BLOCK
    key_blk = np.arange(n_pages * page)[None, :] // NSA_SEL_BLOCK
    onehot = jnp.asarray((key_blk == np.arange(n_blk_past)[:, None]).astype(np.float32), dtype=BF16)
    row = np.arange(page)
    perm = jnp.asarray((np.arange(page)[None, :] == ((row % cpp) * NSA_CMP_STRIDE + row // cpp)[:, None])
                       .astype(np.float32), dtype=BF16)
    const = lambda a: pl.BlockSpec(a.shape, lambda b, pt: (0,) * a.ndim)
    grid_spec = pltpu.PrefetchScalarGridSpec(
        num_scalar_prefetch=1,
        grid=(nb,),
        in_specs=[pl.BlockSpec(memory_space=pl.ANY),
                  pl.BlockSpec((4, T, NSA_HD), lambda b, pt: (0, b, 0)),
                  pl.BlockSpec((T, 128), lambda b, pt: (b, 0)),
                  const(w4), const(perm), const(pe), const(ov), const(onehot),
                  pl.BlockSpec((T, 256), lambda b, pt: (b, 0)),
                  pl.BlockSpec((None, None, 128, n_buf), lambda b, pt: (layer, b, 0, 0)),
                  pl.BlockSpec((T, 128), lambda b, pt: (b, 0))],
        out_specs=pl.BlockSpec((T, 256), lambda b, pt: (b, 0)),
        scratch_shapes=[pltpu.VMEM((2, n_pages, 4 * NSA_HD, page), F32), pltpu.SemaphoreType.DMA((2,)),
                        pltpu.VMEM((n_pages // 2, page, 256), F32)],
    )
    return pl.pallas_call(
        functools.partial(_nsa_sample_kernel, layer=layer, n_pages=n_pages, nc=nc, n_sel=n_sel, past=past),
        grid_spec=grid_spec,
        out_shape=jax.ShapeDtypeStruct((N, 256), F32),
        compiler_params=_cparams(("arbitrary",)),
        name="nsa_sample",
    )(pt_flat, cache_nsa_t, q, gates, w4, perm, pe, ov, onehot, new4, win_cache_t, win_new)


def _block_diag(w):
    n, a, b = w.shape
    out = jnp.zeros((n * a, n * b), w.dtype)
    for j in range(n):
        out = out.at[j * a:(j + 1) * a, j * b:(j + 1) * b].set(w[j])
    return out


def _rot_cols(w):
    half = w.shape[-1] // 2
    return jnp.concatenate([-w[..., half:], w[..., :half]], axis=-1)


def _layer_params(l, W):
    idx = np.cumsum(IN_SPLITS)[:-1].tolist()
    lru_x, lru_g, conf_u, cq, ckv, kr, nq, nkv, ng = jnp.split(W['w_in'][l], idx, axis=-1)
    pad = jnp.zeros((D_MODEL, D_INP - C_MISC - 2 * MLA_ROPE - 3 * NSA_HEADS), F32)
    w_in = jnp.concatenate([lru_x, lru_g, conf_u, cq, ckv, nq, nkv, kr, _rot_cols(kr), ng, pad], axis=-1)
    wuq = W['mla_wuq'][l].reshape(MLA_Q_LORA, MLA_HEADS, MLA_NOPE + MLA_ROPE)
    wuq_rope = wuq[:, :, MLA_NOPE:]
    wuq = jnp.concatenate([wuq[:, :, :MLA_NOPE].reshape(MLA_Q_LORA, -1), wuq_rope.reshape(MLA_Q_LORA, -1),
                           _rot_cols(wuq_rope).reshape(MLA_Q_LORA, -1)], axis=-1)
    wuk = _block_diag(jnp.transpose(W['mla_wuk'][l], (1, 2, 0)))
    wuv = jnp.transpose(W['mla_wuv'][l], (1, 0, 2))
    wk = W['nsa_wc_k'][l].reshape(NSA_CMP_BLOCK, NSA_HD, NSA_HD)
    wv = W['nsa_wc_v'][l].reshape(NSA_CMP_BLOCK, NSA_HD, NSA_HD)
    z = jnp.zeros((NSA_CMP_STRIDE, NSA_HD, NSA_HD), F32)
    w4 = jnp.concatenate([jnp.concatenate([wk[:16], z, wk[16:], z], axis=2),
                          jnp.concatenate([z, wv[:16], z, wv[16:]], axis=2)], axis=1)
    pe = jnp.concatenate([W['nsa_pe_k'][l], W['nsa_pe_v'][l]], axis=-1)
    pe2 = jnp.concatenate([pe[:16, None], pe[16:, None], jnp.zeros((16, 6, 128), F32)], axis=1)
    r1 = lambda a: a.reshape(1, -1)
    return dict(
        w_in=w_in.astype(BF16), g_pre_mix=r1(W['g_pre_mix'][l]), g_post_mix=r1(W['g_post_mix'][l]),
        g_pre_ffn=r1(W['g_pre_ffn'][l]), g_post_ffn=r1(W['g_post_ffn'][l]),
        mla_gq=r1(W['mla_gq'][l]), mla_gkv=r1(W['mla_gkv'][l]),
        wuq=wuq.astype(BF16), wuk=wuk.astype(BF16), wuv=wuv.astype(BF16),
        wuvt=jnp.transpose(wuv, (0, 2, 1)).astype(BF16),
        w4=w4.astype(BF16), pe2=pe2,
        lru_conv_w=W['lru_conv_w'][l], lru_conv_b=r1(W['lru_conv_b'][l]),
        lru_wa_bd=_block_diag(W['lru_wa'][l]).astype(BF16), lru_ba=r1(W['lru_ba'][l]),
        lru_wx_bd=_block_diag(W['lru_wx'][l]).astype(BF16), lru_bx=r1(W['lru_bx'][l]),
        lru_lam=r1(W['lru_lam'][l]),
        conf_dw_w=W['conf_dw_w'][l], conf_dw_b=r1(W['conf_dw_b'][l]),
        conf_ln_g=r1(W['conf_ln_g'][l]), conf_ln_b=r1(W['conf_ln_b'][l]),
        w_out=W['w_out'][l].astype(BF16), w_ff1=W['w_ff1'][l].astype(BF16), w_ff2=W['w_ff2'][l].astype(BF16),
    )


def _rope_tables(pos):
    half = MLA_ROPE // 2
    freq = jnp.power(ROPE_THETA, -jnp.arange(half, dtype=F32) / half)
    ang = pos.astype(F32)[:, None] * freq
    cosv = jnp.tile(jnp.cos(ang), (1, 2 * MLA_HEADS))
    sinv = jnp.tile(jnp.sin(ang), (1, 2 * MLA_HEADS))
    return cosv, sinv


def _overlap_t(nblk_pad, ncp, nc):
    n = np.arange(ncp)[None, :]
    j = np.arange(nblk_pad)[:, None]
    ov = (n * NSA_CMP_STRIDE < j * NSA_SEL_BLOCK + NSA_SEL_BLOCK) & (n * NSA_CMP_STRIDE + NSA_CMP_BLOCK > j * NSA_SEL_BLOCK)
    return jnp.asarray((ov & (n < nc)).astype(np.float32), dtype=BF16)


def _key_tiles(a, nb, t, front=0):
    k = a.reshape(nb, -1, t, a.shape[-1])
    return jnp.pad(k, ((0, 0), (front, 0), (0, 0), (0, 0)))


def _value_tiles_t(a, nb, t, front=0):
    v = jnp.swapaxes(_key_tiles(a, nb, t, front), 2, 3)
    ones = jnp.ones(v.shape[:2] + (1, t), v.dtype)
    return jnp.concatenate([v, ones, jnp.zeros(v.shape[:2] + (VSUM_ROWS - 1, t), v.dtype)], axis=2)


def kernel(x_prompt, x_sample, c_prompt, c_sample, state_lru_h, state_lru_conv, state_conv, cache_mla, cache_nsa,
           cache_nsa_win, page_table, w_mod, b_mod, g_pre_mix, g_post_mix, g_pre_ffn, g_post_ffn, w_in, lru_conv_w,
           lru_conv_b, lru_wa, lru_ba, lru_wx, lru_bx, lru_lam, conf_dw_w, conf_dw_b, conf_ln_g, conf_ln_b, mla_gq,
           mla_wuq, mla_gkv, mla_wuk, mla_wuv, nsa_pe_k, nsa_wc_k, nsa_pe_v, nsa_wc_v, w_out, w_ff1, w_ff2):
    W = dict(w_in=w_in, g_pre_mix=g_pre_mix, g_post_mix=g_post_mix, g_pre_ffn=g_pre_ffn, g_post_ffn=g_post_ffn,
             lru_conv_w=lru_conv_w, lru_conv_b=lru_conv_b, lru_wa=lru_wa, lru_ba=lru_ba, lru_wx=lru_wx, lru_bx=lru_bx,
             lru_lam=lru_lam, conf_dw_w=conf_dw_w, conf_dw_b=conf_dw_b, conf_ln_g=conf_ln_g, conf_ln_b=conf_ln_b,
             mla_gq=mla_gq, mla_wuq=mla_wuq, mla_gkv=mla_gkv, mla_wuk=mla_wuk, mla_wuv=mla_wuv,
             nsa_pe_k=nsa_pe_k, nsa_wc_k=nsa_wc_k, nsa_pe_v=nsa_pe_v, nsa_wc_v=nsa_wc_v,
             w_out=w_out, w_ff1=w_ff1, w_ff2=w_ff2)
    L = w_in.shape[0]
    nbp, S, D = x_prompt.shape
    nbs, T, _ = x_sample.shape
    n_pages = page_table.shape[1]
    page = cache_mla.shape[2]
    past = n_pages * page
    Np, Ns = nbp * S, nbs * T
    tq = min(Q_TILE, S)
    tk = min(K_TILE, S)
    tm_p = min(256, Np)
    tm_s = min(256, Ns)
    tc = min(512, S)

    nc_p = S // NSA_CMP_STRIDE - 1
    nblk_p = -(-S // NSA_SEL_BLOCK)
    ov_p = _overlap_t(nblk_p, S // NSA_CMP_STRIDE, nc_p)
    assert NSA_WINDOW % tq == 0 and S % tk == 0 and tk % tq == 0
    blk_of_pos = jnp.arange(S, dtype=jnp.int32)[:, None] // NSA_SEL_BLOCK
    blk_onehot_p = jnp.tile((blk_of_pos == jnp.arange(nblk_p, dtype=jnp.int32)[None, :]).astype(BF16), (nbp, 1))
    len_s = past + T
    assert len_s // NSA_CMP_STRIDE == past // NSA_CMP_STRIDE and past % NSA_SEL_BLOCK == 0 and T <= NSA_SEL_BLOCK
    nc_s = len_s // NSA_CMP_STRIDE - 1
    nblk_s = -(-len_s // NSA_SEL_BLOCK)
    nblk_s_pad = -(-nblk_s // 128) * 128
    ov_s = _overlap_t(nblk_s_pad, past // NSA_CMP_STRIDE, nc_s)

    rows = nbp + nbs
    rows_pad = -(-rows // 8) * 8
    c_all = jnp.concatenate([c_prompt, c_sample, jnp.zeros((rows_pad - rows, D), F32)], axis=0)
    mod = _modulation(c_all, w_mod.astype(BF16), b_mod)

    params = [_layer_params(l, W) for l in range(L)]
    pe_all = _pe_term(jnp.stack([p['pe2'] for p in params]), jnp.stack([p['w4'] for p in params]))

    cos_p, sin_p = _rope_tables(jnp.arange(S, dtype=jnp.int32))
    cos_s, sin_s = _rope_tables(past + jnp.arange(T, dtype=jnp.int32))
    cos_s = jnp.tile(cos_s, (nbs, 1))
    sin_s = jnp.tile(sin_s, (nbs, 1))
    pt_flat = page_table.reshape(-1).astype(jnp.int32)
    cache_mla_t = jnp.swapaxes(cache_mla, 2, 3)
    cache_nsa_t = jnp.transpose(cache_nsa, (0, 1, 3, 4, 2)).reshape(L, cache_nsa.shape[1], 4 * NSA_HD, page)
    win_cache_t = jnp.transpose(cache_nsa_win, (0, 1, 3, 4, 2)).reshape(L, nbs, 2 * NSA_HD, cache_nsa_win.shape[2])

    xp = x_prompt.reshape(Np, D)
    xs = x_sample.reshape(Ns, D)
    outs_p, outs_s = [], []
    for l in range(L):
        P = params[l]
        pe = pe_all[l]
        mp = [m.reshape(nbp, 1, D) for m in jnp.split(mod[l, :nbp], 6, axis=-1)]
        ms = [jnp.repeat(m, T, axis=0) for m in jnp.split(mod[l, nbp:rows], 6, axis=-1)]

        lru, glu, qm, row, nq, nsa4, win, gates = _in_proj(
            xp, mp[0], mp[1], P['g_pre_mix'], P['w_in'], P['mla_gq'], P['wuq'], P['wuk'], P['mla_gkv'],
            cos_p, sin_p, per_token=False, tm=tm_p)
        a_out, b_out, h_last = _seq_prompt(lru.reshape(nbp, S, 512), glu.reshape(nbp, S, 256), P, tc=tc)
        rowb = row.astype(BF16)
        k3 = _key_tiles(rowb, nbp, tk)
        vt3 = _value_tiles_t(rowb[:, :MLA_KV_LORA], nbp, tk)
        Pp = _cmp_prompt(nsa4, P['w4'], nbp)
        nsab = nsa4.astype(BF16)
        ks3 = _key_tiles(jnp.concatenate([blk_onehot_p, nsab[:, 128:192]], axis=1), nbp, tk)
        vst3 = _value_tiles_t(nsab[:, 192:256], nbp, tk)
        winb = win.astype(BF16)
        kw3 = _key_tiles(winb[:, 0:64], nbp, tq, front=NSA_WINDOW // tq)
        vwt3 = _value_tiles_t(winb[:, 64:128], nbp, tq, front=NSA_WINDOW // tq)
        c_out, d_out = _attn_prompt(qm, k3, vt3, P['wuvt'], nq, gates, Pp, pe, ov_p, ks3, vst3, kw3, vwt3, tq=tq,
                                    nc=nc_p, n_sel=min(NSA_N_SEL, nblk_p))
        xp = _post(xp, a_out.reshape(Np, 256), b_out.reshape(Np, 256), c_out, d_out, mp[2], mp[3], mp[4], mp[5],
                   P['w_out'], P['g_post_mix'], P['g_pre_ffn'], P['g_post_ffn'], P['w_ff1'], P['w_ff2'],
                   per_token=False, tm=tm_p)
        n_win = min(NSA_WINDOW, S)
        outs_p.append((h_last[:, 0], lru.reshape(nbp, S, 512)[:, S - (LRU_CONV - 1):, :D_LRU],
                       glu.reshape(nbp, S, 256)[:, S - (CONF_KERNEL - 1):],
                       row.reshape(nbp, S, MLA_CACHE_W), nsa4.reshape(nbp, S, 4, NSA_HD),
                       win.reshape(nbp, S, 2, NSA_HD)[:, S - n_win:]))

        lru, glu, qm, row, nq, nsa4, win, gates = _in_proj(
            xs, ms[0], ms[1], P['g_pre_mix'], P['w_in'], P['mla_gq'], P['wuq'], P['wuk'], P['mla_gkv'],
            cos_s, sin_s, per_token=True, tm=tm_s)
        tmaj = lambda a: jnp.swapaxes(a, 0, 1)
        a_t, b_t, h_last, ncs, ngs = _seq_sample(
            tmaj(lru.reshape(nbs, T, 512)), tmaj(glu.reshape(nbs, T, 256)), tmaj(state_lru_conv[l]),
            tmaj(state_conv[l]), state_lru_h[l], P)
        c_out = _mla_sample(cache_mla_t, pt_flat, qm, row, P['wuv'], l, nbs, n_pages)
        d_out = _nsa_sample(cache_nsa_t, pt_flat, nq, gates, P['w4'], pe, ov_s, nsa4, win_cache_t, win, l, nbs, n_pages,
                            nc=nc_s, n_sel=min(NSA_N_SEL, nblk_s), past=past)
        xs = _post(xs, tmaj(a_t).reshape(Ns, 256), tmaj(b_t).reshape(Ns, 256), c_out, d_out, ms[2], ms[3], ms[4],
                   ms[5], P['w_out'], P['g_post_mix'], P['g_pre_ffn'], P['g_post_ffn'], P['w_ff1'], P['w_ff2'],
                   per_token=True, tm=tm_s)
        wkv = jnp.concatenate([cache_nsa_win[l], win.reshape(nbs, T, 2, NSA_HD)], axis=1)
        n_win = min(NSA_WINDOW, past + T)
        outs_s.append((h_last, tmaj(ncs), tmaj(ngs), row.reshape(nbs, T, MLA_CACHE_W),
                       nsa4.reshape(nbs, T, 4, NSA_HD), wkv[:, -n_win:]))

    stack = lambda outs, i: jnp.stack([o[i] for o in outs])
    return (xp.reshape(nbp, S, D), xs.reshape(nbs, T, D),
            stack(outs_p, 0), stack(outs_s, 0), stack(outs_p, 1), stack(outs_s, 1), stack(outs_p, 2), stack(outs_s, 2),
            stack(outs_p, 3), stack(outs_s, 3), stack(outs_p, 4), stack(outs_s, 4), stack(outs_p, 5), stack(outs_s, 5))
```

```python
import functools

import numpy as np
import jax
import jax.numpy as jnp
from jax import lax
from jax.experimental import pallas as pl
from jax.experimental.pallas import tpu as pltpu

F32 = jnp.float32
BF16 = jnp.bfloat16

D_MODEL = 1024
D_LRU = 256
LRU_BLOCKS = 4
LRU_CONV = 4
LRU_C = 8.0
D_CONF = 256
CONF_KERNEL = 31
MLA_HEADS = 4
MLA_NOPE = 64
MLA_ROPE = 32
MLA_V = 64
MLA_Q_LORA = 256
MLA_KV_LORA = 128
MLA_CACHE_W = MLA_KV_LORA + MLA_ROPE
ROPE_THETA = 10000.0
NSA_HEADS = 4
NSA_HD = 64
NSA_CMP_BLOCK = 32
NSA_CMP_STRIDE = 16
NSA_SEL_BLOCK = 64
NSA_N_SEL = 16
NSA_WINDOW = 512
NSA_FORCE = 1.0e4
D_FF = 4 * D_MODEL
EPS = 1e-6
IN_SPLITS = (D_LRU, D_LRU, 2 * D_CONF, MLA_Q_LORA, MLA_KV_LORA, MLA_ROPE, NSA_HEADS * NSA_HD, 6 * NSA_HD, 3 * NSA_HEADS)

C_LRU = 0
C_CONF = 512
C_CQ = 1024
C_CKV = 1280
C_NQ = 1408
C_NKV = 1664
C_MISC = 2048
D_INP = 2176
MISC_NG = 64

NEG = -1e30
LOG2E = 1.4426950408889634
VSUM_ROWS = 8
VMEM_LIMIT_V7X = 56 * 1024 * 1024
Q_TILE = 128
K_TILE = 512
SAMPLE_R = 128


def _cparams(sem):
    return pltpu.CompilerParams(dimension_semantics=sem, vmem_limit_bytes=VMEM_LIMIT_V7X)


def _rms(x, g):
    return x * lax.rsqrt(jnp.mean(x * x, axis=-1, keepdims=True) + EPS) * g


def _bdot(a, b):
    return jnp.dot(a.astype(BF16), b.astype(BF16), preferred_element_type=F32)


def _dot_nt(a, b):
    return lax.dot_general(a.astype(BF16), b.astype(BF16), (((1,), (1,)), ((), ())), preferred_element_type=F32)


def _dot_tn(a, b):
    return lax.dot_general(a.astype(BF16), b.astype(BF16), (((0,), (0,)), ((), ())), preferred_element_type=F32)


def _mod_kernel(c_ref, w_ref, b_ref, o_ref):
    o_ref[...] = _bdot(jax.nn.silu(c_ref[...]), w_ref[...]) + b_ref[...]


def _modulation(c_all, w_mod, b_mod):
    L, D, D6 = w_mod.shape
    rows = c_all.shape[0]
    tn = 1536
    return pl.pallas_call(
        _mod_kernel,
        grid=(L, D6 // tn),
        in_specs=[pl.BlockSpec((rows, D), lambda l, j: (0, 0)),
                  pl.BlockSpec((None, D, tn), lambda l, j: (l, 0, j)),
                  pl.BlockSpec((None, 1, tn), lambda l, j: (l, 0, j))],
        out_specs=pl.BlockSpec((None, rows, tn), lambda l, j: (l, 0, j)),
        out_shape=jax.ShapeDtypeStruct((L, rows, D6), F32),
        compiler_params=_cparams(("arbitrary", "arbitrary")),
        name="modulation",
    )(c_all, w_mod, b_mod.reshape(L, 1, D6))


def _pe_kernel(pe_ref, w_ref, o_ref):
    acc = jnp.zeros((8, 256), F32)
    for r in range(NSA_CMP_STRIDE):
        acc = acc + _bdot(pe_ref[r], w_ref[r])
    o_ref[...] = jnp.broadcast_to(acc[0:1, 0:128] + acc[1:2, 128:256], (8, 128))


def _pe_term(pe2, w4):
    L = pe2.shape[0]
    return pl.pallas_call(
        _pe_kernel,
        grid=(L,),
        in_specs=[pl.BlockSpec((None, NSA_CMP_STRIDE, 8, 128), lambda l: (l, 0, 0, 0)),
                  pl.BlockSpec((None, NSA_CMP_STRIDE, 128, 256), lambda l: (l, 0, 0, 0))],
        out_specs=pl.BlockSpec((None, 8, 128), lambda l: (l, 0, 0)),
        out_shape=jax.ShapeDtypeStruct((L, 8, 128), F32),
        compiler_params=_cparams(("arbitrary",)),
        name="pe_term",
    )(pe2, w4)


def _in_proj_kernel(x_ref, sh_ref, sc_ref, g_ref, w_ref, gq_ref, wuq_ref, wuk_ref, gkv_ref, cos_ref, sin_ref,
                    lru_ref, glu_ref, qm_ref, row_ref, nq_ref, nsa4_ref, win_ref, gate_ref):
    h = _rms(x_ref[...], g_ref[...]) * (1.0 + sc_ref[...]) + sh_ref[...]
    u = _bdot(h, w_ref[...])
    lru_ref[...] = u[:, C_LRU:C_LRU + 512]
    glu_ref[...] = u[:, C_CONF:C_CONF + 256] * jax.nn.sigmoid(u[:, C_CONF + 256:C_CONF + 512])
    nsa4_ref[...] = u[:, C_NKV:C_NKV + 256]
    win_ref[...] = u[:, C_NKV + 256:C_NKV + 384]
    misc = u[:, C_MISC:C_MISC + 128]
    gate_ref[...] = jax.nn.sigmoid(misc)
    cosv = cos_ref[...]
    sinv = sin_ref[...]
    qr = _bdot(_rms(u[:, C_CQ:C_CQ + 256], gq_ref[...]), wuq_ref[...])
    q_rope = qr[:, 256:384] * cosv + qr[:, 384:512] * sinv
    q_lat = _bdot(qr[:, 0:256], wuk_ref[...])
    scale = (MLA_NOPE + MLA_ROPE) ** -0.5 * LOG2E
    for hd in range(MLA_HEADS):
        qm_ref[hd, :, 0:128] = q_lat[:, 128 * hd:128 * (hd + 1)] * scale
        qm_ref[hd, :, 128:160] = q_rope[:, 32 * hd:32 * (hd + 1)] * scale
    row_ref[:, 0:128] = _rms(u[:, C_CKV:C_CKV + 128], gkv_ref[...])
    row_ref[:, 128:160] = misc[:, 0:32] * cosv[:, 0:32] + misc[:, 32:64] * sinv[:, 0:32]
    nq = u[:, C_NQ:C_NQ + 256] * (NSA_HD ** -0.5 * LOG2E)
    for hd in range(NSA_HEADS):
        nq_ref[hd] = nq[:, 64 * hd:64 * (hd + 1)]


def _in_proj(x, sh, sc, g, w_in, gq, wuq, wuk, gkv, cosv, sinv, *, per_token, tm):
    N, D = x.shape
    nt = N // tm
    if per_token:
        mod_spec = pl.BlockSpec((tm, D), lambda i: (i, 0))
        tab_spec = pl.BlockSpec((tm, 128), lambda i: (i, 0))
    else:
        per_b = nt // sh.shape[0]
        mod_spec = pl.BlockSpec((None, 1, D), lambda i: (i // per_b, 0, 0))
        tab_spec = pl.BlockSpec((tm, 128), lambda i: (i % per_b, 0))
    const = lambda shape: pl.BlockSpec(shape, lambda i: tuple(0 for _ in shape))
    row = lambda w: pl.BlockSpec((tm, w), lambda i: (i, 0))
    head = lambda w: pl.BlockSpec((4, tm, w), lambda i: (0, i, 0))
    return pl.pallas_call(
        _in_proj_kernel,
        grid=(nt,),
        in_specs=[row(D), mod_spec, mod_spec, const((1, D)), const((D, D_INP)), const((1, 256)),
                  const((256, 512)), const((256, 512)), const((1, 128)), tab_spec, tab_spec],
        out_specs=[row(512), row(256), head(MLA_CACHE_W), row(MLA_CACHE_W), head(NSA_HD), row(256), row(128), row(128)],
        out_shape=[jax.ShapeDtypeStruct((N, 512), F32), jax.ShapeDtypeStruct((N, 256), F32),
                   jax.ShapeDtypeStruct((4, N, MLA_CACHE_W), F32), jax.ShapeDtypeStruct((N, MLA_CACHE_W), F32),
                   jax.ShapeDtypeStruct((4, N, NSA_HD), F32), jax.ShapeDtypeStruct((N, 256), F32),
                   jax.ShapeDtypeStruct((N, 128), F32), jax.ShapeDtypeStruct((N, 128), F32)],
        compiler_params=_cparams(("parallel",)),
        name="in_proj",
    )(x, sh, sc, g, w_in, gq, wuq, wuk, gkv, cosv, sinv)


def _post_kernel(x_ref, a_ref, b_ref, c_ref, d_ref, gt1_ref, sh2_ref, sc2_ref, gt2_ref,
                 wo_ref, gpm_ref, gpf_ref, gqf_ref, w1_ref, w2_ref, o_ref):
    y = _bdot(a_ref[...], wo_ref[0:256, :])
    y = y + _bdot(b_ref[...], wo_ref[256:512, :])
    y = y + _bdot(c_ref[...], wo_ref[512:768, :])
    y = y + _bdot(d_ref[...], wo_ref[768:1024, :])
    x = x_ref[...] + gt1_ref[...] * _rms(y, gpm_ref[...])
    h = (_rms(x, gpf_ref[...]) * (1.0 + sc2_ref[...]) + sh2_ref[...]).astype(BF16)
    f = jnp.zeros(x.shape, F32)
    fc = 1024
    for j in range(D_FF // fc):
        t = jnp.maximum(jnp.dot(h, w1_ref[:, j * fc:(j + 1) * fc], preferred_element_type=F32), 0.0)
        f = f + _bdot(t * t, w2_ref[j * fc:(j + 1) * fc, :])
    o_ref[...] = x + gt2_ref[...] * _rms(f, gqf_ref[...])


def _post(x, a, b, c, d, gt1, sh2, sc2, gt2, wo, gpm, gpf, gqf, w1, w2, *, per_token, tm):
    N, D = x.shape
    nt = N // tm
    if per_token:
        mod_spec = pl.BlockSpec((tm, D), lambda i: (i, 0))
    else:
        per_b = nt // gt1.shape[0]
        mod_spec = pl.BlockSpec((None, 1, D), lambda i: (i // per_b, 0, 0))
    const = lambda shape: pl.BlockSpec(shape, lambda i: tuple(0 for _ in shape))
    row = lambda w: pl.BlockSpec((tm, w), lambda i: (i, 0))
    return pl.pallas_call(
        _post_kernel,
        grid=(nt,),
        in_specs=[row(D), row(256), row(256), row(256), row(256), mod_spec, mod_spec, mod_spec, mod_spec,
                  const((D, D)), const((1, D)), const((1, D)), const((1, D)), const((D, D_FF)), const((D_FF, D))],
        out_specs=row(D),
        out_shape=jax.ShapeDtypeStruct((N, D), F32),
        compiler_params=_cparams(("parallel",)),
        name="post",
    )(x, a, b, c, d, gt1, sh2, sc2, gt2, wo, gpm, gpf, gqf, w1, w2)


def _lru_gates(y, gate_w, lam):
    wa, ba, wx, bx = gate_w
    r = jax.nn.sigmoid(_bdot(y, wa) + ba)
    i = jax.nn.sigmoid(_bdot(y, wx) + bx)
    log_a = -LRU_C * r * jax.nn.softplus(-lam)
    return jnp.exp(log_a), jnp.sqrt(1.0 - jnp.exp(2.0 * log_a)) * i * y


def _layernorm(y, g, b):
    mu = jnp.mean(y, axis=-1, keepdims=True)
    var = jnp.mean(jnp.square(y - mu), axis=-1, keepdims=True)
    return (y - mu) * lax.rsqrt(var + EPS) * g + b


def _seq_prompt_kernel(xg_ref, glu_ref, cw_ref, cb_ref, wa_ref, ba_ref, wx_ref, bx_ref, lam_ref,
                       dw_ref, db_ref, lng_ref, lnb_ref,
                       a_ref, b_ref, hl_ref, xbuf, gbuf, abuf, dbuf, hbuf):
    i = pl.program_id(0)
    nb, tc, _ = glu_ref.shape

    @pl.when(i == 0)
    def _():
        xbuf[:, 0:8, :] = jnp.zeros((nb, 8, D_LRU), F32)
        gbuf[:, 0:32, :] = jnp.zeros((nb, 32, D_CONF), F32)
        hbuf[...] = jnp.zeros(hbuf.shape, F32)

    xbuf[:, 8:8 + tc, :] = xg_ref[:, :, 0:D_LRU]
    y = jnp.zeros((nb, tc, D_LRU), F32) + cb_ref[...]
    for k in range(LRU_CONV):
        y = y + xbuf[:, 8 - (LRU_CONV - 1) + k:8 - (LRU_CONV - 1) + k + tc, :] * cw_ref[k:k + 1, :]
    a, drive = _lru_gates(y.reshape(nb * tc, D_LRU),
                          (wa_ref[...], ba_ref[...], wx_ref[...], bx_ref[...]), lam_ref[...])
    abuf[...] = a.reshape(nb, tc, D_LRU)
    dbuf[...] = drive.reshape(nb, tc, D_LRU)

    def step(t, h):
        h = abuf[:, pl.ds(t, 1), :] * h + dbuf[:, pl.ds(t, 1), :]
        dbuf[:, pl.ds(t, 1), :] = h
        return h

    h = lax.fori_loop(0, tc, step, hbuf[...], unroll=8)
    hbuf[...] = h
    hl_ref[...] = jnp.broadcast_to(h, hl_ref.shape)
    a_ref[...] = dbuf[...] * jax.nn.gelu(xg_ref[:, :, D_LRU:2 * D_LRU])
    xbuf[:, 0:8, :] = xbuf[:, tc:tc + 8, :]

    gbuf[:, 32:32 + tc, :] = glu_ref[...]
    z = jnp.zeros((nb, tc, D_CONF), F32) + db_ref[...]
    for k in range(CONF_KERNEL):
        o = 32 - (CONF_KERNEL - 1) + k
        z = z + gbuf[:, o:o + tc, :] * dw_ref[k:k + 1, :]
    b_ref[...] = jax.nn.silu(_layernorm(z, lng_ref[...], lnb_ref[...]))
    gbuf[:, 0:32, :] = gbuf[:, tc:tc + 32, :]


def _seq_prompt(xg, glu, P, *, tc):
    nb, S, _ = glu.shape
    const = lambda shape: pl.BlockSpec(shape, lambda i: tuple(0 for _ in shape))
    seq = lambda w: pl.BlockSpec((nb, tc, w), lambda i: (0, i, 0))
    return pl.pallas_call(
        _seq_prompt_kernel,
        grid=(S // tc,),
        in_specs=[seq(512), seq(256), const((LRU_CONV, 256)), const((1, 256)), const((256, 256)), const((1, 256)),
                  const((256, 256)), const((1, 256)), const((1, 256)), const((CONF_KERNEL, 256)), const((1, 256)),
                  const((1, 256)), const((1, 256))],
        out_specs=[seq(256), seq(256), const((nb, 8, 256))],
        out_shape=[jax.ShapeDtypeStruct((nb, S, 256), F32), jax.ShapeDtypeStruct((nb, S, 256), F32),
                   jax.ShapeDtypeStruct((nb, 8, 256), F32)],
        scratch_shapes=[pltpu.VMEM((nb, tc + 8, 256), F32), pltpu.VMEM((nb, tc + 32, 256), F32),
                        pltpu.VMEM((nb, tc, 256), F32), pltpu.VMEM((nb, tc, 256), F32),
                        pltpu.VMEM((nb, 1, 256), F32)],
        compiler_params=_cparams(("arbitrary",)),
        name="seq_prompt",
    )(xg, glu, P['lru_conv_w'], P['lru_conv_b'], P['lru_wa_bd'], P['lru_ba'], P['lru_wx_bd'], P['lru_bx'],
      P['lru_lam'], P['conf_dw_w'], P['conf_dw_b'], P['conf_ln_g'], P['conf_ln_b'])


def _seq_sample_kernel(xg_ref, glu_ref, cs_ref, gs_ref, h0_ref, cw_ref, cb_ref, wa_ref, ba_ref, wx_ref, bx_ref,
                       lam_ref, dw_ref, db_ref, lng_ref, lnb_ref,
                       a_ref, b_ref, hl_ref, ncs_ref, ngs_ref, xbuf, gbuf):
    T, nb, _ = glu_ref.shape
    nc, ng = LRU_CONV - 1, CONF_KERNEL - 1
    xbuf[0:nc] = cs_ref[...]
    xbuf[nc:nc + T] = xg_ref[:, :, 0:D_LRU]
    y = jnp.zeros((T, nb, D_LRU), F32) + cb_ref[...]
    for k in range(LRU_CONV):
        y = y + xbuf[k:k + T] * cw_ref[k:k + 1, :]
    a, drive = _lru_gates(y.reshape(T * nb, D_LRU),
                          (wa_ref[...], ba_ref[...], wx_ref[...], bx_ref[...]), lam_ref[...])
    a = a.reshape(T, nb, D_LRU)
    drive = drive.reshape(T, nb, D_LRU)
    h = h0_ref[...]
    for t in range(T):
        h = a[t] * h + drive[t]
        a_ref[t] = h * jax.nn.gelu(xg_ref[t, :, D_LRU:2 * D_LRU])
    hl_ref[...] = h
    ncs_ref[...] = xbuf[T:T + nc]

    gbuf[0:ng] = gs_ref[...]
    gbuf[ng:ng + T] = glu_ref[...]
    z = jnp.zeros((T, nb, D_CONF), F32) + db_ref[...]
    for k in range(CONF_KERNEL):
        z = z + gbuf[k:k + T] * dw_ref[k:k + 1, :]
    b_ref[...] = jax.nn.silu(_layernorm(z, lng_ref[...], lnb_ref[...]))
    ngs_ref[...] = gbuf[T:T + ng]


def _seq_sample(xg_t, glu_t, cs_t, gs_t, h0, P):
    T, nb, _ = glu_t.shape
    nc, ng = LRU_CONV - 1, CONF_KERNEL - 1
    full = lambda a: pl.BlockSpec(a.shape, lambda i: tuple(0 for _ in a.shape))
    args = (xg_t, glu_t, cs_t, gs_t, h0, P['lru_conv_w'], P['lru_conv_b'], P['lru_wa_bd'], P['lru_ba'],
            P['lru_wx_bd'], P['lru_bx'], P['lru_lam'], P['conf_dw_w'], P['conf_dw_b'], P['conf_ln_g'], P['conf_ln_b'])
    outs = [jax.ShapeDtypeStruct((T, nb, 256), F32), jax.ShapeDtypeStruct((T, nb, 256), F32),
            jax.ShapeDtypeStruct((nb, 256), F32), jax.ShapeDtypeStruct((nc, nb, 256), F32),
            jax.ShapeDtypeStruct((ng, nb, 256), F32)]
    return pl.pallas_call(
        _seq_sample_kernel,
        grid=(1,),
        in_specs=[full(a) for a in args],
        out_specs=[full(o) for o in outs],
        out_shape=outs,
        scratch_shapes=[pltpu.VMEM((T + nc, nb, 256), F32), pltpu.VMEM((T + ng, nb, 256), F32)],
        compiler_params=_cparams(("arbitrary",)),
        name="seq_sample",
    )(*args)


def _softmax_stages(vt_ref, m_ref, acc_ref):
    def soft(s_ref, p_ref, a_ref):
        s = s_ref[...]
        m_prev = m_ref[...]
        m_new = jnp.maximum(m_prev, jnp.max(s, axis=0, keepdims=True))
        p_ref[...] = jnp.exp2(s - m_new).astype(BF16)
        a_ref[...] = jnp.exp2(m_prev - m_new)
        m_ref[...] = m_new

    def pv(p_ref, a_ref, tile):
        acc_ref[...] = a_ref[...] * acc_ref[...] + jnp.dot(vt_ref[tile], p_ref[...], preferred_element_type=F32)

    return soft, pv


def _softmax_pv(s, v_t):
    p = jnp.exp2(s - jnp.max(s, axis=0, keepdims=True))
    acc = jnp.dot(v_t, p.astype(BF16), preferred_element_type=F32)
    dv = acc.shape[0] - VSUM_ROWS
    return acc[0:dv] / jnp.maximum(acc[dv:dv + 1], 1e-30)


def _init_stats(m_ref, acc_ref, l_ref=None):
    m_ref[...] = jnp.full(m_ref.shape, NEG, F32)
    acc_ref[...] = jnp.zeros(acc_ref.shape, F32)
    if l_ref is not None:
        l_ref[...] = jnp.zeros(l_ref.shape, F32)


def _finish_t(acc_ref):
    dv = acc_ref.shape[0] - VSUM_ROWS
    return acc_ref[0:dv, :] / jnp.maximum(acc_ref[dv:dv + 1, :], 1e-30)


def _finish(l_ref, acc_ref):
    return acc_ref[...] / jnp.maximum(l_ref[...], 1e-30)


def _compressed_kv(p_ref, pe_ref, nc):
    ncp = p_ref.shape[0]
    top = p_ref[:, 0:128]
    bot = pltpu.roll(p_ref[:, 128:256], ncp - 1, 0)
    n = lax.broadcasted_iota(jnp.int32, (ncp, 128), 0)
    return jnp.where(n < nc, top + bot + pe_ref[0:1, :], 0.0)


def _compressed_branch(q, kcv, q_pos, nc):
    ncp = kcv.shape[0]
    s = _dot_nt(kcv[:, 0:64], q)
    n = lax.broadcasted_iota(jnp.int32, s.shape, 0)
    mask = (n * NSA_CMP_STRIDE + (NSA_CMP_BLOCK - 1) <= q_pos) & (n < nc)
    s = jnp.where(mask, s, NEG)
    e = jnp.where(mask, jnp.exp2(s - jnp.max(s, axis=0, keepdims=True)), 0.0)
    p = e / jnp.maximum(jnp.sum(e, axis=0, keepdims=True), 1e-30)
    o = _dot_tn(kcv[:, 64:128], p)
    return o, p


def _select_blocks(imp, q_pos, n_sel):
    blk = lax.broadcasted_iota(jnp.int32, imp.shape, 0)
    cur = q_pos // NSA_SEL_BLOCK
    valid = blk * NSA_SEL_BLOCK <= q_pos
    forced = (blk == 0) | (blk == cur) | (blk == cur - 1)
    score = jnp.where(valid, imp + jnp.where(forced, NSA_FORCE, 0.0), -1.0)
    big = jnp.int32(2 ** 30)
    sel = jnp.zeros(imp.shape, jnp.bool_)
    for _ in range(n_sel):
        mx = jnp.max(score, axis=0, keepdims=True)
        first = jnp.min(jnp.where(score == mx, blk, big), axis=0, keepdims=True)
        hit = blk == first
        sel = sel | hit
        score = jnp.where(hit, -3e38, score)
    return jnp.where(sel, 0.0, NEG)


def _cmp_prompt_kernel(x_ref, w_ref, p_ref):
    n_chunk = p_ref.shape[0]
    acc = jnp.zeros(p_ref.shape, F32)
    for r in range(NSA_CMP_STRIDE):
        acc = acc + _bdot(x_ref[pl.ds(r, n_chunk, stride=NSA_CMP_STRIDE), :], w_ref[r])
    p_ref[...] = acc


def _cmp_prompt(nsa4, w4, nb):
    N = nsa4.shape[0]
    S = N // nb
    n_chunk = S // NSA_CMP_STRIDE
    return pl.pallas_call(
        _cmp_prompt_kernel,
        grid=(nb,),
        in_specs=[pl.BlockSpec((S, 128), lambda b: (b, 0)),
                  pl.BlockSpec((NSA_CMP_STRIDE, 128, 256), lambda b: (0, 0, 0))],
        out_specs=pl.BlockSpec((None, n_chunk, 256), lambda b: (b, 0, 0)),
        out_shape=jax.ShapeDtypeStruct((nb, n_chunk, 256), F32),
        compiler_params=_cparams(("parallel",)),
        name="cmp_prompt",
    )(nsa4, w4)


def _attn_prompt_kernel(qm_ref, km_ref, vmt_ref, wuvt_ref, q_ref, g_ref, p_ref, pe_ref, ov_ref, ks_ref, vst_ref,
                        kw_ref, vwt_ref, c_ref, o_ref,
                        m_ref, acc_ref, mm_ref, accm_ref, ow_ref, *stage, nc, n_sel):
    stage_refs = [stage[3 * n:3 * n + 3] for n in range(4)]
    i = pl.program_id(1)
    _, tq, _ = q_ref.shape
    tk = ks_ref.shape[1]
    tw = kw_ref.shape[1]
    R = NSA_HEADS * tq
    q_f32 = q_ref[...].reshape(R, NSA_HD)
    q = q_f32.astype(BF16)
    qm = qm_ref[...].reshape(R, MLA_CACHE_W).astype(BF16)
    q_pos_q = i * tq + lax.broadcasted_iota(jnp.int32, (1, tq), 1)
    q_pos = jnp.concatenate([q_pos_q] * NSA_HEADS, axis=1)

    n_wt = NSA_WINDOW // tw
    kw = jnp.concatenate([kw_ref[i + c] for c in range(n_wt + 1)], axis=0)
    vw_t = jnp.concatenate([vwt_ref[i + c] for c in range(n_wt + 1)], axis=1)
    k_pos = (i - n_wt) * tw + lax.broadcasted_iota(jnp.int32, (kw.shape[0], R), 0)
    ok = (k_pos >= 0) & (k_pos <= q_pos) & (k_pos > q_pos - NSA_WINDOW)
    ow_ref[...] = _softmax_pv(jnp.where(ok, _dot_nt(kw, q), NEG), vw_t)

    kcv = _compressed_kv(p_ref, pe_ref, nc)
    o_c, p_c = _compressed_branch(q, kcv, q_pos, nc)
    imp4 = jnp.dot(ov_ref[...], p_c.astype(BF16), preferred_element_type=F32)
    imp = imp4[:, 0:tq]
    for hd in range(1, NSA_HEADS):
        imp = imp + imp4[:, hd * tq:(hd + 1) * tq]
    bias = _select_blocks(imp, q_pos_q, n_sel).T
    q_sel = jnp.concatenate([jnp.concatenate([bias] * NSA_HEADS, axis=0), q_f32], axis=1).astype(BF16)

    n_full = (i * tq) // tk
    _init_stats(m_ref, acc_ref)
    _init_stats(mm_ref, accm_ref)

    def score_m(j):
        return _dot_nt(km_ref[j], qm)

    def score_s(j):
        return _dot_nt(ks_ref[j], q_sel)

    def full_tile(j):
        return jnp.clip(j, 0, jnp.maximum(n_full - 1, 0))

    soft_m, pv_m = _softmax_stages(vmt_ref, mm_ref, accm_ref)
    soft_s, pv_s = _softmax_stages(vst_ref, m_ref, acc_ref)
    (sma, pma, ama), (smb, pmb, amb), (ssa, psa, asa), (ssb, psb, asb) = stage_refs

    causal = n_full * tk + lax.broadcasted_iota(jnp.int32, (tk, R), 0) <= q_pos
    sma[...] = jnp.where(causal, score_m(n_full), NEG)
    ssa[...] = jnp.where(causal, score_s(n_full), NEG)
    smb[...] = score_m(full_tile(0))
    ssb[...] = score_s(full_tile(0))
    soft_m(sma, pma, ama)
    soft_s(ssa, psa, asa)

    def pair(jj, c):
        k = 2 * jj + 1
        prev = jnp.where(jj == 0, n_full, k - 2)
        sma[...] = score_m(full_tile(k))
        ssa[...] = score_s(full_tile(k))
        soft_m(smb, pmb, amb)
        soft_s(ssb, psb, asb)
        pv_m(pma, ama, prev)
        pv_s(psa, asa, prev)
        smb[...] = score_m(full_tile(k + 1))
        ssb[...] = score_s(full_tile(k + 1))
        soft_m(sma, pma, ama)
        soft_s(ssa, psa, asa)
        pv_m(pmb, amb, k - 1)
        pv_s(psb, asb, k - 1)
        return c

    n_pair = n_full // 2
    lax.fori_loop(0, n_pair, pair, 0)
    last = jnp.where(n_pair == 0, n_full, 2 * n_pair - 1)
    pv_m(pma, ama, last)
    pv_s(psa, asa, last)

    @pl.when(n_full % 2 == 1)
    def _():
        soft_m(smb, pmb, amb)
        soft_s(ssb, psb, asb)
        pv_m(pmb, amb, n_full - 1)
        pv_s(psb, asb, n_full - 1)

    o_s = _finish_t(acc_ref)
    o_m = _finish_t(accm_ref)
    c_t = jnp.concatenate([_bdot(wuvt_ref[hd], o_m[:, hd * tq:(hd + 1) * tq]) for hd in range(MLA_HEADS)], axis=0)
    c_ref[...] = c_t.T
    o_w = ow_ref[...]

    g_t = g_ref[...].T
    outs = []
    for hd in range(NSA_HEADS):
        sl = slice(hd * tq, (hd + 1) * tq)
        r0 = MISC_NG + 3 * hd
        outs.append(g_t[r0:r0 + 1, :] * o_c[:, sl] + g_t[r0 + 1:r0 + 2, :] * o_s[:, sl]
                    + g_t[r0 + 2:r0 + 3, :] * o_w[:, sl])
    o_ref[...] = jnp.concatenate(outs, axis=0).T


def _attn_prompt(qm, km3, vmt3, wuvt, q, gates, P, pe, ov, ks3, vst3, kw3, vwt3, *, tq, nc, n_sel):
    nb, nk, tk, dks = ks3.shape
    nw, tw = kw3.shape[1:3]
    assert tw == tq
    N = q.shape[1]
    nq = N // nb // tq
    ncp = P.shape[1]
    nblk = ov.shape[0]
    R = 4 * tq
    out = pl.BlockSpec((tq, 256), lambda b, i: (b * nq + i, 0))
    whole = lambda a: pl.BlockSpec((None,) + a.shape[1:], lambda b, i: (b,) + (0,) * (a.ndim - 1))
    return pl.pallas_call(
        functools.partial(_attn_prompt_kernel, nc=nc, n_sel=n_sel),
        grid=(nb, nq),
        in_specs=[pl.BlockSpec((4, tq, MLA_CACHE_W), lambda b, i: (0, b * nq + i, 0)),
                  whole(km3), whole(vmt3),
                  pl.BlockSpec((4, MLA_V, MLA_KV_LORA), lambda b, i: (0, 0, 0)),
                  pl.BlockSpec((4, tq, NSA_HD), lambda b, i: (0, b * nq + i, 0)),
                  pl.BlockSpec((tq, 128), lambda b, i: (b * nq + i, 0)),
                  pl.BlockSpec((None, ncp, 256), lambda b, i: (b, 0, 0)),
                  pl.BlockSpec((8, 128), lambda b, i: (0, 0)),
                  pl.BlockSpec((nblk, ncp), lambda b, i: (0, 0)),
                  whole(ks3), whole(vst3), whole(kw3), whole(vwt3)],
        out_specs=[out, out],
        out_shape=[jax.ShapeDtypeStruct((N, 256), F32), jax.ShapeDtypeStruct((N, 256), F32)],
        scratch_shapes=[pltpu.VMEM((1, R), F32), pltpu.VMEM((NSA_HD + VSUM_ROWS, R), F32),
                        pltpu.VMEM((1, R), F32), pltpu.VMEM((MLA_KV_LORA + VSUM_ROWS, R), F32)]
        + [pltpu.VMEM((NSA_HD, R), F32)]
        + [pltpu.VMEM((tk, R), F32), pltpu.VMEM((tk, R), BF16), pltpu.VMEM((1, R), F32)] * 4,
        compiler_params=_cparams(("arbitrary", "arbitrary")),
        name="attn_prompt",
    )(qm, km3, vmt3, wuvt, q, gates, P, pe, ov, ks3, vst3, kw3, vwt3)


def _sample_queries(q_ref):
    nh, T, d = q_ref.shape
    q = q_ref[...].reshape(nh * T, d)
    return jnp.concatenate([q, jnp.zeros((SAMPLE_R - nh * T, d), F32)], axis=0).astype(BF16)


def _rows_attend(parts):
    m = functools.reduce(jnp.maximum, [jnp.max(s, axis=1, keepdims=True) for s, _ in parts])
    num, den = 0.0, 0.0
    for s, pv in parts:
        p = jnp.exp2(s - m)
        den = den + jnp.sum(p, axis=1, keepdims=True)
        num = num + pv(p.astype(BF16))
    return num / jnp.maximum(den, 1e-30)


def _page_copy(cache_ref, pt_ref, buf_ref, sem_ref, layer, n_pages, b, slot, j):
    return pltpu.make_async_copy(cache_ref.at[layer, pt_ref[b * n_pages + j]], buf_ref.at[slot, j], sem_ref.at[slot])


def _fetch_pages(cache_ref, pt_ref, buf_ref, sem_ref, layer, n_pages):
    b = pl.program_id(0)
    slot = b % 2
    copy = functools.partial(_page_copy, cache_ref, pt_ref, buf_ref, sem_ref, layer, n_pages)

    @pl.when(b == 0)
    def _():
        for j in range(n_pages):
            copy(b, slot, j).start()

    @pl.when(b + 1 < pl.num_programs(0))
    def _():
        for j in range(n_pages):
            copy(b + 1, 1 - slot, j).start()

    for j in range(n_pages):
        copy(b, slot, j).wait()
    return slot


def _mla_sample_kernel(pt_ref, cache_ref, q_ref, new_ref, wuv_ref, o_ref, buf_ref, sem_ref, *, layer, n_pages):
    slot = _fetch_pages(cache_ref, pt_ref, buf_ref, sem_ref, layer, n_pages)
    _, T, dk = q_ref.shape
    R = MLA_HEADS * T
    q = q_ref[...].reshape(R, dk).astype(BF16)
    kt = jnp.concatenate([buf_ref[slot, j].astype(BF16) for j in range(n_pages)], axis=1)
    vt = kt[0:MLA_KV_LORA]
    new = new_ref[...].astype(BF16)
    k_t = lax.broadcasted_iota(jnp.int32, (R, T), 1)
    q_t = lax.broadcasted_iota(jnp.int32, (R, T), 0) % T
    o = _rows_attend([
        (jnp.dot(q, kt, preferred_element_type=F32), lambda p: _dot_nt(p, vt)),
        (jnp.where(k_t <= q_t, _dot_nt(q, new), NEG),
         lambda p: jnp.dot(p, new[:, 0:MLA_KV_LORA], preferred_element_type=F32))])
    o_ref[...] = jnp.concatenate([_bdot(o[hd * T:(hd + 1) * T, :], wuv_ref[hd]) for hd in range(MLA_HEADS)], axis=1)


def _mla_sample(cache_mla_t, pt_flat, q, rows, wuv, layer, nb, n_pages):
    T = q.shape[1] // nb
    N = q.shape[1]
    page = cache_mla_t.shape[3]
    grid_spec = pltpu.PrefetchScalarGridSpec(
        num_scalar_prefetch=1,
        grid=(nb,),
        in_specs=[pl.BlockSpec(memory_space=pl.ANY),
                  pl.BlockSpec((4, T, MLA_CACHE_W), lambda b, pt: (0, b, 0)),
                  pl.BlockSpec((T, MLA_CACHE_W), lambda b, pt: (b, 0)),
                  pl.BlockSpec((4, MLA_KV_LORA, MLA_V), lambda b, pt: (0, 0, 0))],
        out_specs=pl.BlockSpec((T, 256), lambda b, pt: (b, 0)),
        scratch_shapes=[pltpu.VMEM((2, n_pages, MLA_CACHE_W, page), F32), pltpu.SemaphoreType.DMA((2,))],
    )
    return pl.pallas_call(
        functools.partial(_mla_sample_kernel, layer=layer, n_pages=n_pages),
        grid_spec=grid_spec,
        out_shape=jax.ShapeDtypeStruct((N, 256), F32),
        compiler_params=_cparams(("arbitrary",)),
        name="mla_sample",
    )(pt_flat, cache_mla_t, q, rows, wuv)


def _nsa_sample_kernel(pt_ref, cache_ref, q_ref, g_ref, w4_ref, perm_ref, pe_ref, ov_ref, e_ref, new_ref, wc_ref,
                       wn_ref, wnt_ref, o_ref, wo_ref, buf_ref, sem_ref, y_ref, *, layer, n_pages, nc, n_sel, past):
    slot = _fetch_pages(cache_ref, pt_ref, buf_ref, sem_ref, layer, n_pages)
    _, T, _ = q_ref.shape
    R = NSA_HEADS * T
    half = n_pages // 2
    cpp = y_ref.shape[1] // NSA_CMP_STRIDE
    q_f32 = q_ref[...].reshape(R, NSA_HD)
    q = q_f32.astype(BF16)
    q_pos = past + lax.broadcasted_iota(jnp.int32, (R, 1), 0) % T

    for jp in range(half):
        pair = jnp.concatenate([buf_ref[slot, jp, 0:128, :], buf_ref[slot, jp + half, 0:128, :]], axis=0)
        y_ref[jp] = _dot_nt(perm_ref[...], pair)
    slabs = []
    for r in range(NSA_CMP_STRIDE):
        yr = y_ref[:, r * cpp:(r + 1) * cpp, :].reshape(half * cpp, 256)
        slabs.append(jnp.concatenate([yr[:, 0:128], yr[:, 128:256]], axis=0))
    P = _bdot(jnp.concatenate(slabs, axis=1), w4_ref[...].reshape(NSA_CMP_STRIDE * 128, 256))

    lane = lax.broadcasted_iota(jnp.int32, (1, SAMPLE_R), 1)
    q_pos_l = past + lane % T
    kcv = _compressed_kv(P, pe_ref, nc)
    o_c, p_c = _compressed_branch(_sample_queries(q_ref), kcv, q_pos_l, nc)
    o_c = o_c.T[0:R]
    imp = jnp.dot(ov_ref[...], p_c.astype(BF16), preferred_element_type=F32)
    imp = jnp.where(lane < R, imp, 0.0)
    tot = imp
    for k in range(1, NSA_HEADS):
        tot = tot + pltpu.roll(imp, k * T, 1) + pltpu.roll(imp, SAMPLE_R - R + k * T, 1)
    n_real = -(-(-(-(past + T) // NSA_SEL_BLOCK)) // 8) * 8
    bias = jnp.concatenate([_select_blocks(tot[0:n_real], q_pos_l, n_sel),
                            jnp.full((tot.shape[0] - n_real, SAMPLE_R), NEG, F32)], axis=0).T[0:R]

    n_blk_past = e_ref.shape[0]
    kvt = jnp.concatenate([buf_ref[slot, j, 128:256, :].astype(BF16) for j in range(n_pages)], axis=1)
    k_sel = jnp.concatenate([e_ref[...], kvt[0:64]], axis=0)
    q_sel = jnp.concatenate([bias[:, 0:n_blk_past], q_f32], axis=1).astype(BF16)
    vs_t = kvt[64:128]
    new = new_ref[...].astype(BF16)
    k_new = past + lax.broadcasted_iota(jnp.int32, (R, T), 1)
    s_new = _dot_nt(q, new[:, 128:192]) + bias[:, n_blk_past:n_blk_past + 1]
    o_s = _rows_attend([
        (jnp.dot(q_sel, k_sel, preferred_element_type=F32), lambda p: _dot_nt(p, vs_t)),
        (jnp.where(k_new <= q_pos, s_new, NEG), lambda p: jnp.dot(p, new[:, 192:256], preferred_element_type=F32))])

    wc = wc_ref[...].astype(BF16)
    n_buf = wc.shape[1]
    k_buf = past - n_buf + lax.broadcasted_iota(jnp.int32, (R, n_buf), 1)
    ok_buf = (k_buf >= 0) & (k_buf <= q_pos) & (k_buf > q_pos - NSA_WINDOW)
    wn = wn_ref[...].astype(BF16)
    ok_new = (k_new <= q_pos) & (k_new > q_pos - NSA_WINDOW)
    o_w = _rows_attend([
        (jnp.where(ok_buf, jnp.dot(q, wc[0:64], preferred_element_type=F32), NEG), lambda p: _dot_nt(p, wc[64:128])),
        (jnp.where(ok_new, _dot_nt(q, wn[:, 0:64]), NEG),
         lambda p: jnp.dot(p, wn[:, 64:128], preferred_element_type=F32))])

    g = g_ref[...]
    outs = []
    for hd in range(NSA_HEADS):
        sl = slice(hd * T, (hd + 1) * T)
        r0 = MISC_NG + 3 * hd
        outs.append(g[:, r0:r0 + 1] * o_c[sl] + g[:, r0 + 1:r0 + 2] * o_s[sl] + g[:, r0 + 2:r0 + 3] * o_w[sl])
    o_ref[...] = jnp.concatenate(outs, axis=1)

    n_win = wo_ref.shape[1]
    keep = n_win - T
    wo_ref[:, 0:keep] = wc_ref[:, n_buf - keep:n_buf]
    wo_ref[:, keep:n_win] = wnt_ref[...]


def _nsa_sample(cache_nsa_t, pt_flat, q, gates, w4, pe, ov, new4, win_cache_t, win_new, layer, nb, n_pages,
                *, nc, n_sel, past):
    N = q.shape[1]
    T = N // nb
    nblk, ncp = ov.shape
    n_buf = win_cache_t.shape[3]
    page = cache_nsa_t.shape[3]
    cpp = page // NSA_CMP_STRIDE
    assert n_pages % 2 == 0 and cpp == 8 and ncp == n_pages * cpp
    n_win = min(NSA_WINDOW, n_buf + T)
    win_new_t = jnp.swapaxes(win_new.reshape(nb, T, 128), 1, 2)
    n_blk_past = n_pages * page // NSA_SEL_BLOCK
    key_blk = np.arange(n_pages * page)[None, :] // NSA_SEL_BLOCK
    onehot = jnp.asarray((key_blk == np.arange(n_blk_past)[:, None]).astype(np.float32), dtype=BF16)
    row = np.arange(page)
    perm = jnp.asarray((np.arange(page)[None, :] == ((row % cpp) * NSA_CMP_STRIDE + row // cpp)[:, None])
                       .astype(np.float32), dtype=BF16)
    const = lambda a: pl.BlockSpec(a.shape, lambda b, pt: (0,) * a.ndim)
    grid_spec = pltpu.PrefetchScalarGridSpec(
        num_scalar_prefetch=1,
        grid=(nb,),
        in_specs=[pl.BlockSpec(memory_space=pl.ANY),
                  pl.BlockSpec((4, T, NSA_HD), lambda b, pt: (0, b, 0)),
                  pl.BlockSpec((T, 128), lambda b, pt: (b, 0)),
                  const(w4), const(perm), const(pe), const(ov), const(onehot),
                  pl.BlockSpec((T, 256), lambda b, pt: (b, 0)),
                  pl.BlockSpec((None, None, 128, n_buf), lambda b, pt: (layer, b, 0, 0)),
                  pl.BlockSpec((T, 128), lambda b, pt: (b, 0)),
                  pl.BlockSpec((None, 128, T), lambda b, pt: (b, 0, 0))],
        out_specs=[pl.BlockSpec((T, 256), lambda b, pt: (b, 0)),
                   pl.BlockSpec((None, 128, n_win), lambda b, pt: (b, 0, 0))],
        scratch_shapes=[pltpu.VMEM((2, n_pages, 4 * NSA_HD, page), F32), pltpu.SemaphoreType.DMA((2,)),
                        pltpu.VMEM((n_pages // 2, page, 256), F32)],
    )
    return pl.pallas_call(
        functools.partial(_nsa_sample_kernel, layer=layer, n_pages=n_pages, nc=nc, n_sel=n_sel, past=past),
        grid_spec=grid_spec,
        out_shape=[jax.ShapeDtypeStruct((N, 256), F32), jax.ShapeDtypeStruct((nb, 128, n_win), F32)],
        compiler_params=_cparams(("arbitrary",)),
        name="nsa_sample",
    )(pt_flat, cache_nsa_t, q, gates, w4, perm, pe, ov, onehot, new4, win_cache_t, win_new, win_new_t)


def _block_diag(w):
    n, a, b = w.shape
    out = jnp.zeros((n * a, n * b), w.dtype)
    for j in range(n):
        out = out.at[j * a:(j + 1) * a, j * b:(j + 1) * b].set(w[j])
    return out


def _rot_cols(w):
    half = w.shape[-1] // 2
    return jnp.concatenate([-w[..., half:], w[..., :half]], axis=-1)


def _layer_params(l, W):
    idx = np.cumsum(IN_SPLITS)[:-1].tolist()
    lru_x, lru_g, conf_u, cq, ckv, kr, nq, nkv, ng = jnp.split(W['w_in'][l], idx, axis=-1)
    pad = jnp.zeros((D_MODEL, D_INP - C_MISC - 2 * MLA_ROPE - 3 * NSA_HEADS), F32)
    w_in = jnp.concatenate([lru_x, lru_g, conf_u, cq, ckv, nq, nkv, kr, _rot_cols(kr), ng, pad], axis=-1)
    wuq = W['mla_wuq'][l].reshape(MLA_Q_LORA, MLA_HEADS, MLA_NOPE + MLA_ROPE)
    wuq_rope = wuq[:, :, MLA_NOPE:]
    wuq = jnp.concatenate([wuq[:, :, :MLA_NOPE].reshape(MLA_Q_LORA, -1), wuq_rope.reshape(MLA_Q_LORA, -1),
                           _rot_cols(wuq_rope).reshape(MLA_Q_LORA, -1)], axis=-1)
    wuk = _block_diag(jnp.transpose(W['mla_wuk'][l], (1, 2, 0)))
    wuv = jnp.transpose(W['mla_wuv'][l], (1, 0, 2))
    wk = W['nsa_wc_k'][l].reshape(NSA_CMP_BLOCK, NSA_HD, NSA_HD)
    wv = W['nsa_wc_v'][l].reshape(NSA_CMP_BLOCK, NSA_HD, NSA_HD)
    z = jnp.zeros((NSA_CMP_STRIDE, NSA_HD, NSA_HD), F32)
    w4 = jnp.concatenate([jnp.concatenate([wk[:16], z, wk[16:], z], axis=2),
                          jnp.concatenate([z, wv[:16], z, wv[16:]], axis=2)], axis=1)
    pe = jnp.concatenate([W['nsa_pe_k'][l], W['nsa_pe_v'][l]], axis=-1)
    pe2 = jnp.concatenate([pe[:16, None], pe[16:, None], jnp.zeros((16, 6, 128), F32)], axis=1)
    r1 = lambda a: a.reshape(1, -1)
    return dict(
        w_in=w_in.astype(BF16), g_pre_mix=r1(W['g_pre_mix'][l]), g_post_mix=r1(W['g_post_mix'][l]),
        g_pre_ffn=r1(W['g_pre_ffn'][l]), g_post_ffn=r1(W['g_post_ffn'][l]),
        mla_gq=r1(W['mla_gq'][l]), mla_gkv=r1(W['mla_gkv'][l]),
        wuq=wuq.astype(BF16), wuk=wuk.astype(BF16), wuv=wuv.astype(BF16),
        wuvt=jnp.transpose(wuv, (0, 2, 1)).astype(BF16),
        w4=w4.astype(BF16), pe2=pe2,
        lru_conv_w=W['lru_conv_w'][l], lru_conv_b=r1(W['lru_conv_b'][l]),
        lru_wa_bd=_block_diag(W['lru_wa'][l]).astype(BF16), lru_ba=r1(W['lru_ba'][l]),
        lru_wx_bd=_block_diag(W['lru_wx'][l]).astype(BF16), lru_bx=r1(W['lru_bx'][l]),
        lru_lam=r1(W['lru_lam'][l]),
        conf_dw_w=W['conf_dw_w'][l], conf_dw_b=r1(W['conf_dw_b'][l]),
        conf_ln_g=r1(W['conf_ln_g'][l]), conf_ln_b=r1(W['conf_ln_b'][l]),
        w_out=W['w_out'][l].astype(BF16), w_ff1=W['w_ff1'][l].astype(BF16), w_ff2=W['w_ff2'][l].astype(BF16),
    )


def _rope_tables(pos):
    half = MLA_ROPE // 2
    freq = jnp.power(ROPE_THETA, -jnp.arange(half, dtype=F32) / half)
    ang = pos.astype(F32)[:, None] * freq
    cosv = jnp.tile(jnp.cos(ang), (1, 2 * MLA_HEADS))
    sinv = jnp.tile(jnp.sin(ang), (1, 2 * MLA_HEADS))
    return cosv, sinv


def _overlap_t(nblk_pad, ncp, nc):
    n = np.arange(ncp)[None, :]
    j = np.arange(nblk_pad)[:, None]
    ov = (n * NSA_CMP_STRIDE < j * NSA_SEL_BLOCK + NSA_SEL_BLOCK) & (n * NSA_CMP_STRIDE + NSA_CMP_BLOCK > j * NSA_SEL_BLOCK)
    return jnp.asarray((ov & (n < nc)).astype(np.float32), dtype=BF16)


def _key_tiles(a, nb, t, front=0):
    k = a.reshape(nb, -1, t, a.shape[-1])
    return jnp.pad(k, ((0, 0), (front, 0), (0, 0), (0, 0)))


def _value_tiles_t(a, nb, t, front=0):
    v = jnp.swapaxes(_key_tiles(a, nb, t, front), 2, 3)
    ones = jnp.ones(v.shape[:2] + (1, t), v.dtype)
    return jnp.concatenate([v, ones, jnp.zeros(v.shape[:2] + (VSUM_ROWS - 1, t), v.dtype)], axis=2)


def kernel(x_prompt, x_sample, c_prompt, c_sample, state_lru_h, state_lru_conv, state_conv, cache_mla, cache_nsa,
           cache_nsa_win, page_table, w_mod, b_mod, g_pre_mix, g_post_mix, g_pre_ffn, g_post_ffn, w_in, lru_conv_w,
           lru_conv_b, lru_wa, lru_ba, lru_wx, lru_bx, lru_lam, conf_dw_w, conf_dw_b, conf_ln_g, conf_ln_b, mla_gq,
           mla_wuq, mla_gkv, mla_wuk, mla_wuv, nsa_pe_k, nsa_wc_k, nsa_pe_v, nsa_wc_v, w_out, w_ff1, w_ff2):
    W = dict(w_in=w_in, g_pre_mix=g_pre_mix, g_post_mix=g_post_mix, g_pre_ffn=g_pre_ffn, g_post_ffn=g_post_ffn,
             lru_conv_w=lru_conv_w, lru_conv_b=lru_conv_b, lru_wa=lru_wa, lru_ba=lru_ba, lru_wx=lru_wx, lru_bx=lru_bx,
             lru_lam=lru_lam, conf_dw_w=conf_dw_w, conf_dw_b=conf_dw_b, conf_ln_g=conf_ln_g, conf_ln_b=conf_ln_b,
             mla_gq=mla_gq, mla_wuq=mla_wuq, mla_gkv=mla_gkv, mla_wuk=mla_wuk, mla_wuv=mla_wuv,
             nsa_pe_k=nsa_pe_k, nsa_wc_k=nsa_wc_k, nsa_pe_v=nsa_pe_v, nsa_wc_v=nsa_wc_v,
             w_out=w_out, w_ff1=w_ff1, w_ff2=w_ff2)
    L = w_in.shape[0]
    nbp, S, D = x_prompt.shape
    nbs, T, _ = x_sample.shape
    n_pages = page_table.shape[1]
    page = cache_mla.shape[2]
    past = n_pages * page
    Np, Ns = nbp * S, nbs * T
    tq = min(Q_TILE, S)
    tk = min(K_TILE, S)
    tm_p = min(256, Np)
    tm_s = min(256, Ns)
    tc = min(512, S)

    nc_p = S // NSA_CMP_STRIDE - 1
    nblk_p = -(-S // NSA_SEL_BLOCK)
    ov_p = _overlap_t(nblk_p, S // NSA_CMP_STRIDE, nc_p)
    assert NSA_WINDOW % tq == 0 and S % tk == 0 and tk % tq == 0
    blk_of_pos = jnp.arange(S, dtype=jnp.int32)[:, None] // NSA_SEL_BLOCK
    blk_onehot_p = jnp.tile((blk_of_pos == jnp.arange(nblk_p, dtype=jnp.int32)[None, :]).astype(BF16), (nbp, 1))
    len_s = past + T
    assert len_s // NSA_CMP_STRIDE == past // NSA_CMP_STRIDE and past % NSA_SEL_BLOCK == 0 and T <= NSA_SEL_BLOCK
    nc_s = len_s // NSA_CMP_STRIDE - 1
    nblk_s = -(-len_s // NSA_SEL_BLOCK)
    nblk_s_pad = -(-nblk_s // 128) * 128
    ov_s = _overlap_t(nblk_s_pad, past // NSA_CMP_STRIDE, nc_s)

    rows = nbp + nbs
    rows_pad = -(-rows // 8) * 8
    c_all = jnp.concatenate([c_prompt, c_sample, jnp.zeros((rows_pad - rows, D), F32)], axis=0)
    mod = _modulation(c_all, w_mod.astype(BF16), b_mod)

    params = [_layer_params(l, W) for l in range(L)]
    pe_all = _pe_term(jnp.stack([p['pe2'] for p in params]), jnp.stack([p['w4'] for p in params]))

    cos_p, sin_p = _rope_tables(jnp.arange(S, dtype=jnp.int32))
    cos_s, sin_s = _rope_tables(past + jnp.arange(T, dtype=jnp.int32))
    cos_s = jnp.tile(cos_s, (nbs, 1))
    sin_s = jnp.tile(sin_s, (nbs, 1))
    pt_flat = page_table.reshape(-1).astype(jnp.int32)
    cache_mla_t = jnp.swapaxes(cache_mla, 2, 3)
    cache_nsa_t = jnp.transpose(cache_nsa, (0, 1, 3, 4, 2)).reshape(L, cache_nsa.shape[1], 4 * NSA_HD, page)
    win_cache_t = jnp.transpose(cache_nsa_win, (0, 1, 3, 4, 2)).reshape(L, nbs, 2 * NSA_HD, cache_nsa_win.shape[2])

    xp = x_prompt.reshape(Np, D)
    xs = x_sample.reshape(Ns, D)
    outs_p, outs_s = [], []
    for l in range(L):
        P = params[l]
        pe = pe_all[l]
        mp = [m.reshape(nbp, 1, D) for m in jnp.split(mod[l, :nbp], 6, axis=-1)]
        ms = [jnp.repeat(m, T, axis=0) for m in jnp.split(mod[l, nbp:rows], 6, axis=-1)]

        lru, glu, qm, row, nq, nsa4, win, gates = _in_proj(
            xp, mp[0], mp[1], P['g_pre_mix'], P['w_in'], P['mla_gq'], P['wuq'], P['wuk'], P['mla_gkv'],
            cos_p, sin_p, per_token=False, tm=tm_p)
        a_out, b_out, h_last = _seq_prompt(lru.reshape(nbp, S, 512), glu.reshape(nbp, S, 256), P, tc=tc)
        rowb = row.astype(BF16)
        k3 = _key_tiles(rowb, nbp, tk)
        vt3 = _value_tiles_t(rowb[:, :MLA_KV_LORA], nbp, tk)
        Pp = _cmp_prompt(nsa4, P['w4'], nbp)
        nsab = nsa4.astype(BF16)
        ks3 = _key_tiles(jnp.concatenate([blk_onehot_p, nsab[:, 128:192]], axis=1), nbp, tk)
        vst3 = _value_tiles_t(nsab[:, 192:256], nbp, tk)
        winb = win.astype(BF16)
        kw3 = _key_tiles(winb[:, 0:64], nbp, tq, front=NSA_WINDOW // tq)
        vwt3 = _value_tiles_t(winb[:, 64:128], nbp, tq, front=NSA_WINDOW // tq)
        c_out, d_out = _attn_prompt(qm, k3, vt3, P['wuvt'], nq, gates, Pp, pe, ov_p, ks3, vst3, kw3, vwt3, tq=tq,
                                    nc=nc_p, n_sel=min(NSA_N_SEL, nblk_p))
        xp = _post(xp, a_out.reshape(Np, 256), b_out.reshape(Np, 256), c_out, d_out, mp[2], mp[3], mp[4], mp[5],
                   P['w_out'], P['g_post_mix'], P['g_pre_ffn'], P['g_post_ffn'], P['w_ff1'], P['w_ff2'],
                   per_token=False, tm=tm_p)
        n_win = min(NSA_WINDOW, S)
        outs_p.append((h_last[:, 0], lru.reshape(nbp, S, 512)[:, S - (LRU_CONV - 1):, :D_LRU],
                       glu.reshape(nbp, S, 256)[:, S - (CONF_KERNEL - 1):],
                       row.reshape(nbp, S, MLA_CACHE_W), nsa4.reshape(nbp, S, 4, NSA_HD),
                       win.reshape(nbp, S, 2, NSA_HD)[:, S - n_win:]))

        lru, glu, qm, row, nq, nsa4, win, gates = _in_proj(
            xs, ms[0], ms[1], P['g_pre_mix'], P['w_in'], P['mla_gq'], P['wuq'], P['wuk'], P['mla_gkv'],
            cos_s, sin_s, per_token=True, tm=tm_s)
        tmaj = lambda a: jnp.swapaxes(a, 0, 1)
        a_t, b_t, h_last, ncs, ngs = _seq_sample(
            tmaj(lru.reshape(nbs, T, 512)), tmaj(glu.reshape(nbs, T, 256)), tmaj(state_lru_conv[l]),
            tmaj(state_conv[l]), state_lru_h[l], P)
        c_out = _mla_sample(cache_mla_t, pt_flat, qm, row, P['wuv'], l, nbs, n_pages)
        d_out, win_t = _nsa_sample(cache_nsa_t, pt_flat, nq, gates, P['w4'], pe, ov_s, nsa4, win_cache_t, win, l, nbs,
                                   n_pages, nc=nc_s, n_sel=min(NSA_N_SEL, nblk_s), past=past)
        xs = _post(xs, tmaj(a_t).reshape(Ns, 256), tmaj(b_t).reshape(Ns, 256), c_out, d_out, ms[2], ms[3], ms[4],
                   ms[5], P['w_out'], P['g_post_mix'], P['g_pre_ffn'], P['g_post_ffn'], P['w_ff1'], P['w_ff2'],
                   per_token=True, tm=tm_s)
        win_s = jnp.transpose(win_t.reshape(nbs, 2, NSA_HD, win_t.shape[2]), (0, 3, 1, 2))
        outs_s.append((h_last, tmaj(ncs), tmaj(ngs), row.reshape(nbs, T, MLA_CACHE_W),
                       nsa4.reshape(nbs, T, 4, NSA_HD), win_s))

    stack = lambda outs, i: jnp.stack([o[i] for o in outs])
    return (xp.reshape(nbp, S, D), xs.reshape(nbs, T, D),
            stack(outs_p, 0), stack(outs_s, 0), stack(outs_p, 1), stack(outs_s, 1), stack(outs_p, 2), stack(outs_s, 2),
            stack(outs_p, 3), stack(outs_s, 3), stack(outs_p, 4), stack(outs_s, 4), stack(outs_p, 5), stack(outs_s, 5))
```

```python
import functools

import numpy as np
import jax
import jax.numpy as jnp
from jax import lax
from jax.experimental import pallas as pl
from jax.experimental.pallas import tpu as pltpu

F32 = jnp.float32
BF16 = jnp.bfloat16

D_MODEL = 1024
D_LRU = 256
LRU_BLOCKS = 4
LRU_CONV = 4
LRU_C = 8.0
D_CONF = 256
CONF_KERNEL = 31
MLA_HEADS = 4
MLA_NOPE = 64
MLA_ROPE = 32
MLA_V = 64
MLA_Q_LORA = 256
MLA_KV_LORA = 128
MLA_CACHE_W = MLA_KV_LORA + MLA_ROPE
ROPE_THETA = 10000.0
NSA_HEADS = 4
NSA_HD = 64
NSA_CMP_BLOCK = 32
NSA_CMP_STRIDE = 16
NSA_SEL_BLOCK = 64
NSA_N_SEL = 16
NSA_WINDOW = 512
NSA_FORCE = 1.0e4
D_FF = 4 * D_MODEL
EPS = 1e-6
IN_SPLITS = (D_LRU, D_LRU, 2 * D_CONF, MLA_Q_LORA, MLA_KV_LORA, MLA_ROPE, NSA_HEADS * NSA_HD, 6 * NSA_HD, 3 * NSA_HEADS)

C_LRU = 0
C_CONF = 512
C_CQ = 1024
C_CKV = 1280
C_NQ = 1408
C_NKV = 1664
C_MISC = 2048
D_INP = 2176
MISC_NG = 64

NEG = -1e30
LOG2E = 1.4426950408889634
VSUM_ROWS = 8
VMEM_LIMIT_V7X = 56 * 1024 * 1024
Q_TILE = 256
K_TILE = 512
KEY_GROUPS = 4
SAMPLE_R = 128


def _cparams(sem):
    return pltpu.CompilerParams(dimension_semantics=sem, vmem_limit_bytes=VMEM_LIMIT_V7X)


def _rms(x, g):
    return x * lax.rsqrt(jnp.mean(x * x, axis=-1, keepdims=True) + EPS) * g


def _bdot(a, b):
    return jnp.dot(a.astype(BF16), b.astype(BF16), preferred_element_type=F32)


def _dot_nt(a, b):
    return lax.dot_general(a.astype(BF16), b.astype(BF16), (((1,), (1,)), ((), ())), preferred_element_type=F32)


def _dot_tn(a, b):
    return lax.dot_general(a.astype(BF16), b.astype(BF16), (((0,), (0,)), ((), ())), preferred_element_type=F32)


def _mod_kernel(c_ref, w_ref, b_ref, o_ref):
    o_ref[...] = _bdot(jax.nn.silu(c_ref[...]), w_ref[...]) + b_ref[...]


def _modulation(c_all, w_mod, b_mod):
    L, D, D6 = w_mod.shape
    rows = c_all.shape[0]
    tn = 1536
    return pl.pallas_call(
        _mod_kernel,
        grid=(L, D6 // tn),
        in_specs=[pl.BlockSpec((rows, D), lambda l, j: (0, 0)),
                  pl.BlockSpec((None, D, tn), lambda l, j: (l, 0, j)),
                  pl.BlockSpec((None, 1, tn), lambda l, j: (l, 0, j))],
        out_specs=pl.BlockSpec((None, rows, tn), lambda l, j: (l, 0, j)),
        out_shape=jax.ShapeDtypeStruct((L, rows, D6), F32),
        compiler_params=_cparams(("arbitrary", "arbitrary")),
        name="modulation",
    )(c_all, w_mod, b_mod.reshape(L, 1, D6))


def _pe_kernel(pe_ref, w_ref, o_ref):
    acc = jnp.zeros((8, 256), F32)
    for r in range(NSA_CMP_STRIDE):
        acc = acc + _bdot(pe_ref[r], w_ref[r])
    o_ref[...] = jnp.broadcast_to(acc[0:1, 0:128] + acc[1:2, 128:256], (8, 128))


def _pe_term(pe2, w4):
    L = pe2.shape[0]
    return pl.pallas_call(
        _pe_kernel,
        grid=(L,),
        in_specs=[pl.BlockSpec((None, NSA_CMP_STRIDE, 8, 128), lambda l: (l, 0, 0, 0)),
                  pl.BlockSpec((None, NSA_CMP_STRIDE, 128, 256), lambda l: (l, 0, 0, 0))],
        out_specs=pl.BlockSpec((None, 8, 128), lambda l: (l, 0, 0)),
        out_shape=jax.ShapeDtypeStruct((L, 8, 128), F32),
        compiler_params=_cparams(("arbitrary",)),
        name="pe_term",
    )(pe2, w4)


def _in_proj_kernel(x_ref, sh_ref, sc_ref, g_ref, w_ref, gq_ref, wuq_ref, wuk_ref, gkv_ref, cos_ref, sin_ref,
                    lru_ref, glu_ref, qm_ref, row_ref, nq_ref, nsa4_ref, win_ref, gate_ref):
    h = _rms(x_ref[...], g_ref[...]) * (1.0 + sc_ref[...]) + sh_ref[...]
    u = _bdot(h, w_ref[...])
    lru_ref[...] = u[:, C_LRU:C_LRU + 512]
    glu_ref[...] = u[:, C_CONF:C_CONF + 256] * jax.nn.sigmoid(u[:, C_CONF + 256:C_CONF + 512])
    nsa4_ref[...] = u[:, C_NKV:C_NKV + 256]
    win_ref[...] = u[:, C_NKV + 256:C_NKV + 384]
    misc = u[:, C_MISC:C_MISC + 128]
    gate_ref[...] = jax.nn.sigmoid(misc)
    cosv = cos_ref[...]
    sinv = sin_ref[...]
    qr = _bdot(_rms(u[:, C_CQ:C_CQ + 256], gq_ref[...]), wuq_ref[...])
    q_rope = qr[:, 256:384] * cosv + qr[:, 384:512] * sinv
    q_lat = _bdot(qr[:, 0:256], wuk_ref[...])
    scale = (MLA_NOPE + MLA_ROPE) ** -0.5 * LOG2E
    for hd in range(MLA_HEADS):
        qm_ref[hd, :, 0:128] = q_lat[:, 128 * hd:128 * (hd + 1)] * scale
        qm_ref[hd, :, 128:160] = q_rope[:, 32 * hd:32 * (hd + 1)] * scale
    row_ref[:, 0:128] = _rms(u[:, C_CKV:C_CKV + 128], gkv_ref[...])
    row_ref[:, 128:160] = misc[:, 0:32] * cosv[:, 0:32] + misc[:, 32:64] * sinv[:, 0:32]
    nq = u[:, C_NQ:C_NQ + 256] * (NSA_HD ** -0.5 * LOG2E)
    for hd in range(NSA_HEADS):
        nq_ref[hd] = nq[:, 64 * hd:64 * (hd + 1)]


def _in_proj(x, sh, sc, g, w_in, gq, wuq, wuk, gkv, cosv, sinv, *, per_token, tm):
    N, D = x.shape
    nt = N // tm
    if per_token:
        mod_spec = pl.BlockSpec((tm, D), lambda i: (i, 0))
        tab_spec = pl.BlockSpec((tm, 128), lambda i: (i, 0))
    else:
        per_b = nt // sh.shape[0]
        mod_spec = pl.BlockSpec((None, 1, D), lambda i: (i // per_b, 0, 0))
        tab_spec = pl.BlockSpec((tm, 128), lambda i: (i % per_b, 0))
    const = lambda shape: pl.BlockSpec(shape, lambda i: tuple(0 for _ in shape))
    row = lambda w: pl.BlockSpec((tm, w), lambda i: (i, 0))
    head = lambda w: pl.BlockSpec((4, tm, w), lambda i: (0, i, 0))
    return pl.pallas_call(
        _in_proj_kernel,
        grid=(nt,),
        in_specs=[row(D), mod_spec, mod_spec, const((1, D)), const((D, D_INP)), const((1, 256)),
                  const((256, 512)), const((256, 512)), const((1, 128)), tab_spec, tab_spec],
        out_specs=[row(512), row(256), head(MLA_CACHE_W), row(MLA_CACHE_W), head(NSA_HD), row(256), row(128), row(128)],
        out_shape=[jax.ShapeDtypeStruct((N, 512), F32), jax.ShapeDtypeStruct((N, 256), F32),
                   jax.ShapeDtypeStruct((4, N, MLA_CACHE_W), F32), jax.ShapeDtypeStruct((N, MLA_CACHE_W), F32),
                   jax.ShapeDtypeStruct((4, N, NSA_HD), F32), jax.ShapeDtypeStruct((N, 256), F32),
                   jax.ShapeDtypeStruct((N, 128), F32), jax.ShapeDtypeStruct((N, 128), F32)],
        compiler_params=_cparams(("parallel",)),
        name="in_proj",
    )(x, sh, sc, g, w_in, gq, wuq, wuk, gkv, cosv, sinv)


def _post_kernel(x_ref, a_ref, b_ref, c_ref, d_ref, gt1_ref, sh2_ref, sc2_ref, gt2_ref,
                 wo_ref, gpm_ref, gpf_ref, gqf_ref, w1_ref, w2_ref, o_ref):
    y = _bdot(a_ref[...], wo_ref[0:256, :])
    y = y + _bdot(b_ref[...], wo_ref[256:512, :])
    y = y + _bdot(c_ref[...], wo_ref[512:768, :])
    y = y + _bdot(d_ref[...], wo_ref[768:1024, :])
    x = x_ref[...] + gt1_ref[...] * _rms(y, gpm_ref[...])
    h = (_rms(x, gpf_ref[...]) * (1.0 + sc2_ref[...]) + sh2_ref[...]).astype(BF16)
    f = jnp.zeros(x.shape, F32)
    fc = 1024
    for j in range(D_FF // fc):
        t = jnp.maximum(jnp.dot(h, w1_ref[:, j * fc:(j + 1) * fc], preferred_element_type=F32), 0.0)
        f = f + _bdot(t * t, w2_ref[j * fc:(j + 1) * fc, :])
    o_ref[...] = x + gt2_ref[...] * _rms(f, gqf_ref[...])


def _post(x, a, b, c, d, gt1, sh2, sc2, gt2, wo, gpm, gpf, gqf, w1, w2, *, per_token, tm):
    N, D = x.shape
    nt = N // tm
    if per_token:
        mod_spec = pl.BlockSpec((tm, D), lambda i: (i, 0))
    else:
        per_b = nt // gt1.shape[0]
        mod_spec = pl.BlockSpec((None, 1, D), lambda i: (i // per_b, 0, 0))
    const = lambda shape: pl.BlockSpec(shape, lambda i: tuple(0 for _ in shape))
    row = lambda w: pl.BlockSpec((tm, w), lambda i: (i, 0))
    return pl.pallas_call(
        _post_kernel,
        grid=(nt,),
        in_specs=[row(D), row(256), row(256), row(256), row(256), mod_spec, mod_spec, mod_spec, mod_spec,
                  const((D, D)), const((1, D)), const((1, D)), const((1, D)), const((D, D_FF)), const((D_FF, D))],
        out_specs=row(D),
        out_shape=jax.ShapeDtypeStruct((N, D), F32),
        compiler_params=_cparams(("parallel",)),
        name="post",
    )(x, a, b, c, d, gt1, sh2, sc2, gt2, wo, gpm, gpf, gqf, w1, w2)


def _lru_gates(y, gate_w, lam):
    wa, ba, wx, bx = gate_w
    r = jax.nn.sigmoid(_bdot(y, wa) + ba)
    i = jax.nn.sigmoid(_bdot(y, wx) + bx)
    log_a = -LRU_C * r * jax.nn.softplus(-lam)
    return jnp.exp(log_a), jnp.sqrt(1.0 - jnp.exp(2.0 * log_a)) * i * y


def _layernorm(y, g, b):
    mu = jnp.mean(y, axis=-1, keepdims=True)
    var = jnp.mean(jnp.square(y - mu), axis=-1, keepdims=True)
    return (y - mu) * lax.rsqrt(var + EPS) * g + b


def _seq_prompt_kernel(xg_ref, glu_ref, cw_ref, cb_ref, wa_ref, ba_ref, wx_ref, bx_ref, lam_ref,
                       dw_ref, db_ref, lng_ref, lnb_ref,
                       a_ref, b_ref, hl_ref, xbuf, gbuf, abuf, dbuf, hbuf, sbuf):
    i = pl.program_id(0)
    nb, tc, _ = glu_ref.shape

    @pl.when(i == 0)
    def _():
        xbuf[:, 0:8, :] = jnp.zeros((nb, 8, D_LRU), F32)
        gbuf[:, 0:32, :] = jnp.zeros((nb, 32, D_CONF), F32)
        hbuf[...] = jnp.zeros(hbuf.shape, F32)

    xbuf[:, 8:8 + tc, :] = xg_ref[:, :, 0:D_LRU]
    y = jnp.zeros((nb, tc, D_LRU), F32) + cb_ref[...]
    for k in range(LRU_CONV):
        y = y + xbuf[:, 8 - (LRU_CONV - 1) + k:8 - (LRU_CONV - 1) + k + tc, :] * cw_ref[k:k + 1, :]
    a, drive = _lru_gates(y.reshape(nb * tc, D_LRU),
                          (wa_ref[...], ba_ref[...], wx_ref[...], bx_ref[...]), lam_ref[...])
    abuf[...] = a.reshape(nb, tc, D_LRU)
    dbuf[...] = drive.reshape(nb, tc, D_LRU)

    def step(t, h):
        h = abuf[:, pl.ds(t, 1), :] * h + dbuf[:, pl.ds(t, 1), :]
        dbuf[:, pl.ds(t, 1), :] = h
        return h

    h = lax.fori_loop(0, tc, step, hbuf[...], unroll=8)
    hbuf[...] = h
    hl_ref[...] = jnp.broadcast_to(h, hl_ref.shape)
    a_ref[...] = dbuf[...] * jax.nn.gelu(xg_ref[:, :, D_LRU:2 * D_LRU])
    xbuf[:, 0:8, :] = xbuf[:, tc:tc + 8, :]

    gbuf[:, 32:32 + tc, :] = glu_ref[...]
    z = jnp.zeros((nb, tc, D_CONF), F32) + db_ref[...]
    first = 32 - (CONF_KERNEL - 1)
    for r in range(8):
        taps = [k for k in range(CONF_KERNEL) if (first + k) % 8 == r]
        rows = first + taps[-1] + tc - r
        sbuf[:, 0:rows, :] = gbuf[:, r:r + rows, :]
        for k in taps:
            a = first + k - r
            z = z + sbuf[:, a:a + tc, :] * dw_ref[k:k + 1, :]
    b_ref[...] = jax.nn.silu(_layernorm(z, lng_ref[...], lnb_ref[...]))
    gbuf[:, 0:32, :] = gbuf[:, tc:tc + 32, :]


def _seq_prompt(xg, glu, P, *, tc):
    nb, S, _ = glu.shape
    const = lambda shape: pl.BlockSpec(shape, lambda i: tuple(0 for _ in shape))
    seq = lambda w: pl.BlockSpec((nb, tc, w), lambda i: (0, i, 0))
    return pl.pallas_call(
        _seq_prompt_kernel,
        grid=(S // tc,),
        in_specs=[seq(512), seq(256), const((LRU_CONV, 256)), const((1, 256)), const((256, 256)), const((1, 256)),
                  const((256, 256)), const((1, 256)), const((1, 256)), const((CONF_KERNEL, 256)), const((1, 256)),
                  const((1, 256)), const((1, 256))],
        out_specs=[seq(256), seq(256), const((nb, 8, 256))],
        out_shape=[jax.ShapeDtypeStruct((nb, S, 256), F32), jax.ShapeDtypeStruct((nb, S, 256), F32),
                   jax.ShapeDtypeStruct((nb, 8, 256), F32)],
        scratch_shapes=[pltpu.VMEM((nb, tc + 8, 256), F32), pltpu.VMEM((nb, tc + 32, 256), F32),
                        pltpu.VMEM((nb, tc, 256), F32), pltpu.VMEM((nb, tc, 256), F32),
                        pltpu.VMEM((nb, 1, 256), F32), pltpu.VMEM((nb, tc + 32, 256), F32)],
        compiler_params=_cparams(("arbitrary",)),
        name="seq_prompt",
    )(xg, glu, P['lru_conv_w'], P['lru_conv_b'], P['lru_wa_bd'], P['lru_ba'], P['lru_wx_bd'], P['lru_bx'],
      P['lru_lam'], P['conf_dw_w'], P['conf_dw_b'], P['conf_ln_g'], P['conf_ln_b'])


def _seq_sample_kernel(xg_ref, glu_ref, cs_ref, gs_ref, h0_ref, cw_ref, cb_ref, wa_ref, ba_ref, wx_ref, bx_ref,
                       lam_ref, dw_ref, db_ref, lng_ref, lnb_ref,
                       a_ref, b_ref, hl_ref, ncs_ref, ngs_ref, xbuf, gbuf):
    T, nb, _ = glu_ref.shape
    nc, ng = LRU_CONV - 1, CONF_KERNEL - 1
    xbuf[0:nc] = cs_ref[...]
    xbuf[nc:nc + T] = xg_ref[:, :, 0:D_LRU]
    y = jnp.zeros((T, nb, D_LRU), F32) + cb_ref[...]
    for k in range(LRU_CONV):
        y = y + xbuf[k:k + T] * cw_ref[k:k + 1, :]
    a, drive = _lru_gates(y.reshape(T * nb, D_LRU),
                          (wa_ref[...], ba_ref[...], wx_ref[...], bx_ref[...]), lam_ref[...])
    a = a.reshape(T, nb, D_LRU)
    drive = drive.reshape(T, nb, D_LRU)
    h = h0_ref[...]
    for t in range(T):
        h = a[t] * h + drive[t]
        a_ref[t] = h * jax.nn.gelu(xg_ref[t, :, D_LRU:2 * D_LRU])
    hl_ref[...] = h
    ncs_ref[...] = xbuf[T:T + nc]

    gbuf[0:ng] = gs_ref[...]
    gbuf[ng:ng + T] = glu_ref[...]
    z = jnp.zeros((T, nb, D_CONF), F32) + db_ref[...]
    for k in range(CONF_KERNEL):
        z = z + gbuf[k:k + T] * dw_ref[k:k + 1, :]
    b_ref[...] = jax.nn.silu(_layernorm(z, lng_ref[...], lnb_ref[...]))
    ngs_ref[...] = gbuf[T:T + ng]


def _seq_sample(xg_t, glu_t, cs_t, gs_t, h0, P):
    T, nb, _ = glu_t.shape
    nc, ng = LRU_CONV - 1, CONF_KERNEL - 1
    full = lambda a: pl.BlockSpec(a.shape, lambda i: tuple(0 for _ in a.shape))
    args = (xg_t, glu_t, cs_t, gs_t, h0, P['lru_conv_w'], P['lru_conv_b'], P['lru_wa_bd'], P['lru_ba'],
            P['lru_wx_bd'], P['lru_bx'], P['lru_lam'], P['conf_dw_w'], P['conf_dw_b'], P['conf_ln_g'], P['conf_ln_b'])
    outs = [jax.ShapeDtypeStruct((T, nb, 256), F32), jax.ShapeDtypeStruct((T, nb, 256), F32),
            jax.ShapeDtypeStruct((nb, 256), F32), jax.ShapeDtypeStruct((nc, nb, 256), F32),
            jax.ShapeDtypeStruct((ng, nb, 256), F32)]
    return pl.pallas_call(
        _seq_sample_kernel,
        grid=(1,),
        in_specs=[full(a) for a in args],
        out_specs=[full(o) for o in outs],
        out_shape=outs,
        scratch_shapes=[pltpu.VMEM((T + nc, nb, 256), F32), pltpu.VMEM((T + ng, nb, 256), F32)],
        compiler_params=_cparams(("arbitrary",)),
        name="seq_sample",
    )(*args)


def _softmax_stages(vt_ref, m_ref, acc_ref):
    def soft(s_ref, p_ref, a_ref):
        s = s_ref[...]
        m_prev = m_ref[...]
        m_new = jnp.maximum(m_prev, jnp.max(s, axis=0, keepdims=True))
        p_ref[...] = jnp.exp2(s - m_new).astype(BF16)
        a_ref[...] = jnp.exp2(m_prev - m_new)
        m_ref[...] = m_new

    def pv(p_ref, a_ref, tile):
        acc_ref[...] = a_ref[...] * acc_ref[...] + jnp.dot(vt_ref[tile], p_ref[...], preferred_element_type=F32)

    return soft, pv


def _softmax_pv(s, v_t):
    p = jnp.exp2(s - jnp.max(s, axis=0, keepdims=True))
    acc = jnp.dot(v_t, p.astype(BF16), preferred_element_type=F32)
    dv = acc.shape[0] - VSUM_ROWS
    return acc[0:dv] / jnp.maximum(acc[dv:dv + 1], 1e-30)


def _init_stats(m_ref, acc_ref, l_ref=None):
    m_ref[...] = jnp.full(m_ref.shape, NEG, F32)
    acc_ref[...] = jnp.zeros(acc_ref.shape, F32)
    if l_ref is not None:
        l_ref[...] = jnp.zeros(l_ref.shape, F32)


def _finish_t(acc_ref):
    dv = acc_ref.shape[0] - VSUM_ROWS
    return acc_ref[0:dv, :] / jnp.maximum(acc_ref[dv:dv + 1, :], 1e-30)


def _finish(l_ref, acc_ref):
    return acc_ref[...] / jnp.maximum(l_ref[...], 1e-30)


def _compressed_kv(p_ref, pe_ref, nc):
    ncp = p_ref.shape[0]
    top = p_ref[:, 0:128]
    bot = pltpu.roll(p_ref[:, 128:256], ncp - 1, 0)
    n = lax.broadcasted_iota(jnp.int32, (ncp, 128), 0)
    return jnp.where(n < nc, top + bot + pe_ref[0:1, :], 0.0)


def _compressed_branch(q, kcv, q_pos, nc):
    ncp = kcv.shape[0]
    s = _dot_nt(kcv[:, 0:64], q)
    n = lax.broadcasted_iota(jnp.int32, s.shape, 0)
    mask = (n * NSA_CMP_STRIDE + (NSA_CMP_BLOCK - 1) <= q_pos) & (n < nc)
    s = jnp.where(mask, s, NEG)
    e = jnp.where(mask, jnp.exp2(s - jnp.max(s, axis=0, keepdims=True)), 0.0)
    p = e / jnp.maximum(jnp.sum(e, axis=0, keepdims=True), 1e-30)
    o = _dot_tn(kcv[:, 64:128], p)
    return o, p


def _select_blocks(imp, q_pos, n_sel, q_pos_min=None):
    assert n_sel >= 3 and NSA_FORCE > 2 * NSA_HEADS
    blk = lax.broadcasted_iota(jnp.int32, imp.shape, 0)
    cur = q_pos // NSA_SEL_BLOCK
    valid = blk * NSA_SEL_BLOCK <= q_pos
    forced = (blk == 0) | (blk == cur) | (blk == cur - 1)
    n_forced = jnp.minimum(cur + 1, 3)
    bias = jnp.where(forced, 0.0, NEG)
    score = jnp.where(forced, -3e38, jnp.where(valid, imp, -1.0))
    big = jnp.int32(2 ** 30)
    always_three = q_pos_min is not None and q_pos_min // NSA_SEL_BLOCK >= 2
    for e in range(n_sel - (3 if always_three else 1)):
        mx = jnp.max(score, axis=0, keepdims=True)
        first = jnp.min(jnp.where(score == mx, blk, big), axis=0, keepdims=True)
        hit = blk == first
        if e >= n_sel - 3:
            hit = hit & (n_forced + e < n_sel)
        bias = jnp.where(hit, 0.0, bias)
        score = jnp.where(hit, -3e38, score)
    return bias


def _cmp_prompt_kernel(x_ref, w_ref, p_ref):
    n_chunk = p_ref.shape[0]
    acc = jnp.zeros(p_ref.shape, F32)
    for r in range(NSA_CMP_STRIDE):
        acc = acc + _bdot(x_ref[pl.ds(r, n_chunk, stride=NSA_CMP_STRIDE), :], w_ref[r])
    p_ref[...] = acc


def _cmp_prompt(nsa4, w4, nb):
    N = nsa4.shape[0]
    S = N // nb
    n_chunk = S // NSA_CMP_STRIDE
    return pl.pallas_call(
        _cmp_prompt_kernel,
        grid=(nb,),
        in_specs=[pl.BlockSpec((S, 128), lambda b: (b, 0)),
                  pl.BlockSpec((NSA_CMP_STRIDE, 128, 256), lambda b: (0, 0, 0))],
        out_specs=pl.BlockSpec((None, n_chunk, 256), lambda b: (b, 0, 0)),
        out_shape=jax.ShapeDtypeStruct((nb, n_chunk, 256), F32),
        compiler_params=_cparams(("parallel",)),
        name="cmp_prompt",
    )(nsa4, w4)


def _attn_prompt_kernel(qm_ref, km_ref, vmt_ref, wuvt_ref, q_ref, g_ref, p_ref, pe_ref, ov_ref, ks_ref, vst_ref,
                        kw_ref, vwt_ref, c_ref, o_ref,
                        m_ref, acc_ref, mm_ref, accm_ref, ow_ref, *stage, nc, n_sel):
    stage_refs = [stage[3 * n:3 * n + 3] for n in range(4)]
    i = pl.program_id(1)
    _, tq, _ = q_ref.shape
    tk = ks_ref.shape[1]
    tw = kw_ref.shape[1]
    R = NSA_HEADS * tq
    q_f32 = q_ref[...].reshape(R, NSA_HD)
    q = q_f32.astype(BF16)
    qm = qm_ref[...].reshape(R, MLA_CACHE_W).astype(BF16)
    q_pos_q = i * tq + lax.broadcasted_iota(jnp.int32, (1, tq), 1)
    q_pos = jnp.concatenate([q_pos_q] * NSA_HEADS, axis=1)

    n_wt = NSA_WINDOW // tw
    kw = jnp.concatenate([kw_ref[i + c] for c in range(n_wt + 1)], axis=0)
    vw_t = jnp.concatenate([vwt_ref[i + c] for c in range(n_wt + 1)], axis=1)
    k_pos = (i - n_wt) * tw + lax.broadcasted_iota(jnp.int32, (kw.shape[0], R), 0)
    ok = (k_pos >= 0) & (k_pos <= q_pos) & (k_pos > q_pos - NSA_WINDOW)
    ow_ref[...] = _softmax_pv(jnp.where(ok, _dot_nt(kw, q), NEG), vw_t)

    kcv = _compressed_kv(p_ref, pe_ref, nc)
    o_c, p_c = _compressed_branch(q, kcv, q_pos, nc)
    imp4 = jnp.dot(ov_ref[...], p_c.astype(BF16), preferred_element_type=F32)
    imp = imp4[:, 0:tq]
    for hd in range(1, NSA_HEADS):
        imp = imp + imp4[:, hd * tq:(hd + 1) * tq]
    bias = _select_blocks(imp, q_pos_q, n_sel).T
    q_sel = jnp.concatenate([jnp.concatenate([bias] * NSA_HEADS, axis=0), q_f32], axis=1).astype(BF16)

    n_full = (i * tq) // tk
    _init_stats(m_ref, acc_ref)
    _init_stats(mm_ref, accm_ref)

    def score_m(j):
        return _dot_nt(km_ref[j], qm)

    def score_s(j):
        return _dot_nt(ks_ref[j], q_sel)

    def full_tile(j):
        return jnp.clip(j, 0, jnp.maximum(n_full - 1, 0))

    soft_m, pv_m = _softmax_stages(vmt_ref, mm_ref, accm_ref)
    soft_s, pv_s = _softmax_stages(vst_ref, m_ref, acc_ref)
    (sma, pma, ama), (smb, pmb, amb), (ssa, psa, asa), (ssb, psb, asb) = stage_refs

    causal = n_full * tk + lax.broadcasted_iota(jnp.int32, (tk, R), 0) <= q_pos
    sma[...] = jnp.where(causal, score_m(n_full), NEG)
    ssa[...] = jnp.where(causal, score_s(n_full), NEG)
    smb[...] = score_m(full_tile(0))
    ssb[...] = score_s(full_tile(0))
    soft_m(sma, pma, ama)
    soft_s(ssa, psa, asa)

    def pair(jj, c):
        k = 2 * jj + 1
        prev = jnp.where(jj == 0, n_full, k - 2)
        sma[...] = score_m(full_tile(k))
        ssa[...] = score_s(full_tile(k))
        soft_m(smb, pmb, amb)
        soft_s(ssb, psb, asb)
        pv_m(pma, ama, prev)
        pv_s(psa, asa, prev)
        smb[...] = score_m(full_tile(k + 1))
        ssb[...] = score_s(full_tile(k + 1))
        soft_m(sma, pma, ama)
        soft_s(ssa, psa, asa)
        pv_m(pmb, amb, k - 1)
        pv_s(psb, asb, k - 1)
        return c

    n_pair = n_full // 2
    lax.fori_loop(0, n_pair, pair, 0)
    last = jnp.where(n_pair == 0, n_full, 2 * n_pair - 1)
    pv_m(pma, ama, last)
    pv_s(psa, asa, last)

    @pl.when(n_full % 2 == 1)
    def _():
        soft_m(smb, pmb, amb)
        soft_s(ssb, psb, asb)
        pv_m(pmb, amb, n_full - 1)
        pv_s(psb, asb, n_full - 1)

    o_s = _finish_t(acc_ref)
    o_m = _finish_t(accm_ref)
    c_t = jnp.concatenate([_bdot(wuvt_ref[hd], o_m[:, hd * tq:(hd + 1) * tq]) for hd in range(MLA_HEADS)], axis=0)
    c_ref[...] = c_t.T
    o_w = ow_ref[...]

    g_t = g_ref[...].T
    outs = []
    for hd in range(NSA_HEADS):
        sl = slice(hd * tq, (hd + 1) * tq)
        r0 = MISC_NG + 3 * hd
        outs.append(g_t[r0:r0 + 1, :] * o_c[:, sl] + g_t[r0 + 1:r0 + 2, :] * o_s[:, sl]
                    + g_t[r0 + 2:r0 + 3, :] * o_w[:, sl])
    o_ref[...] = jnp.concatenate(outs, axis=0).T


def _attn_prompt(qm, km3, vmt3, wuvt, q, gates, P, pe, ov, ks3, vst3, kw3, vwt3, *, tq, nc, n_sel):
    nb, nk, tk, dks = ks3.shape
    nw, tw = kw3.shape[1:3]
    assert tw == tq
    N = q.shape[1]
    nq = N // nb // tq
    ncp = P.shape[1]
    nblk = ov.shape[0]
    R = 4 * tq
    out = pl.BlockSpec((tq, 256), lambda b, i: (b * nq + i, 0))
    whole = lambda a: pl.BlockSpec((None,) + a.shape[1:], lambda b, i: (b,) + (0,) * (a.ndim - 1))
    return pl.pallas_call(
        functools.partial(_attn_prompt_kernel, nc=nc, n_sel=n_sel),
        grid=(nb, nq),
        in_specs=[pl.BlockSpec((4, tq, MLA_CACHE_W), lambda b, i: (0, b * nq + i, 0)),
                  whole(km3), whole(vmt3),
                  pl.BlockSpec((4, MLA_V, MLA_KV_LORA), lambda b, i: (0, 0, 0)),
                  pl.BlockSpec((4, tq, NSA_HD), lambda b, i: (0, b * nq + i, 0)),
                  pl.BlockSpec((tq, 128), lambda b, i: (b * nq + i, 0)),
                  pl.BlockSpec((None, ncp, 256), lambda b, i: (b, 0, 0)),
                  pl.BlockSpec((8, 128), lambda b, i: (0, 0)),
                  pl.BlockSpec((nblk, ncp), lambda b, i: (0, 0)),
                  whole(ks3), whole(vst3), whole(kw3), whole(vwt3)],
        out_specs=[out, out],
        out_shape=[jax.ShapeDtypeStruct((N, 256), F32), jax.ShapeDtypeStruct((N, 256), F32)],
        scratch_shapes=[pltpu.VMEM((1, R), F32), pltpu.VMEM((NSA_HD + VSUM_ROWS, R), F32),
                        pltpu.VMEM((1, R), F32), pltpu.VMEM((MLA_KV_LORA + VSUM_ROWS, R), F32)]
        + [pltpu.VMEM((NSA_HD, R), F32)]
        + [pltpu.VMEM((tk, R), F32), pltpu.VMEM((tk, R), BF16), pltpu.VMEM((1, R), F32)] * 4,
        compiler_params=_cparams(("arbitrary", "arbitrary")),
        name="attn_prompt",
    )(qm, km3, vmt3, wuvt, q, gates, P, pe, ov, ks3, vst3, kw3, vwt3)


def _sample_queries(q_ref):
    nh, T, d = q_ref.shape
    q = q_ref[...].reshape(nh * T, d)
    return jnp.concatenate([q, jnp.zeros((SAMPLE_R - nh * T, d), F32)], axis=0).astype(BF16)


def _rows_attend(parts):
    stats = []
    for s, pv in parts:
        m_i = jnp.max(s, axis=1, keepdims=True)
        p = jnp.exp2(s - m_i)
        stats.append((m_i, jnp.sum(p, axis=1, keepdims=True), pv(p.astype(BF16))))
    m = functools.reduce(jnp.maximum, [m_i for m_i, _, _ in stats])
    num, den = 0.0, 0.0
    for m_i, l_i, o_i in stats:
        w = jnp.exp2(m_i - m)
        den = den + w * l_i
        num = num + w * o_i
    return num / jnp.maximum(den, 1e-30)


def _page_groups(n_pages):
    size = max(n_pages // KEY_GROUPS, 1)
    return [range(a, min(a + size, n_pages)) for a in range(0, n_pages, size)]


def _page_copy(cache_ref, pt_ref, buf_ref, sem_ref, layer, n_pages, b, slot, j):
    return pltpu.make_async_copy(cache_ref.at[layer, pt_ref[b * n_pages + j]], buf_ref.at[slot, j], sem_ref.at[slot])


def _fetch_pages(cache_ref, pt_ref, buf_ref, sem_ref, layer, n_pages):
    b = pl.program_id(0)
    slot = b % 2
    copy = functools.partial(_page_copy, cache_ref, pt_ref, buf_ref, sem_ref, layer, n_pages)

    @pl.when(b == 0)
    def _():
        for j in range(n_pages):
            copy(b, slot, j).start()

    @pl.when(b + 1 < pl.num_programs(0))
    def _():
        for j in range(n_pages):
            copy(b + 1, 1 - slot, j).start()

    for j in range(n_pages):
        copy(b, slot, j).wait()
    return slot


def _mla_sample_kernel(pt_ref, cache_ref, q_ref, new_ref, wuv_ref, o_ref, buf_ref, sem_ref, *, layer, n_pages):
    slot = _fetch_pages(cache_ref, pt_ref, buf_ref, sem_ref, layer, n_pages)
    _, T, dk = q_ref.shape
    R = MLA_HEADS * T
    q = q_ref[...].reshape(R, dk).astype(BF16)
    parts = []
    for pages in _page_groups(n_pages):
        kt = jnp.concatenate([buf_ref[slot, j].astype(BF16) for j in pages], axis=1)
        parts.append((jnp.dot(q, kt, preferred_element_type=F32),
                      lambda p, vt=kt[0:MLA_KV_LORA]: _dot_nt(p, vt)))
    new = new_ref[...].astype(BF16)
    k_t = lax.broadcasted_iota(jnp.int32, (R, T), 1)
    q_t = lax.broadcasted_iota(jnp.int32, (R, T), 0) % T
    parts.append((jnp.where(k_t <= q_t, _dot_nt(q, new), NEG),
                  lambda p: jnp.dot(p, new[:, 0:MLA_KV_LORA], preferred_element_type=F32)))
    o = _rows_attend(parts)
    o_ref[...] = jnp.concatenate([_bdot(o[hd * T:(hd + 1) * T, :], wuv_ref[hd]) for hd in range(MLA_HEADS)], axis=1)


def _mla_sample(cache_mla_t, pt_flat, q, rows, wuv, layer, nb, n_pages):
    T = q.shape[1] // nb
    N = q.shape[1]
    page = cache_mla_t.shape[3]
    grid_spec = pltpu.PrefetchScalarGridSpec(
        num_scalar_prefetch=1,
        grid=(nb,),
        in_specs=[pl.BlockSpec(memory_space=pl.ANY),
                  pl.BlockSpec((4, T, MLA_CACHE_W), lambda b, pt: (0, b, 0)),
                  pl.BlockSpec((T, MLA_CACHE_W), lambda b, pt: (b, 0)),
                  pl.BlockSpec((4, MLA_KV_LORA, MLA_V), lambda b, pt: (0, 0, 0))],
        out_specs=pl.BlockSpec((T, 256), lambda b, pt: (b, 0)),
        scratch_shapes=[pltpu.VMEM((2, n_pages, MLA_CACHE_W, page), F32), pltpu.SemaphoreType.DMA((2,))],
    )
    return pl.pallas_call(
        functools.partial(_mla_sample_kernel, layer=layer, n_pages=n_pages),
        grid_spec=grid_spec,
        out_shape=jax.ShapeDtypeStruct((N, 256), F32),
        compiler_params=_cparams(("arbitrary",)),
        name="mla_sample",
    )(pt_flat, cache_mla_t, q, rows, wuv)


def _nsa_sample_kernel(pt_ref, cache_ref, q_ref, g_ref, w4_ref, perm_ref, pe_ref, ov_ref, e_ref, new_ref, wc_ref,
                       wn_ref, wnt_ref, o_ref, wo_ref, buf_ref, sem_ref, y_ref, *, layer, n_pages, nc, n_sel, past):
    slot = _fetch_pages(cache_ref, pt_ref, buf_ref, sem_ref, layer, n_pages)
    _, T, _ = q_ref.shape
    R = NSA_HEADS * T
    half = n_pages // 2
    cpp = y_ref.shape[1] // NSA_CMP_STRIDE
    q_f32 = q_ref[...].reshape(R, NSA_HD)
    q = q_f32.astype(BF16)
    q_pos = past + lax.broadcasted_iota(jnp.int32, (R, 1), 0) % T

    for jp in range(half):
        pair = jnp.concatenate([buf_ref[slot, jp, 0:128, :], buf_ref[slot, jp + half, 0:128, :]], axis=0)
        y_ref[jp] = _dot_nt(perm_ref[...], pair)
    slabs = []
    for r in range(NSA_CMP_STRIDE):
        yr = y_ref[:, r * cpp:(r + 1) * cpp, :].reshape(half * cpp, 256)
        slabs.append(jnp.concatenate([yr[:, 0:128], yr[:, 128:256]], axis=0))
    P = _bdot(jnp.concatenate(slabs, axis=1), w4_ref[...].reshape(NSA_CMP_STRIDE * 128, 256))

    lane = lax.broadcasted_iota(jnp.int32, (1, SAMPLE_R), 1)
    q_pos_l = past + lane % T
    kcv = _compressed_kv(P, pe_ref, nc)
    o_c, p_c = _compressed_branch(_sample_queries(q_ref), kcv, q_pos_l, nc)
    o_c = o_c.T[0:R]
    imp = jnp.dot(ov_ref[...], p_c.astype(BF16), preferred_element_type=F32)
    imp = jnp.where(lane < R, imp, 0.0)
    tot = imp
    for k in range(1, NSA_HEADS):
        tot = tot + pltpu.roll(imp, k * T, 1) + pltpu.roll(imp, SAMPLE_R - R + k * T, 1)
    n_real = -(-(-(-(past + T) // NSA_SEL_BLOCK)) // 8) * 8
    bias = jnp.concatenate([_select_blocks(tot[0:n_real], q_pos_l, n_sel, q_pos_min=past),
                            jnp.full((tot.shape[0] - n_real, SAMPLE_R), NEG, F32)], axis=0).T[0:R]

    n_blk_past = e_ref.shape[0]
    q_sel = jnp.concatenate([bias[:, 0:n_blk_past], q_f32], axis=1).astype(BF16)
    page = buf_ref.shape[3]
    parts = []
    for pages in _page_groups(n_pages):
        kvt = jnp.concatenate([buf_ref[slot, j, 128:256, :].astype(BF16) for j in pages], axis=1)
        k_sel = jnp.concatenate([e_ref[:, pages[0] * page:(pages[-1] + 1) * page], kvt[0:64]], axis=0)
        parts.append((jnp.dot(q_sel, k_sel, preferred_element_type=F32), lambda p, vt=kvt[64:128]: _dot_nt(p, vt)))
    new = new_ref[...].astype(BF16)
    k_new = past + lax.broadcasted_iota(jnp.int32, (R, T), 1)
    s_new = _dot_nt(q, new[:, 128:192]) + bias[:, n_blk_past:n_blk_past + 1]
    parts.append((jnp.where(k_new <= q_pos, s_new, NEG),
                  lambda p: jnp.dot(p, new[:, 192:256], preferred_element_type=F32)))
    o_s = _rows_attend(parts)

    wc = wc_ref[...].astype(BF16)
    n_buf = wc.shape[1]
    k_buf = past - n_buf + lax.broadcasted_iota(jnp.int32, (R, n_buf), 1)
    ok_buf = (k_buf >= 0) & (k_buf <= q_pos) & (k_buf > q_pos - NSA_WINDOW)
    wn = wn_ref[...].astype(BF16)
    ok_new = (k_new <= q_pos) & (k_new > q_pos - NSA_WINDOW)
    o_w = _rows_attend([
        (jnp.where(ok_buf, jnp.dot(q, wc[0:64], preferred_element_type=F32), NEG), lambda p: _dot_nt(p, wc[64:128])),
        (jnp.where(ok_new, _dot_nt(q, wn[:, 0:64]), NEG),
         lambda p: jnp.dot(p, wn[:, 64:128], preferred_element_type=F32))])

    g = g_ref[...]
    outs = []
    for hd in range(NSA_HEADS):
        sl = slice(hd * T, (hd + 1) * T)
        r0 = MISC_NG + 3 * hd
        outs.append(g[:, r0:r0 + 1] * o_c[sl] + g[:, r0 + 1:r0 + 2] * o_s[sl] + g[:, r0 + 2:r0 + 3] * o_w[sl])
    o_ref[...] = jnp.concatenate(outs, axis=1)

    n_win = wo_ref.shape[1]
    keep = n_win - T
    wo_ref[:, 0:keep] = wc_ref[:, n_buf - keep:n_buf]
    wo_ref[:, keep:n_win] = wnt_ref[...]


def _nsa_sample(cache_nsa_t, pt_flat, q, gates, w4, pe, ov, new4, win_cache_t, win_new, layer, nb, n_pages,
                *, nc, n_sel, past):
    N = q.shape[1]
    T = N // nb
    nblk, ncp = ov.shape
    n_buf = win_cache_t.shape[3]
    page = cache_nsa_t.shape[3]
    cpp = page // NSA_CMP_STRIDE
    assert n_pages % 2 == 0 and cpp == 8 and ncp == n_pages * cpp
    n_win = min(NSA_WINDOW, n_buf + T)
    win_new_t = jnp.swapaxes(win_new.reshape(nb, T, 128), 1, 2)
    n_blk_past = n_pages * page // NSA_SEL_BLOCK
    key_blk = np.arange(n_pages * page)[None, :] // NSA_SEL_BLOCK
    onehot = jnp.asarray((key_blk == np.arange(n_blk_past)[:, None]).astype(np.float32), dtype=BF16)
    row = np.arange(page)
    perm = jnp.asarray((np.arange(page)[None, :] == ((row % cpp) * NSA_CMP_STRIDE + row // cpp)[:, None])
                       .astype(np.float32), dtype=BF16)
    const = lambda a: pl.BlockSpec(a.shape, lambda b, pt: (0,) * a.ndim)
    grid_spec = pltpu.PrefetchScalarGridSpec(
        num_scalar_prefetch=1,
        grid=(nb,),
        in_specs=[pl.BlockSpec(memory_space=pl.ANY),
                  pl.BlockSpec((4, T, NSA_HD), lambda b, pt: (0, b, 0)),
                  pl.BlockSpec((T, 128), lambda b, pt: (b, 0)),
                  const(w4), const(perm), const(pe), const(ov), const(onehot),
                  pl.BlockSpec((T, 256), lambda b, pt: (b, 0)),
                  pl.BlockSpec((None, None, 128, n_buf), lambda b, pt: (layer, b, 0, 0)),
                  pl.BlockSpec((T, 128), lambda b, pt: (b, 0)),
                  pl.BlockSpec((None, 128, T), lambda b, pt: (b, 0, 0))],
        out_specs=[pl.BlockSpec((T, 256), lambda b, pt: (b, 0)),
                   pl.BlockSpec((None, 128, n_win), lambda b, pt: (b, 0, 0))],
        scratch_shapes=[pltpu.VMEM((2, n_pages, 4 * NSA_HD, page), F32), pltpu.SemaphoreType.DMA((2,)),
                        pltpu.VMEM((n_pages // 2, page, 256), F32)],
    )
    return pl.pallas_call(
        functools.partial(_nsa_sample_kernel, layer=layer, n_pages=n_pages, nc=nc, n_sel=n_sel, past=past),
        grid_spec=grid_spec,
        out_shape=[jax.ShapeDtypeStruct((N, 256), F32), jax.ShapeDtypeStruct((nb, 128, n_win), F32)],
        compiler_params=_cparams(("arbitrary",)),
        name="nsa_sample",
    )(pt_flat, cache_nsa_t, q, gates, w4, perm, pe, ov, onehot, new4, win_cache_t, win_new, win_new_t)


def _block_diag(w):
    n, a, b = w.shape
    out = jnp.zeros((n * a, n * b), w.dtype)
    for j in range(n):
        out = out.at[j * a:(j + 1) * a, j * b:(j + 1) * b].set(w[j])
    return out


def _rot_cols(w):
    half = w.shape[-1] // 2
    return jnp.concatenate([-w[..., half:], w[..., :half]], axis=-1)


def _layer_params(l, W):
    idx = np.cumsum(IN_SPLITS)[:-1].tolist()
    lru_x, lru_g, conf_u, cq, ckv, kr, nq, nkv, ng = jnp.split(W['w_in'][l], idx, axis=-1)
    pad = jnp.zeros((D_MODEL, D_INP - C_MISC - 2 * MLA_ROPE - 3 * NSA_HEADS), F32)
    w_in = jnp.concatenate([lru_x, lru_g, conf_u, cq, ckv, nq, nkv, kr, _rot_cols(kr), ng, pad], axis=-1)
    wuq = W['mla_wuq'][l].reshape(MLA_Q_LORA, MLA_HEADS, MLA_NOPE + MLA_ROPE)
    wuq_rope = wuq[:, :, MLA_NOPE:]
    wuq = jnp.concatenate([wuq[:, :, :MLA_NOPE].reshape(MLA_Q_LORA, -1), wuq_rope.reshape(MLA_Q_LORA, -1),
                           _rot_cols(wuq_rope).reshape(MLA_Q_LORA, -1)], axis=-1)
    wuk = _block_diag(jnp.transpose(W['mla_wuk'][l], (1, 2, 0)))
    wuv = jnp.transpose(W['mla_wuv'][l], (1, 0, 2))
    wk = W['nsa_wc_k'][l].reshape(NSA_CMP_BLOCK, NSA_HD, NSA_HD)
    wv = W['nsa_wc_v'][l].reshape(NSA_CMP_BLOCK, NSA_HD, NSA_HD)
    z = jnp.zeros((NSA_CMP_STRIDE, NSA_HD, NSA_HD), F32)
    w4 = jnp.concatenate([jnp.concatenate([wk[:16], z, wk[16:], z], axis=2),
                          jnp.concatenate([z, wv[:16], z, wv[16:]], axis=2)], axis=1)
    pe = jnp.concatenate([W['nsa_pe_k'][l], W['nsa_pe_v'][l]], axis=-1)
    pe2 = jnp.concatenate([pe[:16, None], pe[16:, None], jnp.zeros((16, 6, 128), F32)], axis=1)
    r1 = lambda a: a.reshape(1, -1)
    return dict(
        w_in=w_in.astype(BF16), g_pre_mix=r1(W['g_pre_mix'][l]), g_post_mix=r1(W['g_post_mix'][l]),
        g_pre_ffn=r1(W['g_pre_ffn'][l]), g_post_ffn=r1(W['g_post_ffn'][l]),
        mla_gq=r1(W['mla_gq'][l]), mla_gkv=r1(W['mla_gkv'][l]),
        wuq=wuq.astype(BF16), wuk=wuk.astype(BF16), wuv=wuv.astype(BF16),
        wuvt=jnp.transpose(wuv, (0, 2, 1)).astype(BF16),
        w4=w4.astype(BF16), pe2=pe2,
        lru_conv_w=W['lru_conv_w'][l], lru_conv_b=r1(W['lru_conv_b'][l]),
        lru_wa_bd=_block_diag(W['lru_wa'][l]).astype(BF16), lru_ba=r1(W['lru_ba'][l]),
        lru_wx_bd=_block_diag(W['lru_wx'][l]).astype(BF16), lru_bx=r1(W['lru_bx'][l]),
        lru_lam=r1(W['lru_lam'][l]),
        conf_dw_w=W['conf_dw_w'][l], conf_dw_b=r1(W['conf_dw_b'][l]),
        conf_ln_g=r1(W['conf_ln_g'][l]), conf_ln_b=r1(W['conf_ln_b'][l]),
        w_out=W['w_out'][l].astype(BF16), w_ff1=W['w_ff1'][l].astype(BF16), w_ff2=W['w_ff2'][l].astype(BF16),
    )


def _rope_tables(pos):
    half = MLA_ROPE // 2
    freq = jnp.power(ROPE_THETA, -jnp.arange(half, dtype=F32) / half)
    ang = pos.astype(F32)[:, None] * freq
    cosv = jnp.tile(jnp.cos(ang), (1, 2 * MLA_HEADS))
    sinv = jnp.tile(jnp.sin(ang), (1, 2 * MLA_HEADS))
    return cosv, sinv


def _overlap_t(nblk_pad, ncp, nc):
    n = np.arange(ncp)[None, :]
    j = np.arange(nblk_pad)[:, None]
    ov = (n * NSA_CMP_STRIDE < j * NSA_SEL_BLOCK + NSA_SEL_BLOCK) & (n * NSA_CMP_STRIDE + NSA_CMP_BLOCK > j * NSA_SEL_BLOCK)
    return jnp.asarray((ov & (n < nc)).astype(np.float32), dtype=BF16)


def _key_tiles(a, nb, t, front=0):
    k = a.reshape(nb, -1, t, a.shape[-1])
    return jnp.pad(k, ((0, 0), (front, 0), (0, 0), (0, 0)))


def _value_tiles_t(a, nb, t, front=0):
    v = jnp.swapaxes(_key_tiles(a, nb, t, front), 2, 3)
    ones = jnp.ones(v.shape[:2] + (1, t), v.dtype)
    return jnp.concatenate([v, ones, jnp.zeros(v.shape[:2] + (VSUM_ROWS - 1, t), v.dtype)], axis=2)


def kernel(x_prompt, x_sample, c_prompt, c_sample, state_lru_h, state_lru_conv, state_conv, cache_mla, cache_nsa,
           cache_nsa_win, page_table, w_mod, b_mod, g_pre_mix, g_post_mix, g_pre_ffn, g_post_ffn, w_in, lru_conv_w,
           lru_conv_b, lru_wa, lru_ba, lru_wx, lru_bx, lru_lam, conf_dw_w, conf_dw_b, conf_ln_g, conf_ln_b, mla_gq,
           mla_wuq, mla_gkv, mla_wuk, mla_wuv, nsa_pe_k, nsa_wc_k, nsa_pe_v, nsa_wc_v, w_out, w_ff1, w_ff2):
    W = dict(w_in=w_in, g_pre_mix=g_pre_mix, g_post_mix=g_post_mix, g_pre_ffn=g_pre_ffn, g_post_ffn=g_post_ffn,
             lru_conv_w=lru_conv_w, lru_conv_b=lru_conv_b, lru_wa=lru_wa, lru_ba=lru_ba, lru_wx=lru_wx, lru_bx=lru_bx,
             lru_lam=lru_lam, conf_dw_w=conf_dw_w, conf_dw_b=conf_dw_b, conf_ln_g=conf_ln_g, conf_ln_b=conf_ln_b,
             mla_gq=mla_gq, mla_wuq=mla_wuq, mla_gkv=mla_gkv, mla_wuk=mla_wuk, mla_wuv=mla_wuv,
             nsa_pe_k=nsa_pe_k, nsa_wc_k=nsa_wc_k, nsa_pe_v=nsa_pe_v, nsa_wc_v=nsa_wc_v,
             w_out=w_out, w_ff1=w_ff1, w_ff2=w_ff2)
    L = w_in.shape[0]
    nbp, S, D = x_prompt.shape
    nbs, T, _ = x_sample.shape
    n_pages = page_table.shape[1]
    page = cache_mla.shape[2]
    past = n_pages * page
    Np, Ns = nbp * S, nbs * T
    tq = min(Q_TILE, S)
    tk = min(K_TILE, S)
    tm_p = min(256, Np)
    tm_s = min(256, Ns)
    tc = min(512, S)

    nc_p = S // NSA_CMP_STRIDE - 1
    nblk_p = -(-S // NSA_SEL_BLOCK)
    ov_p = _overlap_t(nblk_p, S // NSA_CMP_STRIDE, nc_p)
    assert NSA_WINDOW % tq == 0 and S % tk == 0 and tk % tq == 0
    blk_of_pos = jnp.arange(S, dtype=jnp.int32)[:, None] // NSA_SEL_BLOCK
    blk_onehot_p = jnp.tile((blk_of_pos == jnp.arange(nblk_p, dtype=jnp.int32)[None, :]).astype(BF16), (nbp, 1))
    len_s = past + T
    assert len_s // NSA_CMP_STRIDE == past // NSA_CMP_STRIDE and past % NSA_SEL_BLOCK == 0 and T <= NSA_SEL_BLOCK
    nc_s = len_s // NSA_CMP_STRIDE - 1
    nblk_s = -(-len_s // NSA_SEL_BLOCK)
    nblk_s_pad = -(-nblk_s // 128) * 128
    ov_s = _overlap_t(nblk_s_pad, past // NSA_CMP_STRIDE, nc_s)

    rows = nbp + nbs
    rows_pad = -(-rows // 8) * 8
    c_all = jnp.concatenate([c_prompt, c_sample, jnp.zeros((rows_pad - rows, D), F32)], axis=0)
    mod = _modulation(c_all, w_mod.astype(BF16), b_mod)

    params = [_layer_params(l, W) for l in range(L)]
    pe_all = _pe_term(jnp.stack([p['pe2'] for p in params]), jnp.stack([p['w4'] for p in params]))

    cos_p, sin_p = _rope_tables(jnp.arange(S, dtype=jnp.int32))
    cos_s, sin_s = _rope_tables(past + jnp.arange(T, dtype=jnp.int32))
    cos_s = jnp.tile(cos_s, (nbs, 1))
    sin_s = jnp.tile(sin_s, (nbs, 1))
    pt_flat = page_table.reshape(-1).astype(jnp.int32)
    cache_mla_t = jnp.swapaxes(cache_mla, 2, 3)
    cache_nsa_t = jnp.transpose(cache_nsa, (0, 1, 3, 4, 2)).reshape(L, cache_nsa.shape[1], 4 * NSA_HD, page)
    win_cache_t = jnp.transpose(cache_nsa_win, (0, 1, 3, 4, 2)).reshape(L, nbs, 2 * NSA_HD, cache_nsa_win.shape[2])

    xp = x_prompt.reshape(Np, D)
    xs = x_sample.reshape(Ns, D)
    outs_p, outs_s = [], []
    for l in range(L):
        P = params[l]
        pe = pe_all[l]
        mp = [m.reshape(nbp, 1, D) for m in jnp.split(mod[l, :nbp], 6, axis=-1)]
        ms = [jnp.repeat(m, T, axis=0) for m in jnp.split(mod[l, nbp:rows], 6, axis=-1)]

        lru, glu, qm, row, nq, nsa4, win, gates = _in_proj(
            xp, mp[0], mp[1], P['g_pre_mix'], P['w_in'], P['mla_gq'], P['wuq'], P['wuk'], P['mla_gkv'],
            cos_p, sin_p, per_token=False, tm=tm_p)
        a_out, b_out, h_last = _seq_prompt(lru.reshape(nbp, S, 512), glu.reshape(nbp, S, 256), P, tc=tc)
        rowb = row.astype(BF16)
        k3 = _key_tiles(rowb, nbp, tk)
        vt3 = _value_tiles_t(rowb[:, :MLA_KV_LORA], nbp, tk)
        Pp = _cmp_prompt(nsa4, P['w4'], nbp)
        nsab = nsa4.astype(BF16)
        ks3 = _key_tiles(jnp.concatenate([blk_onehot_p, nsab[:, 128:192]], axis=1), nbp, tk)
        vst3 = _value_tiles_t(nsab[:, 192:256], nbp, tk)
        winb = win.astype(BF16)
        kw3 = _key_tiles(winb[:, 0:64], nbp, tq, front=NSA_WINDOW // tq)
        vwt3 = _value_tiles_t(winb[:, 64:128], nbp, tq, front=NSA_WINDOW // tq)
        c_out, d_out = _attn_prompt(qm, k3, vt3, P['wuvt'], nq, gates, Pp, pe, ov_p, ks3, vst3, kw3, vwt3, tq=tq,
                                    nc=nc_p, n_sel=min(NSA_N_SEL, nblk_p))
        xp = _post(xp, a_out.reshape(Np, 256), b_out.reshape(Np, 256), c_out, d_out, mp[2], mp[3], mp[4], mp[5],
                   P['w_out'], P['g_post_mix'], P['g_pre_ffn'], P['g_post_ffn'], P['w_ff1'], P['w_ff2'],
                   per_token=False, tm=tm_p)
        n_win = min(NSA_WINDOW, S)
        outs_p.append((h_last[:, 0], lru.reshape(nbp, S, 512)[:, S - (LRU_CONV - 1):, :D_LRU],
                       glu.reshape(nbp, S, 256)[:, S - (CONF_KERNEL - 1):],
                       row.reshape(nbp, S, MLA_CACHE_W), nsa4.reshape(nbp, S, 4, NSA_HD),
                       win.reshape(nbp, S, 2, NSA_HD)[:, S - n_win:]))

        lru, glu, qm, row, nq, nsa4, win, gates = _in_proj(
            xs, ms[0], ms[1], P['g_pre_mix'], P['w_in'], P['mla_gq'], P['wuq'], P['wuk'], P['mla_gkv'],
            cos_s, sin_s, per_token=True, tm=tm_s)
        tmaj = lambda a: jnp.swapaxes(a, 0, 1)
        a_t, b_t, h_last, ncs, ngs = _seq_sample(
            tmaj(lru.reshape(nbs, T, 512)), tmaj(glu.reshape(nbs, T, 256)), tmaj(state_lru_conv[l]),
            tmaj(state_conv[l]), state_lru_h[l], P)
        c_out = _mla_sample(cache_mla_t, pt_flat, qm, row, P['wuv'], l, nbs, n_pages)
        d_out, win_t = _nsa_sample(cache_nsa_t, pt_flat, nq, gates, P['w4'], pe, ov_s, nsa4, win_cache_t, win, l, nbs,
                                   n_pages, nc=nc_s, n_sel=min(NSA_N_SEL, nblk_s), past=past)
        xs = _post(xs, tmaj(a_t).reshape(Ns, 256), tmaj(b_t).reshape(Ns, 256), c_out, d_out, ms[2], ms[3], ms[4],
                   ms[5], P['w_out'], P['g_post_mix'], P['g_pre_ffn'], P['g_post_ffn'], P['w_ff1'], P['w_ff2'],
                   per_token=True, tm=tm_s)
        win_s = jnp.transpose(win_t.reshape(nbs, 2, NSA_HD, win_t.shape[2]), (0, 3, 1, 2))
        outs_s.append((h_last, tmaj(ncs), tmaj(ngs), row.reshape(nbs, T, MLA_CACHE_W),
                       nsa4.reshape(nbs, T, 4, NSA_HD), win_s))

    stack = lambda outs, i: jnp.stack([o[i] for o in outs])
    return (xp.reshape(nbp, S, D), xs.reshape(nbs, T, D),
            stack(outs_p, 0), stack(outs_s, 0), stack(outs_p, 1), stack(outs_s, 1), stack(outs_p, 2), stack(outs_s, 2),
            stack(outs_p, 3), stack(outs_s, 3), stack(outs_p, 4), stack(outs_s, 4), stack(outs_p, 5), stack(outs_s, 5))
```

```python
import functools

import numpy as np
import jax
import jax.numpy as jnp
from jax import lax
from jax.experimental import pallas as pl
from jax.experimental.pallas import tpu as pltpu

F32 = jnp.float32
BF16 = jnp.bfloat16

D_MODEL = 1024
D_LRU = 256
LRU_BLOCKS = 4
LRU_CONV = 4
LRU_C = 8.0
D_CONF = 256
CONF_KERNEL = 31
MLA_HEADS = 4
MLA_NOPE = 64
MLA_ROPE = 32
MLA_V = 64
MLA_Q_LORA = 256
MLA_KV_LORA = 128
MLA_CACHE_W = MLA_KV_LORA + MLA_ROPE
ROPE_THETA = 10000.0
NSA_HEADS = 4
NSA_HD = 64
NSA_CMP_BLOCK = 32
NSA_CMP_STRIDE = 16
NSA_SEL_BLOCK = 64
NSA_N_SEL = 16
NSA_WINDOW = 512
NSA_FORCE = 1.0e4
D_FF = 4 * D_MODEL
EPS = 1e-6
IN_SPLITS = (D_LRU, D_LRU, 2 * D_CONF, MLA_Q_LORA, MLA_KV_LORA, MLA_ROPE, NSA_HEADS * NSA_HD, 6 * NSA_HD, 3 * NSA_HEADS)

C_LRU = 0
C_CONF = 512
C_CQ = 1024
C_CKV = 1280
C_NQ = 1408
C_NKV = 1664
C_MISC = 2048
D_INP = 2176
MISC_NG = 64

NEG = -1e30
LOG2E = 1.4426950408889634
VSUM_ROWS = 8
VMEM_LIMIT_V7X = 56 * 1024 * 1024
Q_TILE = 256
K_TILE = 512
KEY_GROUPS = 4
SAMPLE_R = 128


def _cparams(sem):
    return pltpu.CompilerParams(dimension_semantics=sem, vmem_limit_bytes=VMEM_LIMIT_V7X)


def _rms(x, g):
    return x * lax.rsqrt(jnp.mean(x * x, axis=-1, keepdims=True) + EPS) * g


def _bdot(a, b):
    return jnp.dot(a.astype(BF16), b.astype(BF16), preferred_element_type=F32)


def _dot_nt(a, b):
    return lax.dot_general(a.astype(BF16), b.astype(BF16), (((1,), (1,)), ((), ())), preferred_element_type=F32)


def _dot_tn(a, b):
    return lax.dot_general(a.astype(BF16), b.astype(BF16), (((0,), (0,)), ((), ())), preferred_element_type=F32)


def _mod_kernel(c_ref, w_ref, b_ref, o_ref):
    o_ref[...] = _bdot(jax.nn.silu(c_ref[...]), w_ref[...]) + b_ref[...]


def _modulation(c_all, w_mod, b_mod):
    L, D, D6 = w_mod.shape
    rows = c_all.shape[0]
    tn = 1536
    return pl.pallas_call(
        _mod_kernel,
        grid=(L, D6 // tn),
        in_specs=[pl.BlockSpec((rows, D), lambda l, j: (0, 0)),
                  pl.BlockSpec((None, D, tn), lambda l, j: (l, 0, j)),
                  pl.BlockSpec((None, 1, tn), lambda l, j: (l, 0, j))],
        out_specs=pl.BlockSpec((None, rows, tn), lambda l, j: (l, 0, j)),
        out_shape=jax.ShapeDtypeStruct((L, rows, D6), F32),
        compiler_params=_cparams(("arbitrary", "arbitrary")),
        name="modulation",
    )(c_all, w_mod, b_mod.reshape(L, 1, D6))


def _pe_kernel(pe_ref, w_ref, o_ref):
    acc = jnp.zeros((8, 256), F32)
    for r in range(NSA_CMP_STRIDE):
        acc = acc + _bdot(pe_ref[r], w_ref[r])
    o_ref[...] = jnp.broadcast_to(acc[0:1, 0:128] + acc[1:2, 128:256], (8, 128))


def _pe_term(pe2, w4):
    L = pe2.shape[0]
    return pl.pallas_call(
        _pe_kernel,
        grid=(L,),
        in_specs=[pl.BlockSpec((None, NSA_CMP_STRIDE, 8, 128), lambda l: (l, 0, 0, 0)),
                  pl.BlockSpec((None, NSA_CMP_STRIDE, 128, 256), lambda l: (l, 0, 0, 0))],
        out_specs=pl.BlockSpec((None, 8, 128), lambda l: (l, 0, 0)),
        out_shape=jax.ShapeDtypeStruct((L, 8, 128), F32),
        compiler_params=_cparams(("arbitrary",)),
        name="pe_term",
    )(pe2, w4)


def _in_proj_kernel(x_ref, sh_ref, sc_ref, g_ref, w_ref, gq_ref, wuq_ref, wuk_ref, gkv_ref, cos_ref, sin_ref,
                    lru_ref, glu_ref, qm_ref, row_ref, nq_ref, nsa4_ref, win_ref, gate_ref):
    h = _rms(x_ref[...], g_ref[...]) * (1.0 + sc_ref[...]) + sh_ref[...]
    u = _bdot(h, w_ref[...])
    lru_ref[...] = u[:, C_LRU:C_LRU + 512]
    glu_ref[...] = u[:, C_CONF:C_CONF + 256] * jax.nn.sigmoid(u[:, C_CONF + 256:C_CONF + 512])
    nsa4_ref[...] = u[:, C_NKV:C_NKV + 256]
    win_ref[...] = u[:, C_NKV + 256:C_NKV + 384]
    misc = u[:, C_MISC:C_MISC + 128]
    gate_ref[...] = jax.nn.sigmoid(misc)
    cosv = cos_ref[...]
    sinv = sin_ref[...]
    qr = _bdot(_rms(u[:, C_CQ:C_CQ + 256], gq_ref[...]), wuq_ref[...])
    q_rope = qr[:, 256:384] * cosv + qr[:, 384:512] * sinv
    q_lat = _bdot(qr[:, 0:256], wuk_ref[...])
    scale = (MLA_NOPE + MLA_ROPE) ** -0.5 * LOG2E
    for hd in range(MLA_HEADS):
        qm_ref[hd, :, 0:128] = q_lat[:, 128 * hd:128 * (hd + 1)] * scale
        qm_ref[hd, :, 128:160] = q_rope[:, 32 * hd:32 * (hd + 1)] * scale
    row_ref[:, 0:128] = _rms(u[:, C_CKV:C_CKV + 128], gkv_ref[...])
    row_ref[:, 128:160] = misc[:, 0:32] * cosv[:, 0:32] + misc[:, 32:64] * sinv[:, 0:32]
    nq = u[:, C_NQ:C_NQ + 256] * (NSA_HD ** -0.5 * LOG2E)
    for hd in range(NSA_HEADS):
        nq_ref[hd] = nq[:, 64 * hd:64 * (hd + 1)]


def _in_proj(x, sh, sc, g, w_in, gq, wuq, wuk, gkv, cosv, sinv, *, per_token, tm):
    N, D = x.shape
    nt = N // tm
    if per_token:
        mod_spec = pl.BlockSpec((tm, D), lambda i: (i, 0))
        tab_spec = pl.BlockSpec((tm, 128), lambda i: (i, 0))
    else:
        per_b = nt // sh.shape[0]
        mod_spec = pl.BlockSpec((None, 1, D), lambda i: (i // per_b, 0, 0))
        tab_spec = pl.BlockSpec((tm, 128), lambda i: (i % per_b, 0))
    const = lambda shape: pl.BlockSpec(shape, lambda i: tuple(0 for _ in shape))
    row = lambda w: pl.BlockSpec((tm, w), lambda i: (i, 0))
    head = lambda w: pl.BlockSpec((4, tm, w), lambda i: (0, i, 0))
    return pl.pallas_call(
        _in_proj_kernel,
        grid=(nt,),
        in_specs=[row(D), mod_spec, mod_spec, const((1, D)), const((D, D_INP)), const((1, 256)),
                  const((256, 512)), const((256, 512)), const((1, 128)), tab_spec, tab_spec],
        out_specs=[row(512), row(256), head(MLA_CACHE_W), row(MLA_CACHE_W), head(NSA_HD), row(256), row(128), row(128)],
        out_shape=[jax.ShapeDtypeStruct((N, 512), F32), jax.ShapeDtypeStruct((N, 256), F32),
                   jax.ShapeDtypeStruct((4, N, MLA_CACHE_W), F32), jax.ShapeDtypeStruct((N, MLA_CACHE_W), F32),
                   jax.ShapeDtypeStruct((4, N, NSA_HD), F32), jax.ShapeDtypeStruct((N, 256), F32),
                   jax.ShapeDtypeStruct((N, 128), F32), jax.ShapeDtypeStruct((N, 128), F32)],
        compiler_params=_cparams(("parallel",)),
        name="in_proj",
    )(x, sh, sc, g, w_in, gq, wuq, wuk, gkv, cosv, sinv)


def _post_kernel(x_ref, a_ref, b_ref, c_ref, d_ref, gt1_ref, sh2_ref, sc2_ref, gt2_ref,
                 wo_ref, gpm_ref, gpf_ref, gqf_ref, w1_ref, w2_ref, o_ref):
    y = _bdot(a_ref[...], wo_ref[0:256, :])
    y = y + _bdot(b_ref[...], wo_ref[256:512, :])
    y = y + _bdot(c_ref[...], wo_ref[512:768, :])
    y = y + _bdot(d_ref[...], wo_ref[768:1024, :])
    x = x_ref[...] + gt1_ref[...] * _rms(y, gpm_ref[...])
    h = (_rms(x, gpf_ref[...]) * (1.0 + sc2_ref[...]) + sh2_ref[...]).astype(BF16)
    f = jnp.zeros(x.shape, F32)
    fc = 1024
    for j in range(D_FF // fc):
        t = jnp.maximum(jnp.dot(h, w1_ref[:, j * fc:(j + 1) * fc], preferred_element_type=F32), 0.0)
        f = f + _bdot(t * t, w2_ref[j * fc:(j + 1) * fc, :])
    o_ref[...] = x + gt2_ref[...] * _rms(f, gqf_ref[...])


def _post(x, a, b, c, d, gt1, sh2, sc2, gt2, wo, gpm, gpf, gqf, w1, w2, *, per_token, tm):
    N, D = x.shape
    nt = N // tm
    if per_token:
        mod_spec = pl.BlockSpec((tm, D), lambda i: (i, 0))
    else:
        per_b = nt // gt1.shape[0]
        mod_spec = pl.BlockSpec((None, 1, D), lambda i: (i // per_b, 0, 0))
    const = lambda shape: pl.BlockSpec(shape, lambda i: tuple(0 for _ in shape))
    row = lambda w: pl.BlockSpec((tm, w), lambda i: (i, 0))
    return pl.pallas_call(
        _post_kernel,
        grid=(nt,),
        in_specs=[row(D), row(256), row(256), row(256), row(256), mod_spec, mod_spec, mod_spec, mod_spec,
                  const((D, D)), const((1, D)), const((1, D)), const((1, D)), const((D, D_FF)), const((D_FF, D))],
        out_specs=row(D),
        out_shape=jax.ShapeDtypeStruct((N, D), F32),
        compiler_params=_cparams(("parallel",)),
        name="post",
    )(x, a, b, c, d, gt1, sh2, sc2, gt2, wo, gpm, gpf, gqf, w1, w2)


def _lru_gates(y, gate_w, lam):
    wa, ba, wx, bx = gate_w
    r = jax.nn.sigmoid(_bdot(y, wa) + ba)
    i = jax.nn.sigmoid(_bdot(y, wx) + bx)
    log_a = -LRU_C * r * jax.nn.softplus(-lam)
    return jnp.exp(log_a), jnp.sqrt(1.0 - jnp.exp(2.0 * log_a)) * i * y


def _layernorm(y, g, b):
    mu = jnp.mean(y, axis=-1, keepdims=True)
    var = jnp.mean(jnp.square(y - mu), axis=-1, keepdims=True)
    return (y - mu) * lax.rsqrt(var + EPS) * g + b


def _seq_prompt_kernel(xg_ref, glu_ref, cw_ref, cb_ref, wa_ref, ba_ref, wx_ref, bx_ref, lam_ref,
                       dw_ref, db_ref, lng_ref, lnb_ref,
                       a_ref, b_ref, hl_ref, xbuf, gbuf, abuf, dbuf, hbuf, sbuf):
    i = pl.program_id(0)
    nb, tc, _ = glu_ref.shape

    @pl.when(i == 0)
    def _():
        xbuf[:, 0:8, :] = jnp.zeros((nb, 8, D_LRU), F32)
        gbuf[:, 0:32, :] = jnp.zeros((nb, 32, D_CONF), F32)
        hbuf[...] = jnp.zeros(hbuf.shape, F32)

    xbuf[:, 8:8 + tc, :] = xg_ref[:, :, 0:D_LRU]
    y = jnp.zeros((nb, tc, D_LRU), F32) + cb_ref[...]
    for k in range(LRU_CONV):
        y = y + xbuf[:, 8 - (LRU_CONV - 1) + k:8 - (LRU_CONV - 1) + k + tc, :] * cw_ref[k:k + 1, :]
    a, drive = _lru_gates(y.reshape(nb * tc, D_LRU),
                          (wa_ref[...], ba_ref[...], wx_ref[...], bx_ref[...]), lam_ref[...])
    abuf[...] = a.reshape(nb, tc, D_LRU)
    dbuf[...] = drive.reshape(nb, tc, D_LRU)

    row = lax.broadcasted_iota(jnp.int32, (nb * 8, D_LRU), 0) % 8

    def tile_step(j, h):
        o = pl.multiple_of(j * 8, 8)
        a8 = abuf[:, pl.ds(o, 8), :].reshape(nb * 8, D_LRU)
        d8 = dbuf[:, pl.ds(o, 8), :].reshape(nb * 8, D_LRU)
        for s in (1, 2, 4):
            d8 = a8 * jnp.where(row >= s, pltpu.roll(d8, s, 0), 0.0) + d8
            a8 = a8 * jnp.where(row >= s, pltpu.roll(a8, s, 0), 1.0)
        hs = a8.reshape(nb, 8, D_LRU) * h + d8.reshape(nb, 8, D_LRU)
        dbuf[:, pl.ds(o, 8), :] = hs
        return hs[:, 7:8, :]

    h = lax.fori_loop(0, tc // 8, tile_step, hbuf[...], unroll=2)
    hbuf[...] = h
    hl_ref[...] = jnp.broadcast_to(h, hl_ref.shape)
    a_ref[...] = dbuf[...] * jax.nn.gelu(xg_ref[:, :, D_LRU:2 * D_LRU])
    xbuf[:, 0:8, :] = xbuf[:, tc:tc + 8, :]

    gbuf[:, 32:32 + tc, :] = glu_ref[...]
    z = jnp.zeros((nb, tc, D_CONF), F32) + db_ref[...]
    first = 32 - (CONF_KERNEL - 1)
    for r in range(8):
        taps = [k for k in range(CONF_KERNEL) if (first + k) % 8 == r]
        rows = first + taps[-1] + tc - r
        sbuf[:, 0:rows, :] = gbuf[:, r:r + rows, :]
        for k in taps:
            a = first + k - r
            z = z + sbuf[:, a:a + tc, :] * dw_ref[k:k + 1, :]
    b_ref[...] = jax.nn.silu(_layernorm(z, lng_ref[...], lnb_ref[...]))
    gbuf[:, 0:32, :] = gbuf[:, tc:tc + 32, :]


def _seq_prompt(xg, glu, P, *, tc):
    nb, S, _ = glu.shape
    const = lambda shape: pl.BlockSpec(shape, lambda i: tuple(0 for _ in shape))
    seq = lambda w: pl.BlockSpec((nb, tc, w), lambda i: (0, i, 0))
    return pl.pallas_call(
        _seq_prompt_kernel,
        grid=(S // tc,),
        in_specs=[seq(512), seq(256), const((LRU_CONV, 256)), const((1, 256)), const((256, 256)), const((1, 256)),
                  const((256, 256)), const((1, 256)), const((1, 256)), const((CONF_KERNEL, 256)), const((1, 256)),
                  const((1, 256)), const((1, 256))],
        out_specs=[seq(256), seq(256), const((nb, 8, 256))],
        out_shape=[jax.ShapeDtypeStruct((nb, S, 256), F32), jax.ShapeDtypeStruct((nb, S, 256), F32),
                   jax.ShapeDtypeStruct((nb, 8, 256), F32)],
        scratch_shapes=[pltpu.VMEM((nb, tc + 8, 256), F32), pltpu.VMEM((nb, tc + 32, 256), F32),
                        pltpu.VMEM((nb, tc, 256), F32), pltpu.VMEM((nb, tc, 256), F32),
                        pltpu.VMEM((nb, 1, 256), F32), pltpu.VMEM((nb, tc + 32, 256), F32)],
        compiler_params=_cparams(("arbitrary",)),
        name="seq_prompt",
    )(xg, glu, P['lru_conv_w'], P['lru_conv_b'], P['lru_wa_bd'], P['lru_ba'], P['lru_wx_bd'], P['lru_bx'],
      P['lru_lam'], P['conf_dw_w'], P['conf_dw_b'], P['conf_ln_g'], P['conf_ln_b'])


def _seq_sample_kernel(xg_ref, glu_ref, cs_ref, gs_ref, h0_ref, cw_ref, cb_ref, wa_ref, ba_ref, wx_ref, bx_ref,
                       lam_ref, dw_ref, db_ref, lng_ref, lnb_ref,
                       a_ref, b_ref, hl_ref, ncs_ref, ngs_ref, xbuf, gbuf):
    T, nb, _ = glu_ref.shape
    nc, ng = LRU_CONV - 1, CONF_KERNEL - 1
    xbuf[0:nc] = cs_ref[...]
    xbuf[nc:nc + T] = xg_ref[:, :, 0:D_LRU]
    y = jnp.zeros((T, nb, D_LRU), F32) + cb_ref[...]
    for k in range(LRU_CONV):
        y = y + xbuf[k:k + T] * cw_ref[k:k + 1, :]
    a, drive = _lru_gates(y.reshape(T * nb, D_LRU),
                          (wa_ref[...], ba_ref[...], wx_ref[...], bx_ref[...]), lam_ref[...])
    a = a.reshape(T, nb, D_LRU)
    drive = drive.reshape(T, nb, D_LRU)
    h = h0_ref[...]
    for t in range(T):
        h = a[t] * h + drive[t]
        a_ref[t] = h * jax.nn.gelu(xg_ref[t, :, D_LRU:2 * D_LRU])
    hl_ref[...] = h
    ncs_ref[...] = xbuf[T:T + nc]

    gbuf[0:ng] = gs_ref[...]
    gbuf[ng:ng + T] = glu_ref[...]
    z = jnp.zeros((T, nb, D_CONF), F32) + db_ref[...]
    for k in range(CONF_KERNEL):
        z = z + gbuf[k:k + T] * dw_ref[k:k + 1, :]
    b_ref[...] = jax.nn.silu(_layernorm(z, lng_ref[...], lnb_ref[...]))
    ngs_ref[...] = gbuf[T:T + ng]


def _seq_sample(xg_t, glu_t, cs_t, gs_t, h0, P):
    T, nb, _ = glu_t.shape
    nc, ng = LRU_CONV - 1, CONF_KERNEL - 1
    full = lambda a: pl.BlockSpec(a.shape, lambda i: tuple(0 for _ in a.shape))
    args = (xg_t, glu_t, cs_t, gs_t, h0, P['lru_conv_w'], P['lru_conv_b'], P['lru_wa_bd'], P['lru_ba'],
            P['lru_wx_bd'], P['lru_bx'], P['lru_lam'], P['conf_dw_w'], P['conf_dw_b'], P['conf_ln_g'], P['conf_ln_b'])
    outs = [jax.ShapeDtypeStruct((T, nb, 256), F32), jax.ShapeDtypeStruct((T, nb, 256), F32),
            jax.ShapeDtypeStruct((nb, 256), F32), jax.ShapeDtypeStruct((nc, nb, 256), F32),
            jax.ShapeDtypeStruct((ng, nb, 256), F32)]
    return pl.pallas_call(
        _seq_sample_kernel,
        grid=(1,),
        in_specs=[full(a) for a in args],
        out_specs=[full(o) for o in outs],
        out_shape=outs,
        scratch_shapes=[pltpu.VMEM((T + nc, nb, 256), F32), pltpu.VMEM((T + ng, nb, 256), F32)],
        compiler_params=_cparams(("arbitrary",)),
        name="seq_sample",
    )(*args)


def _softmax_stages(vt_ref, m_ref, acc_ref):
    def soft(s_ref, p_ref, a_ref):
        s = s_ref[...]
        m_prev = m_ref[...]
        m_new = jnp.maximum(m_prev, jnp.max(s, axis=0, keepdims=True))
        p_ref[...] = jnp.exp2(s - m_new).astype(BF16)
        a_ref[...] = jnp.exp2(m_prev - m_new)
        m_ref[...] = m_new

    def pv(p_ref, a_ref, tile):
        acc_ref[...] = a_ref[...] * acc_ref[...] + jnp.dot(vt_ref[tile], p_ref[...], preferred_element_type=F32)

    return soft, pv


def _softmax_pv(s, v_t):
    p = jnp.exp2(s - jnp.max(s, axis=0, keepdims=True))
    acc = jnp.dot(v_t, p.astype(BF16), preferred_element_type=F32)
    dv = acc.shape[0] - VSUM_ROWS
    return acc[0:dv] / jnp.maximum(acc[dv:dv + 1], 1e-30)


def _init_stats(m_ref, acc_ref, l_ref=None):
    m_ref[...] = jnp.full(m_ref.shape, NEG, F32)
    acc_ref[...] = jnp.zeros(acc_ref.shape, F32)
    if l_ref is not None:
        l_ref[...] = jnp.zeros(l_ref.shape, F32)


def _finish_t(acc_ref):
    dv = acc_ref.shape[0] - VSUM_ROWS
    return acc_ref[0:dv, :] / jnp.maximum(acc_ref[dv:dv + 1, :], 1e-30)


def _finish(l_ref, acc_ref):
    return acc_ref[...] / jnp.maximum(l_ref[...], 1e-30)


def _compressed_kv(p_ref, pe_ref, nc):
    ncp = p_ref.shape[0]
    top = p_ref[:, 0:128]
    bot = pltpu.roll(p_ref[:, 128:256], ncp - 1, 0)
    n = lax.broadcasted_iota(jnp.int32, (ncp, 128), 0)
    return jnp.where(n < nc, top + bot + pe_ref[0:1, :], 0.0)


def _compressed_branch(q, kcv, q_pos, nc):
    ncp = kcv.shape[0]
    s = _dot_nt(kcv[:, 0:64], q)
    n = lax.broadcasted_iota(jnp.int32, (ncp, 1), 0)
    last_row = jnp.where(n < nc, n * NSA_CMP_STRIDE + (NSA_CMP_BLOCK - 1), 2 ** 30)
    mask = last_row <= q_pos
    s = jnp.where(mask, s, NEG)
    e = jnp.where(mask, jnp.exp2(s - jnp.max(s, axis=0, keepdims=True)), 0.0)
    p = e / jnp.maximum(jnp.sum(e, axis=0, keepdims=True), 1e-30)
    o = _dot_tn(kcv[:, 64:128], p)
    return o, p


def _select_blocks(imp, q_pos, n_sel, q_pos_min=None):
    assert n_sel >= 3 and NSA_FORCE > 2 * NSA_HEADS
    blk = lax.broadcasted_iota(jnp.int32, imp.shape, 0)
    cur = q_pos // NSA_SEL_BLOCK
    valid = blk * NSA_SEL_BLOCK <= q_pos
    forced = (blk == 0) | (blk == cur) | (blk == cur - 1)
    n_forced = jnp.minimum(cur + 1, 3)
    bias = jnp.where(forced, 0.0, NEG)
    score = jnp.where(forced, -3e38, jnp.where(valid, imp, -1.0))
    blk_f = blk.astype(F32)
    always_three = q_pos_min is not None and q_pos_min // NSA_SEL_BLOCK >= 2
    for e in range(n_sel - (3 if always_three else 1)):
        mx = jnp.max(score, axis=0, keepdims=True)
        first = jnp.min(jnp.where(score == mx, blk_f, 1e9), axis=0, keepdims=True)
        if e >= n_sel - 3:
            first = jnp.where(n_forced + e < n_sel, first, -1.0)
        hit = blk_f == first
        bias = jnp.where(hit, 0.0, bias)
        score = jnp.where(hit, -3e38, score)
    return bias


def _cmp_prompt_kernel(x_ref, w_ref, p_ref):
    n_chunk = p_ref.shape[0]
    acc = jnp.zeros(p_ref.shape, F32)
    for r in range(NSA_CMP_STRIDE):
        acc = acc + _bdot(x_ref[pl.ds(r, n_chunk, stride=NSA_CMP_STRIDE), :], w_ref[r])
    p_ref[...] = acc


def _cmp_prompt(nsa4, w4, nb):
    N = nsa4.shape[0]
    S = N // nb
    n_chunk = S // NSA_CMP_STRIDE
    return pl.pallas_call(
        _cmp_prompt_kernel,
        grid=(nb,),
        in_specs=[pl.BlockSpec((S, 128), lambda b: (b, 0)),
                  pl.BlockSpec((NSA_CMP_STRIDE, 128, 256), lambda b: (0, 0, 0))],
        out_specs=pl.BlockSpec((None, n_chunk, 256), lambda b: (b, 0, 0)),
        out_shape=jax.ShapeDtypeStruct((nb, n_chunk, 256), F32),
        compiler_params=_cparams(("parallel",)),
        name="cmp_prompt",
    )(nsa4, w4)


def _attn_prompt_kernel(qm_ref, km_ref, vmt_ref, wuvt_ref, q_ref, g_ref, p_ref, pe_ref, ov_ref, ks_ref, vst_ref,
                        kw_ref, vwt_ref, c_ref, o_ref,
                        m_ref, acc_ref, mm_ref, accm_ref, ow_ref, *stage, nc, n_sel):
    stage_refs = [stage[3 * n:3 * n + 3] for n in range(4)]
    i = pl.program_id(1)
    _, tq, _ = q_ref.shape
    tk = ks_ref.shape[1]
    tw = kw_ref.shape[1]
    R = NSA_HEADS * tq
    q_f32 = q_ref[...].reshape(R, NSA_HD)
    q = q_f32.astype(BF16)
    qm = qm_ref[...].reshape(R, MLA_CACHE_W).astype(BF16)
    q_pos_q = i * tq + lax.broadcasted_iota(jnp.int32, (1, tq), 1)
    q_pos = jnp.concatenate([q_pos_q] * NSA_HEADS, axis=1)

    n_wt = NSA_WINDOW // tw
    kw = jnp.concatenate([kw_ref[i + c] for c in range(n_wt + 1)], axis=0)
    vw_t = jnp.concatenate([vwt_ref[i + c] for c in range(n_wt + 1)], axis=1)
    k_pos = (i - n_wt) * tw + lax.broadcasted_iota(jnp.int32, (kw.shape[0], R), 0)
    first_pos = jnp.maximum(q_pos - (NSA_WINDOW - 1), 0)
    s_w = jnp.where(k_pos <= q_pos, jnp.where(k_pos >= first_pos, _dot_nt(kw, q), NEG), NEG)
    ow_ref[...] = _softmax_pv(s_w, vw_t)

    kcv = _compressed_kv(p_ref, pe_ref, nc)
    o_c, p_c = _compressed_branch(q, kcv, q_pos, nc)
    imp4 = jnp.dot(ov_ref[...], p_c.astype(BF16), preferred_element_type=F32)
    imp = imp4[:, 0:tq]
    for hd in range(1, NSA_HEADS):
        imp = imp + imp4[:, hd * tq:(hd + 1) * tq]
    bias = _select_blocks(imp, q_pos_q, n_sel).T
    q_sel = jnp.concatenate([jnp.concatenate([bias] * NSA_HEADS, axis=0), q_f32], axis=1).astype(BF16)

    n_full = (i * tq) // tk
    _init_stats(m_ref, acc_ref)
    _init_stats(mm_ref, accm_ref)

    def score_m(j):
        return _dot_nt(km_ref[j], qm)

    def score_s(j):
        return _dot_nt(ks_ref[j], q_sel)

    def full_tile(j):
        return jnp.clip(j, 0, jnp.maximum(n_full - 1, 0))

    soft_m, pv_m = _softmax_stages(vmt_ref, mm_ref, accm_ref)
    soft_s, pv_s = _softmax_stages(vst_ref, m_ref, acc_ref)
    (sma, pma, ama), (smb, pmb, amb), (ssa, psa, asa), (ssb, psb, asb) = stage_refs

    causal = n_full * tk + lax.broadcasted_iota(jnp.int32, (tk, R), 0) <= q_pos
    sma[...] = jnp.where(causal, score_m(n_full), NEG)
    ssa[...] = jnp.where(causal, score_s(n_full), NEG)
    smb[...] = score_m(full_tile(0))
    ssb[...] = score_s(full_tile(0))
    soft_m(sma, pma, ama)
    soft_s(ssa, psa, asa)

    def pair(jj, c):
        k = 2 * jj + 1
        prev = jnp.where(jj == 0, n_full, k - 2)
        sma[...] = score_m(full_tile(k))
        ssa[...] = score_s(full_tile(k))
        soft_m(smb, pmb, amb)
        soft_s(ssb, psb, asb)
        pv_m(pma, ama, prev)
        pv_s(psa, asa, prev)
        smb[...] = score_m(full_tile(k + 1))
        ssb[...] = score_s(full_tile(k + 1))
        soft_m(sma, pma, ama)
        soft_s(ssa, psa, asa)
        pv_m(pmb, amb, k - 1)
        pv_s(psb, asb, k - 1)
        return c

    n_pair = n_full // 2
    lax.fori_loop(0, n_pair, pair, 0)
    last = jnp.where(n_pair == 0, n_full, 2 * n_pair - 1)
    pv_m(pma, ama, last)
    pv_s(psa, asa, last)

    @pl.when(n_full % 2 == 1)
    def _():
        soft_m(smb, pmb, amb)
        soft_s(ssb, psb, asb)
        pv_m(pmb, amb, n_full - 1)
        pv_s(psb, asb, n_full - 1)

    o_s = _finish_t(acc_ref)
    o_m = _finish_t(accm_ref)
    c_t = jnp.concatenate([_bdot(wuvt_ref[hd], o_m[:, hd * tq:(hd + 1) * tq]) for hd in range(MLA_HEADS)], axis=0)
    c_ref[...] = c_t.T
    o_w = ow_ref[...]

    g_t = g_ref[...].T
    outs = []
    for hd in range(NSA_HEADS):
        sl = slice(hd * tq, (hd + 1) * tq)
        r0 = MISC_NG + 3 * hd
        outs.append(g_t[r0:r0 + 1, :] * o_c[:, sl] + g_t[r0 + 1:r0 + 2, :] * o_s[:, sl]
                    + g_t[r0 + 2:r0 + 3, :] * o_w[:, sl])
    o_ref[...] = jnp.concatenate(outs, axis=0).T


def _attn_prompt(qm, km3, vmt3, wuvt, q, gates, P, pe, ov, ks3, vst3, kw3, vwt3, *, tq, nc, n_sel):
    nb, nk, tk, dks = ks3.shape
    nw, tw = kw3.shape[1:3]
    assert tw == tq
    N = q.shape[1]
    nq = N // nb // tq
    ncp = P.shape[1]
    nblk = ov.shape[0]
    R = 4 * tq
    out = pl.BlockSpec((tq, 256), lambda b, i: (b * nq + i, 0))
    whole = lambda a: pl.BlockSpec((None,) + a.shape[1:], lambda b, i: (b,) + (0,) * (a.ndim - 1))
    return pl.pallas_call(
        functools.partial(_attn_prompt_kernel, nc=nc, n_sel=n_sel),
        grid=(nb, nq),
        in_specs=[pl.BlockSpec((4, tq, MLA_CACHE_W), lambda b, i: (0, b * nq + i, 0)),
                  whole(km3), whole(vmt3),
                  pl.BlockSpec((4, MLA_V, MLA_KV_LORA), lambda b, i: (0, 0, 0)),
                  pl.BlockSpec((4, tq, NSA_HD), lambda b, i: (0, b * nq + i, 0)),
                  pl.BlockSpec((tq, 128), lambda b, i: (b * nq + i, 0)),
                  pl.BlockSpec((None, ncp, 256), lambda b, i: (b, 0, 0)),
                  pl.BlockSpec((8, 128), lambda b, i: (0, 0)),
                  pl.BlockSpec((nblk, ncp), lambda b, i: (0, 0)),
                  whole(ks3), whole(vst3), whole(kw3), whole(vwt3)],
        out_specs=[out, out],
        out_shape=[jax.ShapeDtypeStruct((N, 256), F32), jax.ShapeDtypeStruct((N, 256), F32)],
        scratch_shapes=[pltpu.VMEM((1, R), F32), pltpu.VMEM((NSA_HD + VSUM_ROWS, R), F32),
                        pltpu.VMEM((1, R), F32), pltpu.VMEM((MLA_KV_LORA + VSUM_ROWS, R), F32)]
        + [pltpu.VMEM((NSA_HD, R), F32)]
        + [pltpu.VMEM((tk, R), F32), pltpu.VMEM((tk, R), BF16), pltpu.VMEM((1, R), F32)] * 4,
        compiler_params=_cparams(("arbitrary", "arbitrary")),
        name="attn_prompt",
    )(qm, km3, vmt3, wuvt, q, gates, P, pe, ov, ks3, vst3, kw3, vwt3)


def _sample_queries(q_ref):
    nh, T, d = q_ref.shape
    q = q_ref[...].reshape(nh * T, d)
    return jnp.concatenate([q, jnp.zeros((SAMPLE_R - nh * T, d), F32)], axis=0).astype(BF16)


def _rows_attend(parts):
    stats = []
    for s, pv in parts:
        m_i = jnp.max(s, axis=1, keepdims=True)
        p = jnp.exp2(s - m_i)
        stats.append((m_i, jnp.sum(p, axis=1, keepdims=True), pv(p.astype(BF16))))
    m = functools.reduce(jnp.maximum, [m_i for m_i, _, _ in stats])
    num, den = 0.0, 0.0
    for m_i, l_i, o_i in stats:
        w = jnp.exp2(m_i - m)
        den = den + w * l_i
        num = num + w * o_i
    return num / jnp.maximum(den, 1e-30)


def _page_groups(n_pages):
    size = max(n_pages // KEY_GROUPS, 1)
    return [range(a, min(a + size, n_pages)) for a in range(0, n_pages, size)]


def _page_copy(cache_ref, pt_ref, buf_ref, sem_ref, layer, n_pages, b, slot, j):
    return pltpu.make_async_copy(cache_ref.at[layer, pt_ref[b * n_pages + j]], buf_ref.at[slot, j], sem_ref.at[slot])


def _fetch_pages(cache_ref, pt_ref, buf_ref, sem_ref, layer, n_pages):
    b = pl.program_id(0)
    slot = b % 2
    copy = functools.partial(_page_copy, cache_ref, pt_ref, buf_ref, sem_ref, layer, n_pages)

    @pl.when(b == 0)
    def _():
        for j in range(n_pages):
            copy(b, slot, j).start()

    @pl.when(b + 1 < pl.num_programs(0))
    def _():
        for j in range(n_pages):
            copy(b + 1, 1 - slot, j).start()

    for j in range(n_pages):
        copy(b, slot, j).wait()
    return slot


def _mla_sample_kernel(pt_ref, cache_ref, q_ref, new_ref, wuv_ref, o_ref, buf_ref, sem_ref, *, layer, n_pages):
    slot = _fetch_pages(cache_ref, pt_ref, buf_ref, sem_ref, layer, n_pages)
    _, T, dk = q_ref.shape
    R = MLA_HEADS * T
    q = q_ref[...].reshape(R, dk).astype(BF16)
    parts = []
    for pages in _page_groups(n_pages):
        kt = jnp.concatenate([buf_ref[slot, j].astype(BF16) for j in pages], axis=1)
        parts.append((jnp.dot(q, kt, preferred_element_type=F32),
                      lambda p, vt=kt[0:MLA_KV_LORA]: _dot_nt(p, vt)))
    new = new_ref[...].astype(BF16)
    k_t = lax.broadcasted_iota(jnp.int32, (R, T), 1)
    q_t = lax.broadcasted_iota(jnp.int32, (R, T), 0) % T
    parts.append((jnp.where(k_t <= q_t, _dot_nt(q, new), NEG),
                  lambda p: jnp.dot(p, new[:, 0:MLA_KV_LORA], preferred_element_type=F32)))
    o = _rows_attend(parts)
    o_ref[...] = jnp.concatenate([_bdot(o[hd * T:(hd + 1) * T, :], wuv_ref[hd]) for hd in range(MLA_HEADS)], axis=1)


def _mla_sample(cache_mla_t, pt_flat, q, rows, wuv, layer, nb, n_pages):
    T = q.shape[1] // nb
    N = q.shape[1]
    page = cache_mla_t.shape[3]
    grid_spec = pltpu.PrefetchScalarGridSpec(
        num_scalar_prefetch=1,
        grid=(nb,),
        in_specs=[pl.BlockSpec(memory_space=pl.ANY),
                  pl.BlockSpec((4, T, MLA_CACHE_W), lambda b, pt: (0, b, 0)),
                  pl.BlockSpec((T, MLA_CACHE_W), lambda b, pt: (b, 0)),
                  pl.BlockSpec((4, MLA_KV_LORA, MLA_V), lambda b, pt: (0, 0, 0))],
        out_specs=pl.BlockSpec((T, 256), lambda b, pt: (b, 0)),
        scratch_shapes=[pltpu.VMEM((2, n_pages, MLA_CACHE_W, page), F32), pltpu.SemaphoreType.DMA((2,))],
    )
    return pl.pallas_call(
        functools.partial(_mla_sample_kernel, layer=layer, n_pages=n_pages),
        grid_spec=grid_spec,
        out_shape=jax.ShapeDtypeStruct((N, 256), F32),
        compiler_params=_cparams(("arbitrary",)),
        name="mla_sample",
    )(pt_flat, cache_mla_t, q, rows, wuv)


def _nsa_sample_kernel(pt_ref, cache_ref, q_ref, g_ref, w4_ref, perm_ref, pe_ref, ov_ref, e_ref, new_ref, wc_ref,
                       wn_ref, wnt_ref, o_ref, wo_ref, buf_ref, sem_ref, y_ref, *, layer, n_pages, nc, n_sel, past):
    slot = _fetch_pages(cache_ref, pt_ref, buf_ref, sem_ref, layer, n_pages)
    _, T, _ = q_ref.shape
    R = NSA_HEADS * T
    half = n_pages // 2
    cpp = y_ref.shape[1] // NSA_CMP_STRIDE
    q_f32 = q_ref[...].reshape(R, NSA_HD)
    q = q_f32.astype(BF16)
    q_pos = past + lax.broadcasted_iota(jnp.int32, (R, 1), 0) % T

    for jp in range(half):
        pair = jnp.concatenate([buf_ref[slot, jp, 0:128, :], buf_ref[slot, jp + half, 0:128, :]], axis=0)
        y_ref[jp] = _dot_nt(perm_ref[...], pair)
    slabs = []
    for r in range(NSA_CMP_STRIDE):
        yr = y_ref[:, r * cpp:(r + 1) * cpp, :].reshape(half * cpp, 256)
        slabs.append(jnp.concatenate([yr[:, 0:128], yr[:, 128:256]], axis=0))
    P = _bdot(jnp.concatenate(slabs, axis=1), w4_ref[...].reshape(NSA_CMP_STRIDE * 128, 256))

    lane = lax.broadcasted_iota(jnp.int32, (1, SAMPLE_R), 1)
    q_pos_l = past + lane % T
    kcv = _compressed_kv(P, pe_ref, nc)
    o_c, p_c = _compressed_branch(_sample_queries(q_ref), kcv, q_pos_l, nc)
    o_c = o_c.T[0:R]
    imp = jnp.dot(ov_ref[...], p_c.astype(BF16), preferred_element_type=F32)
    imp = jnp.where(lane < R, imp, 0.0)
    tot = imp
    for k in range(1, NSA_HEADS):
        tot = tot + pltpu.roll(imp, k * T, 1) + pltpu.roll(imp, SAMPLE_R - R + k * T, 1)
    n_real = -(-(-(-(past + T) // NSA_SEL_BLOCK)) // 8) * 8
    bias = jnp.concatenate([_select_blocks(tot[0:n_real], q_pos_l, n_sel, q_pos_min=past),
                            jnp.full((tot.shape[0] - n_real, SAMPLE_R), NEG, F32)], axis=0).T[0:R]

    n_blk_past = e_ref.shape[0]
    q_sel = jnp.concatenate([bias[:, 0:n_blk_past], q_f32], axis=1).astype(BF16)
    page = buf_ref.shape[3]
    parts = []
    for pages in _page_groups(n_pages):
        kvt = jnp.concatenate([buf_ref[slot, j, 128:256, :].astype(BF16) for j in pages], axis=1)
        k_sel = jnp.concatenate([e_ref[:, pages[0] * page:(pages[-1] + 1) * page], kvt[0:64]], axis=0)
        parts.append((jnp.dot(q_sel, k_sel, preferred_element_type=F32), lambda p, vt=kvt[64:128]: _dot_nt(p, vt)))
    new = new_ref[...].astype(BF16)
    k_new = past + lax.broadcasted_iota(jnp.int32, (R, T), 1)
    s_new = _dot_nt(q, new[:, 128:192]) + bias[:, n_blk_past:n_blk_past + 1]
    parts.append((jnp.where(k_new <= q_pos, s_new, NEG),
                  lambda p: jnp.dot(p, new[:, 192:256], preferred_element_type=F32)))
    o_s = _rows_attend(parts)

    wc = wc_ref[...].astype(BF16)
    n_buf = wc.shape[1]
    k_buf = past - n_buf + lax.broadcasted_iota(jnp.int32, (R, n_buf), 1)
    ok_buf = (k_buf >= 0) & (k_buf <= q_pos) & (k_buf > q_pos - NSA_WINDOW)
    wn = wn_ref[...].astype(BF16)
    ok_new = (k_new <= q_pos) & (k_new > q_pos - NSA_WINDOW)
    o_w = _rows_attend([
        (jnp.where(ok_buf, jnp.dot(q, wc[0:64], preferred_element_type=F32), NEG), lambda p: _dot_nt(p, wc[64:128])),
        (jnp.where(ok_new, _dot_nt(q, wn[:, 0:64]), NEG),
         lambda p: jnp.dot(p, wn[:, 64:128], preferred_element_type=F32))])

    g = g_ref[...]
    outs = []
    for hd in range(NSA_HEADS):
        sl = slice(hd * T, (hd + 1) * T)
        r0 = MISC_NG + 3 * hd
        outs.append(g[:, r0:r0 + 1] * o_c[sl] + g[:, r0 + 1:r0 + 2] * o_s[sl] + g[:, r0 + 2:r0 + 3] * o_w[sl])
    o_ref[...] = jnp.concatenate(outs, axis=1)

    n_win = wo_ref.shape[1]
    keep = n_win - T
    wo_ref[:, 0:keep] = wc_ref[:, n_buf - keep:n_buf]
    wo_ref[:, keep:n_win] = wnt_ref[...]


def _nsa_sample(cache_nsa_t, pt_flat, q, gates, w4, pe, ov, new4, win_cache_t, win_new, layer, nb, n_pages,
                *, nc, n_sel, past):
    N = q.shape[1]
    T = N // nb
    nblk, ncp = ov.shape
    n_buf = win_cache_t.shape[3]
    page = cache_nsa_t.shape[3]
    cpp = page // NSA_CMP_STRIDE
    assert n_pages % 2 == 0 and cpp == 8 and ncp == n_pages * cpp
    n_win = min(NSA_WINDOW, n_buf + T)
    win_new_t = jnp.swapaxes(win_new.reshape(nb, T, 128), 1, 2)
    n_blk_past = n_pages * page // NSA_SEL_BLOCK
    key_blk = np.arange(n_pages * page)[None, :] // NSA_SEL_BLOCK
    onehot = jnp.asarray((key_blk == np.arange(n_blk_past)[:, None]).astype(np.float32), dtype=BF16)
    row = np.arange(page)
    perm = jnp.asarray((np.arange(page)[None, :] == ((row % cpp) * NSA_CMP_STRIDE + row // cpp)[:, None])
                       .astype(np.float32), dtype=BF16)
    const = lambda a: pl.BlockSpec(a.shape, lambda b, pt: (0,) * a.ndim)
    grid_spec = pltpu.PrefetchScalarGridSpec(
        num_scalar_prefetch=1,
        grid=(nb,),
        in_specs=[pl.BlockSpec(memory_space=pl.ANY),
                  pl.BlockSpec((4, T, NSA_HD), lambda b, pt: (0, b, 0)),
                  pl.BlockSpec((T, 128), lambda b, pt: (b, 0)),
                  const(w4), const(perm), const(pe), const(ov), const(onehot),
                  pl.BlockSpec((T, 256), lambda b, pt: (b, 0)),
                  pl.BlockSpec((None, None, 128, n_buf), lambda b, pt: (layer, b, 0, 0)),
                  pl.BlockSpec((T, 128), lambda b, pt: (b, 0)),
                  pl.BlockSpec((None, 128, T), lambda b, pt: (b, 0, 0))],
        out_specs=[pl.BlockSpec((T, 256), lambda b, pt: (b, 0)),
                   pl.BlockSpec((None, 128, n_win), lambda b, pt: (b, 0, 0))],
        scratch_shapes=[pltpu.VMEM((2, n_pages, 4 * NSA_HD, page), F32), pltpu.SemaphoreType.DMA((2,)),
                        pltpu.VMEM((n_pages // 2, page, 256), F32)],
    )
    return pl.pallas_call(
        functools.partial(_nsa_sample_kernel, layer=layer, n_pages=n_pages, nc=nc, n_sel=n_sel, past=past),
        grid_spec=grid_spec,
        out_shape=[jax.ShapeDtypeStruct((N, 256), F32), jax.ShapeDtypeStruct((nb, 128, n_win), F32)],
        compiler_params=_cparams(("arbitrary",)),
        name="nsa_sample",
    )(pt_flat, cache_nsa_t, q, gates, w4, perm, pe, ov, onehot, new4, win_cache_t, win_new, win_new_t)


def _block_diag(w):
    n, a, b = w.shape
    out = jnp.zeros((n * a, n * b), w.dtype)
    for j in range(n):
        out = out.at[j * a:(j + 1) * a, j * b:(j + 1) * b].set(w[j])
    return out


def _rot_cols(w):
    half = w.shape[-1] // 2
    return jnp.concatenate([-w[..., half:], w[..., :half]], axis=-1)


def _layer_params(l, W):
    idx = np.cumsum(IN_SPLITS)[:-1].tolist()
    lru_x, lru_g, conf_u, cq, ckv, kr, nq, nkv, ng = jnp.split(W['w_in'][l], idx, axis=-1)
    pad = jnp.zeros((D_MODEL, D_INP - C_MISC - 2 * MLA_ROPE - 3 * NSA_HEADS), F32)
    w_in = jnp.concatenate([lru_x, lru_g, conf_u, cq, ckv, nq, nkv, kr, _rot_cols(kr), ng, pad], axis=-1)
    wuq = W['mla_wuq'][l].reshape(MLA_Q_LORA, MLA_HEADS, MLA_NOPE + MLA_ROPE)
    wuq_rope = wuq[:, :, MLA_NOPE:]
    wuq = jnp.concatenate([wuq[:, :, :MLA_NOPE].reshape(MLA_Q_LORA, -1), wuq_rope.reshape(MLA_Q_LORA, -1),
                           _rot_cols(wuq_rope).reshape(MLA_Q_LORA, -1)], axis=-1)
    wuk = _block_diag(jnp.transpose(W['mla_wuk'][l], (1, 2, 0)))
    wuv = jnp.transpose(W['mla_wuv'][l], (1, 0, 2))
    wk = W['nsa_wc_k'][l].reshape(NSA_CMP_BLOCK, NSA_HD, NSA_HD)
    wv = W['nsa_wc_v'][l].reshape(NSA_CMP_BLOCK, NSA_HD, NSA_HD)
    z = jnp.zeros((NSA_CMP_STRIDE, NSA_HD, NSA_HD), F32)
    w4 = jnp.concatenate([jnp.concatenate([wk[:16], z, wk[16:], z], axis=2),
                          jnp.concatenate([z, wv[:16], z, wv[16:]], axis=2)], axis=1)
    pe = jnp.concatenate([W['nsa_pe_k'][l], W['nsa_pe_v'][l]], axis=-1)
    pe2 = jnp.concatenate([pe[:16, None], pe[16:, None], jnp.zeros((16, 6, 128), F32)], axis=1)
    r1 = lambda a: a.reshape(1, -1)
    return dict(
        w_in=w_in.astype(BF16), g_pre_mix=r1(W['g_pre_mix'][l]), g_post_mix=r1(W['g_post_mix'][l]),
        g_pre_ffn=r1(W['g_pre_ffn'][l]), g_post_ffn=r1(W['g_post_ffn'][l]),
        mla_gq=r1(W['mla_gq'][l]), mla_gkv=r1(W['mla_gkv'][l]),
        wuq=wuq.astype(BF16), wuk=wuk.astype(BF16), wuv=wuv.astype(BF16),
        wuvt=jnp.transpose(wuv, (0, 2, 1)).astype(BF16),
        w4=w4.astype(BF16), pe2=pe2,
        lru_conv_w=W['lru_conv_w'][l], lru_conv_b=r1(W['lru_conv_b'][l]),
        lru_wa_bd=_block_diag(W['lru_wa'][l]).astype(BF16), lru_ba=r1(W['lru_ba'][l]),
        lru_wx_bd=_block_diag(W['lru_wx'][l]).astype(BF16), lru_bx=r1(W['lru_bx'][l]),
        lru_lam=r1(W['lru_lam'][l]),
        conf_dw_w=W['conf_dw_w'][l], conf_dw_b=r1(W['conf_dw_b'][l]),
        conf_ln_g=r1(W['conf_ln_g'][l]), conf_ln_b=r1(W['conf_ln_b'][l]),
        w_out=W['w_out'][l].astype(BF16), w_ff1=W['w_ff1'][l].astype(BF16), w_ff2=W['w_ff2'][l].astype(BF16),
    )


def _rope_tables(pos):
    half = MLA_ROPE // 2
    freq = jnp.power(ROPE_THETA, -jnp.arange(half, dtype=F32) / half)
    ang = pos.astype(F32)[:, None] * freq
    cosv = jnp.tile(jnp.cos(ang), (1, 2 * MLA_HEADS))
    sinv = jnp.tile(jnp.sin(ang), (1, 2 * MLA_HEADS))
    return cosv, sinv


def _overlap_t(nblk_pad, ncp, nc):
    n = np.arange(ncp)[None, :]
    j = np.arange(nblk_pad)[:, None]
    ov = (n * NSA_CMP_STRIDE < j * NSA_SEL_BLOCK + NSA_SEL_BLOCK) & (n * NSA_CMP_STRIDE + NSA_CMP_BLOCK > j * NSA_SEL_BLOCK)
    return jnp.asarray((ov & (n < nc)).astype(np.float32), dtype=BF16)


def _key_tiles(a, nb, t, front=0):
    k = a.reshape(nb, -1, t, a.shape[-1])
    return jnp.pad(k, ((0, 0), (front, 0), (0, 0), (0, 0)))


def _value_tiles_t(a, nb, t, front=0):
    v = jnp.swapaxes(_key_tiles(a, nb, t, front), 2, 3)
    ones = jnp.ones(v.shape[:2] + (1, t), v.dtype)
    return jnp.concatenate([v, ones, jnp.zeros(v.shape[:2] + (VSUM_ROWS - 1, t), v.dtype)], axis=2)


def kernel(x_prompt, x_sample, c_prompt, c_sample, state_lru_h, state_lru_conv, state_conv, cache_mla, cache_nsa,
           cache_nsa_win, page_table, w_mod, b_mod, g_pre_mix, g_post_mix, g_pre_ffn, g_post_ffn, w_in, lru_conv_w,
           lru_conv_b, lru_wa, lru_ba, lru_wx, lru_bx, lru_lam, conf_dw_w, conf_dw_b, conf_ln_g, conf_ln_b, mla_gq,
           mla_wuq, mla_gkv, mla_wuk, mla_wuv, nsa_pe_k, nsa_wc_k, nsa_pe_v, nsa_wc_v, w_out, w_ff1, w_ff2):
    W = dict(w_in=w_in, g_pre_mix=g_pre_mix, g_post_mix=g_post_mix, g_pre_ffn=g_pre_ffn, g_post_ffn=g_post_ffn,
             lru_conv_w=lru_conv_w, lru_conv_b=lru_conv_b, lru_wa=lru_wa, lru_ba=lru_ba, lru_wx=lru_wx, lru_bx=lru_bx,
             lru_lam=lru_lam, conf_dw_w=conf_dw_w, conf_dw_b=conf_dw_b, conf_ln_g=conf_ln_g, conf_ln_b=conf_ln_b,
             mla_gq=mla_gq, mla_wuq=mla_wuq, mla_gkv=mla_gkv, mla_wuk=mla_wuk, mla_wuv=mla_wuv,
             nsa_pe_k=nsa_pe_k, nsa_wc_k=nsa_wc_k, nsa_pe_v=nsa_pe_v, nsa_wc_v=nsa_wc_v,
             w_out=w_out, w_ff1=w_ff1, w_ff2=w_ff2)
    L = w_in.shape[0]
    nbp, S, D = x_prompt.shape
    nbs, T, _ = x_sample.shape
    n_pages = page_table.shape[1]
    page = cache_mla.shape[2]
    past = n_pages * page
    Np, Ns = nbp * S, nbs * T
    tq = min(Q_TILE, S)
    tk = min(K_TILE, S)
    tm_p = min(256, Np)
    tm_s = min(256, Ns)
    tc = min(512, S)

    nc_p = S // NSA_CMP_STRIDE - 1
    nblk_p = -(-S // NSA_SEL_BLOCK)
    ov_p = _overlap_t(nblk_p, S // NSA_CMP_STRIDE, nc_p)
    assert NSA_WINDOW % tq == 0 and S % tk == 0 and tk % tq == 0
    blk_of_pos = jnp.arange(S, dtype=jnp.int32)[:, None] // NSA_SEL_BLOCK
    blk_onehot_p = jnp.tile((blk_of_pos == jnp.arange(nblk_p, dtype=jnp.int32)[None, :]).astype(BF16), (nbp, 1))
    len_s = past + T
    assert len_s // NSA_CMP_STRIDE == past // NSA_CMP_STRIDE and past % NSA_SEL_BLOCK == 0 and T <= NSA_SEL_BLOCK
    nc_s = len_s // NSA_CMP_STRIDE - 1
    nblk_s = -(-len_s // NSA_SEL_BLOCK)
    nblk_s_pad = -(-nblk_s // 128) * 128
    ov_s = _overlap_t(nblk_s_pad, past // NSA_CMP_STRIDE, nc_s)

    rows = nbp + nbs
    rows_pad = -(-rows // 8) * 8
    c_all = jnp.concatenate([c_prompt, c_sample, jnp.zeros((rows_pad - rows, D), F32)], axis=0)
    mod = _modulation(c_all, w_mod, b_mod)

    params = [_layer_params(l, W) for l in range(L)]
    pe_all = _pe_term(jnp.stack([p['pe2'] for p in params]), jnp.stack([p['w4'] for p in params]))

    cos_p, sin_p = _rope_tables(jnp.arange(S, dtype=jnp.int32))
    cos_s, sin_s = _rope_tables(past + jnp.arange(T, dtype=jnp.int32))
    cos_s = jnp.tile(cos_s, (nbs, 1))
    sin_s = jnp.tile(sin_s, (nbs, 1))
    pt_flat = page_table.reshape(-1).astype(jnp.int32)
    cache_mla_t = jnp.swapaxes(cache_mla, 2, 3)
    cache_nsa_t = jnp.transpose(cache_nsa, (0, 1, 3, 4, 2)).reshape(L, cache_nsa.shape[1], 4 * NSA_HD, page)
    win_cache_t = jnp.transpose(cache_nsa_win, (0, 1, 3, 4, 2)).reshape(L, nbs, 2 * NSA_HD, cache_nsa_win.shape[2])

    xp = x_prompt.reshape(Np, D)
    xs = x_sample.reshape(Ns, D)
    outs_p, outs_s = [], []
    for l in range(L):
        P = params[l]
        pe = pe_all[l]
        mp = [m.reshape(nbp, 1, D) for m in jnp.split(mod[l, :nbp], 6, axis=-1)]
        ms = [jnp.repeat(m, T, axis=0) for m in jnp.split(mod[l, nbp:rows], 6, axis=-1)]

        lru, glu, qm, row, nq, nsa4, win, gates = _in_proj(
            xp, mp[0], mp[1], P['g_pre_mix'], P['w_in'], P['mla_gq'], P['wuq'], P['wuk'], P['mla_gkv'],
            cos_p, sin_p, per_token=False, tm=tm_p)
        a_out, b_out, h_last = _seq_prompt(lru.reshape(nbp, S, 512), glu.reshape(nbp, S, 256), P, tc=tc)
        rowb = row.astype(BF16)
        k3 = _key_tiles(rowb, nbp, tk)
        vt3 = _value_tiles_t(rowb[:, :MLA_KV_LORA], nbp, tk)
        Pp = _cmp_prompt(nsa4, P['w4'], nbp)
        nsab = nsa4.astype(BF16)
        ks3 = _key_tiles(jnp.concatenate([blk_onehot_p, nsab[:, 128:192]], axis=1), nbp, tk)
        vst3 = _value_tiles_t(nsab[:, 192:256], nbp, tk)
        winb = win.astype(BF16)
        kw3 = _key_tiles(winb[:, 0:64], nbp, tq, front=NSA_WINDOW // tq)
        vwt3 = _value_tiles_t(winb[:, 64:128], nbp, tq, front=NSA_WINDOW // tq)
        c_out, d_out = _attn_prompt(qm, k3, vt3, P['wuvt'], nq, gates, Pp, pe, ov_p, ks3, vst3, kw3, vwt3, tq=tq,
                                    nc=nc_p, n_sel=min(NSA_N_SEL, nblk_p))
        xp = _post(xp, a_out.reshape(Np, 256), b_out.reshape(Np, 256), c_out, d_out, mp[2], mp[3], mp[4], mp[5],
                   P['w_out'], P['g_post_mix'], P['g_pre_ffn'], P['g_post_ffn'], P['w_ff1'], P['w_ff2'],
                   per_token=False, tm=tm_p)
        n_win = min(NSA_WINDOW, S)
        outs_p.append((h_last[:, 0], lru.reshape(nbp, S, 512)[:, S - (LRU_CONV - 1):, :D_LRU],
                       glu.reshape(nbp, S, 256)[:, S - (CONF_KERNEL - 1):],
                       row.reshape(nbp, S, MLA_CACHE_W), nsa4.reshape(nbp, S, 4, NSA_HD),
                       win.reshape(nbp, S, 2, NSA_HD)[:, S - n_win:]))

        lru, glu, qm, row, nq, nsa4, win, gates = _in_proj(
            xs, ms[0], ms[1], P['g_pre_mix'], P['w_in'], P['mla_gq'], P['wuq'], P['wuk'], P['mla_gkv'],
            cos_s, sin_s, per_token=True, tm=tm_s)
        tmaj = lambda a: jnp.swapaxes(a, 0, 1)
        a_t, b_t, h_last, ncs, ngs = _seq_sample(
            tmaj(lru.reshape(nbs, T, 512)), tmaj(glu.reshape(nbs, T, 256)), tmaj(state_lru_conv[l]),
            tmaj(state_conv[l]), state_lru_h[l], P)
        c_out = _mla_sample(cache_mla_t, pt_flat, qm, row, P['wuv'], l, nbs, n_pages)
        d_out, win_t = _nsa_sample(cache_nsa_t, pt_flat, nq, gates, P['w4'], pe, ov_s, nsa4, win_cache_t, win, l, nbs,
                                   n_pages, nc=nc_s, n_sel=min(NSA_N_SEL, nblk_s), past=past)
        xs = _post(xs, tmaj(a_t).reshape(Ns, 256), tmaj(b_t).reshape(Ns, 256), c_out, d_out, ms[2], ms[3], ms[4],
                   ms[5], P['w_out'], P['g_post_mix'], P['g_pre_ffn'], P['g_post_ffn'], P['w_ff1'], P['w_ff2'],
                   per_token=True, tm=tm_s)
        win_s = jnp.transpose(win_t.reshape(nbs, 2, NSA_HD, win_t.shape[2]), (0, 3, 1, 2))
        outs_s.append((h_last, tmaj(ncs), tmaj(ngs), row.reshape(nbs, T, MLA_CACHE_W),
                       nsa4.reshape(nbs, T, 4, NSA_HD), win_s))

    stack = lambda outs, i: jnp.stack([o[i] for o in outs])
    return (xp.reshape(nbp, S, D), xs.reshape(nbs, T, D),
            stack(outs_p, 0), stack(outs_s, 0), stack(outs_p, 1), stack(outs_s, 1), stack(outs_p, 2), stack(outs_s, 2),
            stack(outs_p, 3), stack(outs_s, 3), stack(outs_p, 4), stack(outs_s, 4), stack(outs_p, 5), stack(outs_s, 5))
```

```python
import functools

import numpy as np
import jax
import jax.numpy as jnp
from jax import lax
from jax.experimental import pallas as pl
from jax.experimental.pallas import tpu as pltpu

F32 = jnp.float32
BF16 = jnp.bfloat16

D_MODEL = 1024
D_LRU = 256
LRU_BLOCKS = 4
LRU_CONV = 4
LRU_C = 8.0
D_CONF = 256
CONF_KERNEL = 31
MLA_HEADS = 4
MLA_NOPE = 64
MLA_ROPE = 32
MLA_V = 64
MLA_Q_LORA = 256
MLA_KV_LORA = 128
MLA_CACHE_W = MLA_KV_LORA + MLA_ROPE
ROPE_THETA = 10000.0
NSA_HEADS = 4
NSA_HD = 64
NSA_CMP_BLOCK = 32
NSA_CMP_STRIDE = 16
NSA_SEL_BLOCK = 64
NSA_N_SEL = 16
NSA_WINDOW = 512
NSA_FORCE = 1.0e4
D_FF = 4 * D_MODEL
EPS = 1e-6
IN_SPLITS = (D_LRU, D_LRU, 2 * D_CONF, MLA_Q_LORA, MLA_KV_LORA, MLA_ROPE, NSA_HEADS * NSA_HD, 6 * NSA_HD, 3 * NSA_HEADS)

C_LRU = 0
C_CONF = 512
C_CQ = 1024
C_CKV = 1280
C_NQ = 1408
C_NKV = 1664
C_MISC = 2048
D_INP = 2176
MISC_NG = 64

NEG = -1e30
LOG2E = 1.4426950408889634
VSUM_ROWS = 8
VMEM_LIMIT_V7X = 56 * 1024 * 1024
Q_TILE = 256
K_TILE = 512
KEY_GROUPS = 4
SAMPLE_R = 128


def _cparams(sem):
    return pltpu.CompilerParams(dimension_semantics=sem, vmem_limit_bytes=VMEM_LIMIT_V7X)


def _rms(x, g):
    return x * lax.rsqrt(jnp.mean(x * x, axis=-1, keepdims=True) + EPS) * g


def _bdot(a, b):
    return jnp.dot(a.astype(BF16), b.astype(BF16), preferred_element_type=F32)


def _dot_nt(a, b):
    return lax.dot_general(a.astype(BF16), b.astype(BF16), (((1,), (1,)), ((), ())), preferred_element_type=F32)


def _dot_tn(a, b):
    return lax.dot_general(a.astype(BF16), b.astype(BF16), (((0,), (0,)), ((), ())), preferred_element_type=F32)


def _mod_kernel(c_ref, w_ref, b_ref, o_ref):
    o_ref[...] = _bdot(jax.nn.silu(c_ref[...]), w_ref[...]) + b_ref[...]


def _modulation(c_all, w_mod, b_mod):
    L, D, D6 = w_mod.shape
    rows = c_all.shape[0]
    tn = 1536
    return pl.pallas_call(
        _mod_kernel,
        grid=(L, D6 // tn),
        in_specs=[pl.BlockSpec((rows, D), lambda l, j: (0, 0)),
                  pl.BlockSpec((None, D, tn), lambda l, j: (l, 0, j)),
                  pl.BlockSpec((None, 1, tn), lambda l, j: (l, 0, j))],
        out_specs=pl.BlockSpec((None, rows, tn), lambda l, j: (l, 0, j)),
        out_shape=jax.ShapeDtypeStruct((L, rows, D6), F32),
        compiler_params=_cparams(("arbitrary", "arbitrary")),
        name="modulation",
    )(c_all, w_mod, b_mod.reshape(L, 1, D6))


def _pe_kernel(pe_ref, w_ref, o_ref):
    acc = jnp.zeros((8, 256), F32)
    for r in range(NSA_CMP_STRIDE):
        acc = acc + _bdot(pe_ref[r], w_ref[r])
    o_ref[...] = jnp.broadcast_to(acc[0:1, 0:128] + acc[1:2, 128:256], (8, 128))


def _pe_term(pe2, w4):
    L = pe2.shape[0]
    return pl.pallas_call(
        _pe_kernel,
        grid=(L,),
        in_specs=[pl.BlockSpec((None, NSA_CMP_STRIDE, 8, 128), lambda l: (l, 0, 0, 0)),
                  pl.BlockSpec((None, NSA_CMP_STRIDE, 128, 256), lambda l: (l, 0, 0, 0))],
        out_specs=pl.BlockSpec((None, 8, 128), lambda l: (l, 0, 0)),
        out_shape=jax.ShapeDtypeStruct((L, 8, 128), F32),
        compiler_params=_cparams(("arbitrary",)),
        name="pe_term",
    )(pe2, w4)


def _in_proj_kernel(x_ref, sh_ref, sc_ref, g_ref, w_ref, gq_ref, wuq_ref, wuk_ref, gkv_ref, cos_ref, sin_ref,
                    lru_ref, glu_ref, qm_ref, row_ref, nq_ref, nsa4_ref, win_ref, gate_ref):
    h = _rms(x_ref[...], g_ref[...]) * (1.0 + sc_ref[...]) + sh_ref[...]
    u = _bdot(h, w_ref[...])
    lru_ref[...] = u[:, C_LRU:C_LRU + 512]
    glu_ref[...] = u[:, C_CONF:C_CONF + 256] * jax.nn.sigmoid(u[:, C_CONF + 256:C_CONF + 512])
    nsa4_ref[...] = u[:, C_NKV:C_NKV + 256]
    win_ref[...] = u[:, C_NKV + 256:C_NKV + 384]
    misc = u[:, C_MISC:C_MISC + 128]
    gate_ref[...] = jax.nn.sigmoid(misc)
    cosv = cos_ref[...]
    sinv = sin_ref[...]
    qr = _bdot(_rms(u[:, C_CQ:C_CQ + 256], gq_ref[...]), wuq_ref[...])
    q_rope = qr[:, 256:384] * cosv + qr[:, 384:512] * sinv
    q_lat = _bdot(qr[:, 0:256], wuk_ref[...])
    scale = (MLA_NOPE + MLA_ROPE) ** -0.5 * LOG2E
    for hd in range(MLA_HEADS):
        qm_ref[hd, :, 0:128] = q_lat[:, 128 * hd:128 * (hd + 1)] * scale
        qm_ref[hd, :, 128:160] = q_rope[:, 32 * hd:32 * (hd + 1)] * scale
    row_ref[:, 0:128] = _rms(u[:, C_CKV:C_CKV + 128], gkv_ref[...])
    row_ref[:, 128:160] = misc[:, 0:32] * cosv[:, 0:32] + misc[:, 32:64] * sinv[:, 0:32]
    nq = u[:, C_NQ:C_NQ + 256] * (NSA_HD ** -0.5 * LOG2E)
    for hd in range(NSA_HEADS):
        nq_ref[hd] = nq[:, 64 * hd:64 * (hd + 1)]


def _in_proj(x, sh, sc, g, w_in, gq, wuq, wuk, gkv, cosv, sinv, *, per_token, tm):
    N, D = x.shape
    nt = N // tm
    if per_token:
        mod_spec = pl.BlockSpec((tm, D), lambda i: (i, 0))
        tab_spec = pl.BlockSpec((tm, 128), lambda i: (i, 0))
    else:
        per_b = nt // sh.shape[0]
        mod_spec = pl.BlockSpec((None, 1, D), lambda i: (i // per_b, 0, 0))
        tab_spec = pl.BlockSpec((tm, 128), lambda i: (i % per_b, 0))
    const = lambda shape: pl.BlockSpec(shape, lambda i: tuple(0 for _ in shape))
    row = lambda w: pl.BlockSpec((tm, w), lambda i: (i, 0))
    head = lambda w: pl.BlockSpec((4, tm, w), lambda i: (0, i, 0))
    return pl.pallas_call(
        _in_proj_kernel,
        grid=(nt,),
        in_specs=[row(D), mod_spec, mod_spec, const((1, D)), const((D, D_INP)), const((1, 256)),
                  const((256, 512)), const((256, 512)), const((1, 128)), tab_spec, tab_spec],
        out_specs=[row(512), row(256), head(MLA_CACHE_W), row(MLA_CACHE_W), head(NSA_HD), row(256), row(128), row(128)],
        out_shape=[jax.ShapeDtypeStruct((N, 512), F32), jax.ShapeDtypeStruct((N, 256), F32),
                   jax.ShapeDtypeStruct((4, N, MLA_CACHE_W), F32), jax.ShapeDtypeStruct((N, MLA_CACHE_W), F32),
                   jax.ShapeDtypeStruct((4, N, NSA_HD), F32), jax.ShapeDtypeStruct((N, 256), F32),
                   jax.ShapeDtypeStruct((N, 128), F32), jax.ShapeDtypeStruct((N, 128), F32)],
        compiler_params=_cparams(("parallel",)),
        name="in_proj",
    )(x, sh, sc, g, w_in, gq, wuq, wuk, gkv, cosv, sinv)


def _post_kernel(x_ref, a_ref, b_ref, c_ref, d_ref, gt1_ref, sh2_ref, sc2_ref, gt2_ref,
                 wo_ref, gpm_ref, gpf_ref, gqf_ref, w1_ref, w2_ref, o_ref):
    y = _bdot(a_ref[...], wo_ref[0:256, :])
    y = y + _bdot(b_ref[...], wo_ref[256:512, :])
    y = y + _bdot(c_ref[...], wo_ref[512:768, :])
    y = y + _bdot(d_ref[...], wo_ref[768:1024, :])
    x = x_ref[...] + gt1_ref[...] * _rms(y, gpm_ref[...])
    h = (_rms(x, gpf_ref[...]) * (1.0 + sc2_ref[...]) + sh2_ref[...]).astype(BF16)
    f = jnp.zeros(x.shape, F32)
    fc = 1024
    for j in range(D_FF // fc):
        t = jnp.maximum(jnp.dot(h, w1_ref[:, j * fc:(j + 1) * fc], preferred_element_type=F32), 0.0)
        f = f + _bdot(t * t, w2_ref[j * fc:(j + 1) * fc, :])
    o_ref[...] = x + gt2_ref[...] * _rms(f, gqf_ref[...])


def _post(x, a, b, c, d, gt1, sh2, sc2, gt2, wo, gpm, gpf, gqf, w1, w2, *, per_token, tm):
    N, D = x.shape
    nt = N // tm
    if per_token:
        mod_spec = pl.BlockSpec((tm, D), lambda i: (i, 0))
    else:
        per_b = nt // gt1.shape[0]
        mod_spec = pl.BlockSpec((None, 1, D), lambda i: (i // per_b, 0, 0))
    const = lambda shape: pl.BlockSpec(shape, lambda i: tuple(0 for _ in shape))
    row = lambda w: pl.BlockSpec((tm, w), lambda i: (i, 0))
    return pl.pallas_call(
        _post_kernel,
        grid=(nt,),
        in_specs=[row(D), row(256), row(256), row(256), row(256), mod_spec, mod_spec, mod_spec, mod_spec,
                  const((D, D)), const((1, D)), const((1, D)), const((1, D)), const((D, D_FF)), const((D_FF, D))],
        out_specs=row(D),
        out_shape=jax.ShapeDtypeStruct((N, D), F32),
        compiler_params=_cparams(("parallel",)),
        name="post",
    )(x, a, b, c, d, gt1, sh2, sc2, gt2, wo, gpm, gpf, gqf, w1, w2)


def _lru_gates(y, gate_w, lam):
    wa, ba, wx, bx = gate_w
    r = jax.nn.sigmoid(_bdot(y, wa) + ba)
    i = jax.nn.sigmoid(_bdot(y, wx) + bx)
    log_a = -LRU_C * r * jax.nn.softplus(-lam)
    return jnp.exp(log_a), jnp.sqrt(1.0 - jnp.exp(2.0 * log_a)) * i * y


def _layernorm(y, g, b):
    mu = jnp.mean(y, axis=-1, keepdims=True)
    var = jnp.mean(jnp.square(y - mu), axis=-1, keepdims=True)
    return (y - mu) * lax.rsqrt(var + EPS) * g + b


def _seq_prompt_kernel(xg_ref, glu_ref, cw_ref, cb_ref, wa_ref, ba_ref, wx_ref, bx_ref, lam_ref,
                       dw_ref, db_ref, lng_ref, lnb_ref,
                       a_ref, b_ref, hl_ref, xbuf, gbuf, abuf, dbuf, hbuf, sbuf):
    i = pl.program_id(0)
    nb, tc, _ = glu_ref.shape

    @pl.when(i == 0)
    def _():
        xbuf[:, 0:8, :] = jnp.zeros((nb, 8, D_LRU), F32)
        gbuf[:, 0:32, :] = jnp.zeros((nb, 32, D_CONF), F32)
        hbuf[...] = jnp.zeros(hbuf.shape, F32)

    xbuf[:, 8:8 + tc, :] = xg_ref[:, :, 0:D_LRU]
    y = jnp.zeros((nb, tc, D_LRU), F32) + cb_ref[...]
    for k in range(LRU_CONV):
        y = y + xbuf[:, 8 - (LRU_CONV - 1) + k:8 - (LRU_CONV - 1) + k + tc, :] * cw_ref[k:k + 1, :]
    a, drive = _lru_gates(y.reshape(nb * tc, D_LRU),
                          (wa_ref[...], ba_ref[...], wx_ref[...], bx_ref[...]), lam_ref[...])
    abuf[...] = a.reshape(nb, tc, D_LRU)
    dbuf[...] = drive.reshape(nb, tc, D_LRU)

    row = lax.broadcasted_iota(jnp.int32, (nb * 8, D_LRU), 0) % 8

    def tile_step(j, h):
        o = pl.multiple_of(j * 8, 8)
        a8 = abuf[:, pl.ds(o, 8), :].reshape(nb * 8, D_LRU)
        d8 = dbuf[:, pl.ds(o, 8), :].reshape(nb * 8, D_LRU)
        for s in (1, 2, 4):
            d8 = a8 * jnp.where(row >= s, pltpu.roll(d8, s, 0), 0.0) + d8
            a8 = a8 * jnp.where(row >= s, pltpu.roll(a8, s, 0), 1.0)
        hs = a8.reshape(nb, 8, D_LRU) * h + d8.reshape(nb, 8, D_LRU)
        dbuf[:, pl.ds(o, 8), :] = hs
        return hs[:, 7:8, :]

    h = lax.fori_loop(0, tc // 8, tile_step, hbuf[...], unroll=2)
    hbuf[...] = h
    hl_ref[...] = jnp.broadcast_to(h, hl_ref.shape)
    a_ref[...] = dbuf[...] * jax.nn.gelu(xg_ref[:, :, D_LRU:2 * D_LRU])
    xbuf[:, 0:8, :] = xbuf[:, tc:tc + 8, :]

    gbuf[:, 32:32 + tc, :] = glu_ref[...]
    z = jnp.zeros((nb, tc, D_CONF), F32) + db_ref[...]
    first = 32 - (CONF_KERNEL - 1)
    for r in range(8):
        taps = [k for k in range(CONF_KERNEL) if (first + k) % 8 == r]
        rows = first + taps[-1] + tc - r
        sbuf[:, 0:rows, :] = gbuf[:, r:r + rows, :]
        for k in taps:
            a = first + k - r
            z = z + sbuf[:, a:a + tc, :] * dw_ref[k:k + 1, :]
    b_ref[...] = jax.nn.silu(_layernorm(z, lng_ref[...], lnb_ref[...]))
    gbuf[:, 0:32, :] = gbuf[:, tc:tc + 32, :]


def _seq_prompt(xg, glu, P, *, tc):
    nb, S, _ = glu.shape
    const = lambda shape: pl.BlockSpec(shape, lambda i: tuple(0 for _ in shape))
    seq = lambda w: pl.BlockSpec((nb, tc, w), lambda i: (0, i, 0))
    return pl.pallas_call(
        _seq_prompt_kernel,
        grid=(S // tc,),
        in_specs=[seq(512), seq(256), const((LRU_CONV, 256)), const((1, 256)), const((256, 256)), const((1, 256)),
                  const((256, 256)), const((1, 256)), const((1, 256)), const((CONF_KERNEL, 256)), const((1, 256)),
                  const((1, 256)), const((1, 256))],
        out_specs=[seq(256), seq(256), const((nb, 8, 256))],
        out_shape=[jax.ShapeDtypeStruct((nb, S, 256), F32), jax.ShapeDtypeStruct((nb, S, 256), F32),
                   jax.ShapeDtypeStruct((nb, 8, 256), F32)],
        scratch_shapes=[pltpu.VMEM((nb, tc + 8, 256), F32), pltpu.VMEM((nb, tc + 32, 256), F32),
                        pltpu.VMEM((nb, tc, 256), F32), pltpu.VMEM((nb, tc, 256), F32),
                        pltpu.VMEM((nb, 1, 256), F32), pltpu.VMEM((nb, tc + 32, 256), F32)],
        compiler_params=_cparams(("arbitrary",)),
        name="seq_prompt",
    )(xg, glu, P['lru_conv_w'], P['lru_conv_b'], P['lru_wa_bd'], P['lru_ba'], P['lru_wx_bd'], P['lru_bx'],
      P['lru_lam'], P['conf_dw_w'], P['conf_dw_b'], P['conf_ln_g'], P['conf_ln_b'])


def _seq_sample_kernel(xg_ref, glu_ref, cs_ref, gs_ref, h0_ref, cw_ref, cb_ref, wa_ref, ba_ref, wx_ref, bx_ref,
                       lam_ref, dw_ref, db_ref, lng_ref, lnb_ref,
                       a_ref, b_ref, hl_ref, ncs_ref, ngs_ref, xbuf, gbuf):
    T, nb, _ = glu_ref.shape
    nc, ng = LRU_CONV - 1, CONF_KERNEL - 1
    xbuf[0:nc] = cs_ref[...]
    xbuf[nc:nc + T] = xg_ref[:, :, 0:D_LRU]
    y = jnp.zeros((T, nb, D_LRU), F32) + cb_ref[...]
    for k in range(LRU_CONV):
        y = y + xbuf[k:k + T] * cw_ref[k:k + 1, :]
    a, drive = _lru_gates(y.reshape(T * nb, D_LRU),
                          (wa_ref[...], ba_ref[...], wx_ref[...], bx_ref[...]), lam_ref[...])
    a = a.reshape(T, nb, D_LRU)
    drive = drive.reshape(T, nb, D_LRU)
    h = h0_ref[...]
    for t in range(T):
        h = a[t] * h + drive[t]
        a_ref[t] = h * jax.nn.gelu(xg_ref[t, :, D_LRU:2 * D_LRU])
    hl_ref[...] = h
    ncs_ref[...] = xbuf[T:T + nc]

    gbuf[0:ng] = gs_ref[...]
    gbuf[ng:ng + T] = glu_ref[...]
    z = jnp.zeros((T, nb, D_CONF), F32) + db_ref[...]
    for k in range(CONF_KERNEL):
        z = z + gbuf[k:k + T] * dw_ref[k:k + 1, :]
    b_ref[...] = jax.nn.silu(_layernorm(z, lng_ref[...], lnb_ref[...]))
    ngs_ref[...] = gbuf[T:T + ng]


def _seq_sample(xg_t, glu_t, cs_t, gs_t, h0, P):
    T, nb, _ = glu_t.shape
    nc, ng = LRU_CONV - 1, CONF_KERNEL - 1
    full = lambda a: pl.BlockSpec(a.shape, lambda i: tuple(0 for _ in a.shape))
    args = (xg_t, glu_t, cs_t, gs_t, h0, P['lru_conv_w'], P['lru_conv_b'], P['lru_wa_bd'], P['lru_ba'],
            P['lru_wx_bd'], P['lru_bx'], P['lru_lam'], P['conf_dw_w'], P['conf_dw_b'], P['conf_ln_g'], P['conf_ln_b'])
    outs = [jax.ShapeDtypeStruct((T, nb, 256), F32), jax.ShapeDtypeStruct((T, nb, 256), F32),
            jax.ShapeDtypeStruct((nb, 256), F32), jax.ShapeDtypeStruct((nc, nb, 256), F32),
            jax.ShapeDtypeStruct((ng, nb, 256), F32)]
    return pl.pallas_call(
        _seq_sample_kernel,
        grid=(1,),
        in_specs=[full(a) for a in args],
        out_specs=[full(o) for o in outs],
        out_shape=outs,
        scratch_shapes=[pltpu.VMEM((T + nc, nb, 256), F32), pltpu.VMEM((T + ng, nb, 256), F32)],
        compiler_params=_cparams(("arbitrary",)),
        name="seq_sample",
    )(*args)


def _softmax_stages(vt_ref, m_ref, acc_ref):
    def soft(s_ref, p_ref, a_ref):
        s = s_ref[...]
        m_prev = m_ref[...]
        m_new = jnp.maximum(m_prev, jnp.max(s, axis=0, keepdims=True))
        p_ref[...] = jnp.exp2(s - m_new).astype(BF16)
        a_ref[...] = jnp.exp2(m_prev - m_new)
        m_ref[...] = m_new

    def pv(p_ref, a_ref, tile):
        acc_ref[...] = a_ref[...] * acc_ref[...] + jnp.dot(vt_ref[tile], p_ref[...], preferred_element_type=F32)

    return soft, pv


def _softmax_pv(s, v_t):
    p = jnp.exp2(s - jnp.max(s, axis=0, keepdims=True))
    acc = jnp.dot(v_t, p.astype(BF16), preferred_element_type=F32)
    dv = acc.shape[0] - VSUM_ROWS
    return acc[0:dv] / jnp.maximum(acc[dv:dv + 1], 1e-30)


def _init_stats(m_ref, acc_ref, l_ref=None):
    m_ref[...] = jnp.full(m_ref.shape, NEG, F32)
    acc_ref[...] = jnp.zeros(acc_ref.shape, F32)
    if l_ref is not None:
        l_ref[...] = jnp.zeros(l_ref.shape, F32)


def _finish_t(acc_ref):
    dv = acc_ref.shape[0] - VSUM_ROWS
    return acc_ref[0:dv, :] / jnp.maximum(acc_ref[dv:dv + 1, :], 1e-30)


def _finish(l_ref, acc_ref):
    return acc_ref[...] / jnp.maximum(l_ref[...], 1e-30)


def _compressed_kv(p_ref, pe_ref, nc):
    ncp = p_ref.shape[0]
    top = p_ref[:, 0:128]
    bot = pltpu.roll(p_ref[:, 128:256], ncp - 1, 0)
    n = lax.broadcasted_iota(jnp.int32, (ncp, 128), 0)
    return jnp.where(n < nc, top + bot + pe_ref[0:1, :], 0.0)


def _compressed_branch(q, kcv, q_pos, nc):
    ncp = kcv.shape[0]
    s = _dot_nt(kcv[:, 0:64], q)
    n = lax.broadcasted_iota(jnp.int32, (ncp, 1), 0)
    last_row = jnp.where(n < nc, n * NSA_CMP_STRIDE + (NSA_CMP_BLOCK - 1), 2 ** 30)
    mask = last_row <= q_pos
    s = jnp.where(mask, s, NEG)
    e = jnp.where(mask, jnp.exp2(s - jnp.max(s, axis=0, keepdims=True)), 0.0)
    p = e / jnp.maximum(jnp.sum(e, axis=0, keepdims=True), 1e-30)
    o = _dot_tn(kcv[:, 64:128], p)
    return o, p


def _select_blocks(imp, q_pos, n_sel, q_pos_min=None):
    assert n_sel >= 3 and NSA_FORCE > 2 * NSA_HEADS
    blk = lax.broadcasted_iota(jnp.int32, imp.shape, 0)
    cur = q_pos // NSA_SEL_BLOCK
    valid = blk * NSA_SEL_BLOCK <= q_pos
    forced = (blk == 0) | (blk == cur) | (blk == cur - 1)
    n_forced = jnp.minimum(cur + 1, 3)
    bias = jnp.where(forced, 0.0, NEG)
    score = jnp.where(forced, -3e38, jnp.where(valid, imp, -1.0))
    blk_f = blk.astype(F32)
    always_three = q_pos_min is not None and q_pos_min // NSA_SEL_BLOCK >= 2
    for e in range(n_sel - (3 if always_three else 1)):
        mx = jnp.max(score, axis=0, keepdims=True)
        first = jnp.min(jnp.where(score == mx, blk_f, 1e9), axis=0, keepdims=True)
        if e >= n_sel - 3:
            first = jnp.where(n_forced + e < n_sel, first, -1.0)
        hit = blk_f == first
        bias = jnp.where(hit, 0.0, bias)
        score = jnp.where(hit, -3e38, score)
    return bias


def _cmp_prompt_kernel(x_ref, w_ref, p_ref):
    n_chunk = p_ref.shape[0]
    acc = jnp.zeros(p_ref.shape, F32)
    for r in range(NSA_CMP_STRIDE):
        acc = acc + _bdot(x_ref[pl.ds(r, n_chunk, stride=NSA_CMP_STRIDE), :], w_ref[r])
    p_ref[...] = acc


def _cmp_prompt(nsa4, w4, nb):
    N = nsa4.shape[0]
    S = N // nb
    n_chunk = S // NSA_CMP_STRIDE
    return pl.pallas_call(
        _cmp_prompt_kernel,
        grid=(nb,),
        in_specs=[pl.BlockSpec((S, 128), lambda b: (b, 0)),
                  pl.BlockSpec((NSA_CMP_STRIDE, 128, 256), lambda b: (0, 0, 0))],
        out_specs=pl.BlockSpec((None, n_chunk, 256), lambda b: (b, 0, 0)),
        out_shape=jax.ShapeDtypeStruct((nb, n_chunk, 256), F32),
        compiler_params=_cparams(("parallel",)),
        name="cmp_prompt",
    )(nsa4, w4)


def _attn_prompt_kernel(qm_ref, km_ref, vmt_ref, wuvt_ref, q_ref, g_ref, p_ref, pe_ref, ov_ref, ks_ref, vst_ref,
                        kw_ref, vwt_ref, c_ref, o_ref,
                        m_ref, acc_ref, mm_ref, accm_ref, ow_ref, *stage, nc, n_sel):
    stage_refs = [stage[3 * n:3 * n + 3] for n in range(4)]
    i = pl.program_id(1)
    _, tq, _ = q_ref.shape
    tk = ks_ref.shape[1]
    tw = kw_ref.shape[1]
    R = NSA_HEADS * tq
    q_f32 = q_ref[...].reshape(R, NSA_HD)
    q = q_f32.astype(BF16)
    qm = qm_ref[...].reshape(R, MLA_CACHE_W).astype(BF16)
    q_pos_q = i * tq + lax.broadcasted_iota(jnp.int32, (1, tq), 1)
    q_pos = jnp.concatenate([q_pos_q] * NSA_HEADS, axis=1)

    n_wt = NSA_WINDOW // tw
    kw = jnp.concatenate([kw_ref[i + c] for c in range(n_wt + 1)], axis=0)
    vw_t = jnp.concatenate([vwt_ref[i + c] for c in range(n_wt + 1)], axis=1)
    k_pos = (i - n_wt) * tw + lax.broadcasted_iota(jnp.int32, (kw.shape[0], R), 0)
    first_pos = jnp.maximum(q_pos - (NSA_WINDOW - 1), 0)
    s_w = jnp.where(k_pos <= q_pos, jnp.where(k_pos >= first_pos, _dot_nt(kw, q), NEG), NEG)
    ow_ref[...] = _softmax_pv(s_w, vw_t)

    kcv = _compressed_kv(p_ref, pe_ref, nc)
    o_c, p_c = _compressed_branch(q, kcv, q_pos, nc)
    imp4 = jnp.dot(ov_ref[...], p_c.astype(BF16), preferred_element_type=F32)
    imp = imp4[:, 0:tq]
    for hd in range(1, NSA_HEADS):
        imp = imp + imp4[:, hd * tq:(hd + 1) * tq]
    bias = _select_blocks(imp, q_pos_q, n_sel).T
    q_sel = jnp.concatenate([jnp.concatenate([bias] * NSA_HEADS, axis=0), q_f32], axis=1).astype(BF16)

    n_full = (i * tq) // tk
    _init_stats(m_ref, acc_ref)
    _init_stats(mm_ref, accm_ref)

    def score_m(j):
        return _dot_nt(km_ref[j], qm)

    def score_s(j):
        return _dot_nt(ks_ref[j], q_sel)

    def full_tile(j):
        return jnp.clip(j, 0, jnp.maximum(n_full - 1, 0))

    soft_m, pv_m = _softmax_stages(vmt_ref, mm_ref, accm_ref)
    soft_s, pv_s = _softmax_stages(vst_ref, m_ref, acc_ref)
    (sma, pma, ama), (smb, pmb, amb), (ssa, psa, asa), (ssb, psb, asb) = stage_refs

    causal = n_full * tk + lax.broadcasted_iota(jnp.int32, (tk, R), 0) <= q_pos
    sma[...] = jnp.where(causal, score_m(n_full), NEG)
    ssa[...] = jnp.where(causal, score_s(n_full), NEG)
    smb[...] = score_m(full_tile(0))
    ssb[...] = score_s(full_tile(0))
    soft_m(sma, pma, ama)
    soft_s(ssa, psa, asa)

    def pair(jj, c):
        k = 2 * jj + 1
        prev = jnp.where(jj == 0, n_full, k - 2)
        sma[...] = score_m(full_tile(k))
        ssa[...] = score_s(full_tile(k))
        soft_m(smb, pmb, amb)
        soft_s(ssb, psb, asb)
        pv_m(pma, ama, prev)
        pv_s(psa, asa, prev)
        smb[...] = score_m(full_tile(k + 1))
        ssb[...] = score_s(full_tile(k + 1))
        soft_m(sma, pma, ama)
        soft_s(ssa, psa, asa)
        pv_m(pmb, amb, k - 1)
        pv_s(psb, asb, k - 1)
        return c

    n_pair = n_full // 2
    lax.fori_loop(0, n_pair, pair, 0)
    last = jnp.where(n_pair == 0, n_full, 2 * n_pair - 1)
    pv_m(pma, ama, last)
    pv_s(psa, asa, last)

    @pl.when(n_full % 2 == 1)
    def _():
        soft_m(smb, pmb, amb)
        soft_s(ssb, psb, asb)
        pv_m(pmb, amb, n_full - 1)
        pv_s(psb, asb, n_full - 1)

    o_s = _finish_t(acc_ref)
    o_m = _finish_t(accm_ref)
    c_t = jnp.concatenate([_bdot(wuvt_ref[hd], o_m[:, hd * tq:(hd + 1) * tq]) for hd in range(MLA_HEADS)], axis=0)
    c_ref[...] = c_t.T
    o_w = ow_ref[...]

    g_t = g_ref[...].T
    outs = []
    for hd in range(NSA_HEADS):
        sl = slice(hd * tq, (hd + 1) * tq)
        r0 = MISC_NG + 3 * hd
        outs.append(g_t[r0:r0 + 1, :] * o_c[:, sl] + g_t[r0 + 1:r0 + 2, :] * o_s[:, sl]
                    + g_t[r0 + 2:r0 + 3, :] * o_w[:, sl])
    o_ref[...] = jnp.concatenate(outs, axis=0).T


def _attn_prompt(qm, km3, vmt3, wuvt, q, gates, P, pe, ov, ks3, vst3, kw3, vwt3, *, tq, nc, n_sel):
    nb, nk, tk, dks = ks3.shape
    nw, tw = kw3.shape[1:3]
    assert tw == tq
    N = q.shape[1]
    nq = N // nb // tq
    ncp = P.shape[1]
    nblk = ov.shape[0]
    R = 4 * tq
    out = pl.BlockSpec((tq, 256), lambda b, i: (b * nq + i, 0))
    whole = lambda a: pl.BlockSpec((None,) + a.shape[1:], lambda b, i: (b,) + (0,) * (a.ndim - 1))
    return pl.pallas_call(
        functools.partial(_attn_prompt_kernel, nc=nc, n_sel=n_sel),
        grid=(nb, nq),
        in_specs=[pl.BlockSpec((4, tq, MLA_CACHE_W), lambda b, i: (0, b * nq + i, 0)),
                  whole(km3), whole(vmt3),
                  pl.BlockSpec((4, MLA_V, MLA_KV_LORA), lambda b, i: (0, 0, 0)),
                  pl.BlockSpec((4, tq, NSA_HD), lambda b, i: (0, b * nq + i, 0)),
                  pl.BlockSpec((tq, 128), lambda b, i: (b * nq + i, 0)),
                  pl.BlockSpec((None, ncp, 256), lambda b, i: (b, 0, 0)),
                  pl.BlockSpec((8, 128), lambda b, i: (0, 0)),
                  pl.BlockSpec((nblk, ncp), lambda b, i: (0, 0)),
                  whole(ks3), whole(vst3), whole(kw3), whole(vwt3)],
        out_specs=[out, out],
        out_shape=[jax.ShapeDtypeStruct((N, 256), F32), jax.ShapeDtypeStruct((N, 256), F32)],
        scratch_shapes=[pltpu.VMEM((1, R), F32), pltpu.VMEM((NSA_HD + VSUM_ROWS, R), F32),
                        pltpu.VMEM((1, R), F32), pltpu.VMEM((MLA_KV_LORA + VSUM_ROWS, R), F32)]
        + [pltpu.VMEM((NSA_HD, R), F32)]
        + [pltpu.VMEM((tk, R), F32), pltpu.VMEM((tk, R), BF16), pltpu.VMEM((1, R), F32)] * 4,
        compiler_params=_cparams(("arbitrary", "arbitrary")),
        name="attn_prompt",
    )(qm, km3, vmt3, wuvt, q, gates, P, pe, ov, ks3, vst3, kw3, vwt3)


def _sample_queries(q_ref):
    nh, T, d = q_ref.shape
    q = q_ref[...].reshape(nh * T, d)
    return jnp.concatenate([q, jnp.zeros((SAMPLE_R - nh * T, d), F32)], axis=0).astype(BF16)


def _rows_attend(parts):
    stats = []
    for s, pv in parts:
        m_i = jnp.max(s, axis=1, keepdims=True)
        p = jnp.exp2(s - m_i)
        stats.append((m_i, jnp.sum(p, axis=1, keepdims=True), pv(p.astype(BF16))))
    m = functools.reduce(jnp.maximum, [m_i for m_i, _, _ in stats])
    num, den = 0.0, 0.0
    for m_i, l_i, o_i in stats:
        w = jnp.exp2(m_i - m)
        den = den + w * l_i
        num = num + w * o_i
    return num / jnp.maximum(den, 1e-30)


def _page_groups(n_pages):
    size = max(n_pages // KEY_GROUPS, 1)
    return [range(a, min(a + size, n_pages)) for a in range(0, n_pages, size)]


def _page_copy(cache_ref, pt_ref, buf_ref, sem_ref, layer, n_pages, b, slot, j):
    return pltpu.make_async_copy(cache_ref.at[layer, pt_ref[b * n_pages + j]], buf_ref.at[slot, j], sem_ref.at[slot])


def _fetch_pages(cache_ref, pt_ref, buf_ref, sem_ref, layer, n_pages):
    b = pl.program_id(0)
    slot = b % 2
    copy = functools.partial(_page_copy, cache_ref, pt_ref, buf_ref, sem_ref, layer, n_pages)

    @pl.when(b == 0)
    def _():
        for j in range(n_pages):
            copy(b, slot, j).start()

    @pl.when(b + 1 < pl.num_programs(0))
    def _():
        for j in range(n_pages):
            copy(b + 1, 1 - slot, j).start()

    for j in range(n_pages):
        copy(b, slot, j).wait()
    return slot


def _mla_sample_kernel(pt_ref, cache_ref, q_ref, new_ref, wuv_ref, o_ref, buf_ref, sem_ref, *, layer, n_pages):
    slot = _fetch_pages(cache_ref, pt_ref, buf_ref, sem_ref, layer, n_pages)
    _, T, dk = q_ref.shape
    R = MLA_HEADS * T
    q = q_ref[...].reshape(R, dk).astype(BF16)
    parts = []
    for pages in _page_groups(n_pages):
        kt = jnp.concatenate([buf_ref[slot, j].astype(BF16) for j in pages], axis=1)
        parts.append((jnp.dot(q, kt, preferred_element_type=F32),
                      lambda p, vt=kt[0:MLA_KV_LORA]: _dot_nt(p, vt)))
    new = new_ref[...].astype(BF16)
    k_t = lax.broadcasted_iota(jnp.int32, (R, T), 1)
    q_t = lax.broadcasted_iota(jnp.int32, (R, T), 0) % T
    parts.append((jnp.where(k_t <= q_t, _dot_nt(q, new), NEG),
                  lambda p: jnp.dot(p, new[:, 0:MLA_KV_LORA], preferred_element_type=F32)))
    o = _rows_attend(parts)
    o_ref[...] = jnp.concatenate([_bdot(o[hd * T:(hd + 1) * T, :], wuv_ref[hd]) for hd in range(MLA_HEADS)], axis=1)


def _mla_sample(cache_mla_t, pt_flat, q, rows, wuv, layer, nb, n_pages):
    T = q.shape[1] // nb
    N = q.shape[1]
    page = cache_mla_t.shape[3]
    grid_spec = pltpu.PrefetchScalarGridSpec(
        num_scalar_prefetch=1,
        grid=(nb,),
        in_specs=[pl.BlockSpec(memory_space=pl.ANY),
                  pl.BlockSpec((4, T, MLA_CACHE_W), lambda b, pt: (0, b, 0)),
                  pl.BlockSpec((T, MLA_CACHE_W), lambda b, pt: (b, 0)),
                  pl.BlockSpec((4, MLA_KV_LORA, MLA_V), lambda b, pt: (0, 0, 0))],
        out_specs=pl.BlockSpec((T, 256), lambda b, pt: (b, 0)),
        scratch_shapes=[pltpu.VMEM((2, n_pages, MLA_CACHE_W, page), F32), pltpu.SemaphoreType.DMA((2,))],
    )
    return pl.pallas_call(
        functools.partial(_mla_sample_kernel, layer=layer, n_pages=n_pages),
        grid_spec=grid_spec,
        out_shape=jax.ShapeDtypeStruct((N, 256), F32),
        compiler_params=_cparams(("arbitrary",)),
        name="mla_sample",
    )(pt_flat, cache_mla_t, q, rows, wuv)


def _nsa_sample_kernel(pt_ref, cache_ref, q_ref, g_ref, w4_ref, perm_ref, pe_ref, ov_ref, e_ref, new_ref, wc_ref,
                       wn_ref, wnt_ref, o_ref, wo_ref, buf_ref, sem_ref, y_ref, *, layer, n_pages, nc, n_sel, past):
    slot = _fetch_pages(cache_ref, pt_ref, buf_ref, sem_ref, layer, n_pages)
    _, T, _ = q_ref.shape
    R = NSA_HEADS * T
    half = n_pages // 2
    cpp = y_ref.shape[1] // NSA_CMP_STRIDE
    q_f32 = q_ref[...].reshape(R, NSA_HD)
    q = q_f32.astype(BF16)
    q_pos = past + lax.broadcasted_iota(jnp.int32, (R, 1), 0) % T

    for jp in range(half):
        pair = jnp.concatenate([buf_ref[slot, jp, 0:128, :], buf_ref[slot, jp + half, 0:128, :]], axis=0)
        y_ref[jp] = _dot_nt(perm_ref[...], pair)
    slabs = []
    for r in range(NSA_CMP_STRIDE):
        yr = y_ref[:, r * cpp:(r + 1) * cpp, :].reshape(half * cpp, 256)
        slabs.append(jnp.concatenate([yr[:, 0:128], yr[:, 128:256]], axis=0))
    P = _bdot(jnp.concatenate(slabs, axis=1), w4_ref[...].reshape(NSA_CMP_STRIDE * 128, 256))

    lane = lax.broadcasted_iota(jnp.int32, (1, SAMPLE_R), 1)
    q_pos_l = past + lane % T
    kcv = _compressed_kv(P, pe_ref, nc)
    o_c, p_c = _compressed_branch(_sample_queries(q_ref), kcv, q_pos_l, nc)
    o_c = o_c.T[0:R]
    imp = jnp.dot(ov_ref[...], p_c.astype(BF16), preferred_element_type=F32)
    imp = jnp.where(lane < R, imp, 0.0)
    tot = imp
    for k in range(1, NSA_HEADS):
        tot = tot + pltpu.roll(imp, k * T, 1) + pltpu.roll(imp, SAMPLE_R - R + k * T, 1)
    n_real = -(-(-(-(past + T) // NSA_SEL_BLOCK)) // 8) * 8
    bias = jnp.concatenate([_select_blocks(tot[0:n_real], q_pos_l, n_sel, q_pos_min=past),
                            jnp.full((tot.shape[0] - n_real, SAMPLE_R), NEG, F32)], axis=0).T[0:R]

    n_blk_past = e_ref.shape[0]
    q_sel = jnp.concatenate([bias[:, 0:n_blk_past], q_f32], axis=1).astype(BF16)
    page = buf_ref.shape[3]
    parts = []
    for pages in _page_groups(n_pages):
        kvt = jnp.concatenate([buf_ref[slot, j, 128:256, :].astype(BF16) for j in pages], axis=1)
        k_sel = jnp.concatenate([e_ref[:, pages[0] * page:(pages[-1] + 1) * page], kvt[0:64]], axis=0)
        parts.append((jnp.dot(q_sel, k_sel, preferred_element_type=F32), lambda p, vt=kvt[64:128]: _dot_nt(p, vt)))
    new = new_ref[...].astype(BF16)
    k_new = past + lax.broadcasted_iota(jnp.int32, (R, T), 1)
    s_new = _dot_nt(q, new[:, 128:192]) + bias[:, n_blk_past:n_blk_past + 1]
    parts.append((jnp.where(k_new <= q_pos, s_new, NEG),
                  lambda p: jnp.dot(p, new[:, 192:256], preferred_element_type=F32)))
    o_s = _rows_attend(parts)

    wc = wc_ref[...].astype(BF16)
    n_buf = wc.shape[1]
    k_buf = past - n_buf + lax.broadcasted_iota(jnp.int32, (R, n_buf), 1)
    ok_buf = (k_buf >= 0) & (k_buf <= q_pos) & (k_buf > q_pos - NSA_WINDOW)
    wn = wn_ref[...].astype(BF16)
    ok_new = (k_new <= q_pos) & (k_new > q_pos - NSA_WINDOW)
    o_w = _rows_attend([
        (jnp.where(ok_buf, jnp.dot(q, wc[0:64], preferred_element_type=F32), NEG), lambda p: _dot_nt(p, wc[64:128])),
        (jnp.where(ok_new, _dot_nt(q, wn[:, 0:64]), NEG),
         lambda p: jnp.dot(p, wn[:, 64:128], preferred_element_type=F32))])

    g = g_ref[...]
    outs = []
    for hd in range(NSA_HEADS):
        sl = slice(hd * T, (hd + 1) * T)
        r0 = MISC_NG + 3 * hd
        outs.append(g[:, r0:r0 + 1] * o_c[sl] + g[:, r0 + 1:r0 + 2] * o_s[sl] + g[:, r0 + 2:r0 + 3] * o_w[sl])
    o_ref[...] = jnp.concatenate(outs, axis=1)

    n_win = wo_ref.shape[1]
    keep = n_win - T
    wo_ref[:, 0:keep] = wc_ref[:, n_buf - keep:n_buf]
    wo_ref[:, keep:n_win] = wnt_ref[...]


def _nsa_sample(cache_nsa_t, pt_flat, q, gates, w4, pe, ov, new4, win_cache_t, win_new, layer, nb, n_pages,
                *, nc, n_sel, past):
    N = q.shape[1]
    T = N // nb
    nblk, ncp = ov.shape
    n_buf = win_cache_t.shape[3]
    page = cache_nsa_t.shape[3]
    cpp = page // NSA_CMP_STRIDE
    assert n_pages % 2 == 0 and cpp == 8 and ncp == n_pages * cpp
    n_win = min(NSA_WINDOW, n_buf + T)
    win_new_t = jnp.swapaxes(win_new.reshape(nb, T, 128), 1, 2)
    n_blk_past = n_pages * page // NSA_SEL_BLOCK
    key_blk = np.arange(n_pages * page)[None, :] // NSA_SEL_BLOCK
    onehot = jnp.asarray((key_blk == np.arange(n_blk_past)[:, None]).astype(np.float32), dtype=BF16)
    row = np.arange(page)
    perm = jnp.asarray((np.arange(page)[None, :] == ((row % cpp) * NSA_CMP_STRIDE + row // cpp)[:, None])
                       .astype(np.float32), dtype=BF16)
    const = lambda a: pl.BlockSpec(a.shape, lambda b, pt: (0,) * a.ndim)
    grid_spec = pltpu.PrefetchScalarGridSpec(
        num_scalar_prefetch=1,
        grid=(nb,),
        in_specs=[pl.BlockSpec(memory_space=pl.ANY),
                  pl.BlockSpec((4, T, NSA_HD), lambda b, pt: (0, b, 0)),
                  pl.BlockSpec((T, 128), lambda b, pt: (b, 0)),
                  const(w4), const(perm), const(pe), const(ov), const(onehot),
                  pl.BlockSpec((T, 256), lambda b, pt: (b, 0)),
                  pl.BlockSpec((None, None, 128, n_buf), lambda b, pt: (layer, b, 0, 0)),
                  pl.BlockSpec((T, 128), lambda b, pt: (b, 0)),
                  pl.BlockSpec((None, 128, T), lambda b, pt: (b, 0, 0))],
        out_specs=[pl.BlockSpec((T, 256), lambda b, pt: (b, 0)),
                   pl.BlockSpec((None, 128, n_win), lambda b, pt: (b, 0, 0))],
        scratch_shapes=[pltpu.VMEM((2, n_pages, 4 * NSA_HD, page), F32), pltpu.SemaphoreType.DMA((2,)),
                        pltpu.VMEM((n_pages // 2, page, 256), F32)],
    )
    return pl.pallas_call(
        functools.partial(_nsa_sample_kernel, layer=layer, n_pages=n_pages, nc=nc, n_sel=n_sel, past=past),
        grid_spec=grid_spec,
        out_shape=[jax.ShapeDtypeStruct((N, 256), F32), jax.ShapeDtypeStruct((nb, 128, n_win), F32)],
        compiler_params=_cparams(("arbitrary",)),
        name="nsa_sample",
    )(pt_flat, cache_nsa_t, q, gates, w4, perm, pe, ov, onehot, new4, win_cache_t, win_new, win_new_t)


def _block_diag(w):
    n, a, b = w.shape
    out = jnp.zeros((n * a, n * b), w.dtype)
    for j in range(n):
        out = out.at[j * a:(j + 1) * a, j * b:(j + 1) * b].set(w[j])
    return out


def _rot_cols(w):
    half = w.shape[-1] // 2
    return jnp.concatenate([-w[..., half:], w[..., :half]], axis=-1)


def _layer_params(l, W):
    idx = np.cumsum(IN_SPLITS)[:-1].tolist()
    lru_x, lru_g, conf_u, cq, ckv, kr, nq, nkv, ng = jnp.split(W['w_in'][l], idx, axis=-1)
    pad = jnp.zeros((D_MODEL, D_INP - C_MISC - 2 * MLA_ROPE - 3 * NSA_HEADS), F32)
    w_in = jnp.concatenate([lru_x, lru_g, conf_u, cq, ckv, nq, nkv, kr, _rot_cols(kr), ng, pad], axis=-1)
    wuq = W['mla_wuq'][l].reshape(MLA_Q_LORA, MLA_HEADS, MLA_NOPE + MLA_ROPE)
    wuq_rope = wuq[:, :, MLA_NOPE:]
    wuq = jnp.concatenate([wuq[:, :, :MLA_NOPE].reshape(MLA_Q_LORA, -1), wuq_rope.reshape(MLA_Q_LORA, -1),
                           _rot_cols(wuq_rope).reshape(MLA_Q_LORA, -1)], axis=-1)
    wuk = _block_diag(jnp.transpose(W['mla_wuk'][l], (1, 2, 0)))
    wuv = jnp.transpose(W['mla_wuv'][l], (1, 0, 2))
    wk = W['nsa_wc_k'][l].reshape(NSA_CMP_BLOCK, NSA_HD, NSA_HD)
    wv = W['nsa_wc_v'][l].reshape(NSA_CMP_BLOCK, NSA_HD, NSA_HD)
    z = jnp.zeros((NSA_CMP_STRIDE, NSA_HD, NSA_HD), F32)
    w4 = jnp.concatenate([jnp.concatenate([wk[:16], z, wk[16:], z], axis=2),
                          jnp.concatenate([z, wv[:16], z, wv[16:]], axis=2)], axis=1)
    pe = jnp.concatenate([W['nsa_pe_k'][l], W['nsa_pe_v'][l]], axis=-1)
    pe2 = jnp.concatenate([pe[:16, None], pe[16:, None], jnp.zeros((16, 6, 128), F32)], axis=1)
    r1 = lambda a: a.reshape(1, -1)
    return dict(
        w_in=w_in.astype(BF16), g_pre_mix=r1(W['g_pre_mix'][l]), g_post_mix=r1(W['g_post_mix'][l]),
        g_pre_ffn=r1(W['g_pre_ffn'][l]), g_post_ffn=r1(W['g_post_ffn'][l]),
        mla_gq=r1(W['mla_gq'][l]), mla_gkv=r1(W['mla_gkv'][l]),
        wuq=wuq.astype(BF16), wuk=wuk.astype(BF16), wuv=wuv.astype(BF16),
        wuvt=jnp.transpose(wuv, (0, 2, 1)).astype(BF16),
        w4=w4.astype(BF16), pe2=pe2,
        lru_conv_w=W['lru_conv_w'][l], lru_conv_b=r1(W['lru_conv_b'][l]),
        lru_wa_bd=_block_diag(W['lru_wa'][l]).astype(BF16), lru_ba=r1(W['lru_ba'][l]),
        lru_wx_bd=_block_diag(W['lru_wx'][l]).astype(BF16), lru_bx=r1(W['lru_bx'][l]),
        lru_lam=r1(W['lru_lam'][l]),
        conf_dw_w=W['conf_dw_w'][l], conf_dw_b=r1(W['conf_dw_b'][l]),
        conf_ln_g=r1(W['conf_ln_g'][l]), conf_ln_b=r1(W['conf_ln_b'][l]),
        w_out=W['w_out'][l].astype(BF16), w_ff1=W['w_ff1'][l].astype(BF16), w_ff2=W['w_ff2'][l].astype(BF16),
    )


def _rope_tables(pos):
    half = MLA_ROPE // 2
    freq = jnp.power(ROPE_THETA, -jnp.arange(half, dtype=F32) / half)
    ang = pos.astype(F32)[:, None] * freq
    cosv = jnp.tile(jnp.cos(ang), (1, 2 * MLA_HEADS))
    sinv = jnp.tile(jnp.sin(ang), (1, 2 * MLA_HEADS))
    return cosv, sinv


def _overlap_t(nblk_pad, ncp, nc):
    n = np.arange(ncp)[None, :]
    j = np.arange(nblk_pad)[:, None]
    ov = (n * NSA_CMP_STRIDE < j * NSA_SEL_BLOCK + NSA_SEL_BLOCK) & (n * NSA_CMP_STRIDE + NSA_CMP_BLOCK > j * NSA_SEL_BLOCK)
    return jnp.asarray((ov & (n < nc)).astype(np.float32), dtype=BF16)


def _key_tiles(a, nb, t, front=0):
    k = a.reshape(nb, -1, t, a.shape[-1])
    return jnp.pad(k, ((0, 0), (front, 0), (0, 0), (0, 0)))


def _value_tiles_t(a, nb, t, front=0):
    v = jnp.swapaxes(_key_tiles(a, nb, t, front), 2, 3)
    ones = jnp.ones(v.shape[:2] + (1, t), v.dtype)
    return jnp.concatenate([v, ones, jnp.zeros(v.shape[:2] + (VSUM_ROWS - 1, t), v.dtype)], axis=2)


def kernel(x_prompt, x_sample, c_prompt, c_sample, state_lru_h, state_lru_conv, state_conv, cache_mla, cache_nsa,
           cache_nsa_win, page_table, w_mod, b_mod, g_pre_mix, g_post_mix, g_pre_ffn, g_post_ffn, w_in, lru_conv_w,
           lru_conv_b, lru_wa, lru_ba, lru_wx, lru_bx, lru_lam, conf_dw_w, conf_dw_b, conf_ln_g, conf_ln_b, mla_gq,
           mla_wuq, mla_gkv, mla_wuk, mla_wuv, nsa_pe_k, nsa_wc_k, nsa_pe_v, nsa_wc_v, w_out, w_ff1, w_ff2):
    W = dict(w_in=w_in, g_pre_mix=g_pre_mix, g_post_mix=g_post_mix, g_pre_ffn=g_pre_ffn, g_post_ffn=g_post_ffn,
             lru_conv_w=lru_conv_w, lru_conv_b=lru_conv_b, lru_wa=lru_wa, lru_ba=lru_ba, lru_wx=lru_wx, lru_bx=lru_bx,
             lru_lam=lru_lam, conf_dw_w=conf_dw_w, conf_dw_b=conf_dw_b, conf_ln_g=conf_ln_g, conf_ln_b=conf_ln_b,
             mla_gq=mla_gq, mla_wuq=mla_wuq, mla_gkv=mla_gkv, mla_wuk=mla_wuk, mla_wuv=mla_wuv,
             nsa_pe_k=nsa_pe_k, nsa_wc_k=nsa_wc_k, nsa_pe_v=nsa_pe_v, nsa_wc_v=nsa_wc_v,
             w_out=w_out, w_ff1=w_ff1, w_ff2=w_ff2)
    L = w_in.shape[0]
    nbp, S, D = x_prompt.shape
    nbs, T, _ = x_sample.shape
    n_pages = page_table.shape[1]
    page = cache_mla.shape[2]
    past = n_pages * page
    Np, Ns = nbp * S, nbs * T
    tq = min(Q_TILE, S)
    tk = min(K_TILE, S)
    tm_p = min(256, Np)
    tm_s = min(256, Ns)
    tc = min(512, S)

    nc_p = S // NSA_CMP_STRIDE - 1
    nblk_p = -(-S // NSA_SEL_BLOCK)
    ov_p = _overlap_t(nblk_p, S // NSA_CMP_STRIDE, nc_p)
    assert NSA_WINDOW % tq == 0 and S % tk == 0 and tk % tq == 0
    blk_of_pos = jnp.arange(S, dtype=jnp.int32)[:, None] // NSA_SEL_BLOCK
    blk_onehot_p = jnp.tile((blk_of_pos == jnp.arange(nblk_p, dtype=jnp.int32)[None, :]).astype(BF16), (nbp, 1))
    len_s = past + T
    assert len_s // NSA_CMP_STRIDE == past // NSA_CMP_STRIDE and past % NSA_SEL_BLOCK == 0 and T <= NSA_SEL_BLOCK
    nc_s = len_s // NSA_CMP_STRIDE - 1
    nblk_s = -(-len_s // NSA_SEL_BLOCK)
    nblk_s_pad = -(-nblk_s // 128) * 128
    ov_s = _overlap_t(nblk_s_pad, past // NSA_CMP_STRIDE, nc_s)

    rows = nbp + nbs
    rows_pad = -(-rows // 8) * 8
    c_all = jnp.concatenate([c_prompt, c_sample, jnp.zeros((rows_pad - rows, D), F32)], axis=0)
    mod = _modulation(c_all, w_mod, b_mod)

    params = [_layer_params(l, W) for l in range(L)]
    pe_all = _pe_term(jnp.stack([p['pe2'] for p in params]), jnp.stack([p['w4'] for p in params]))

    cos_p, sin_p = _rope_tables(jnp.arange(S, dtype=jnp.int32))
    cos_s, sin_s = _rope_tables(past + jnp.arange(T, dtype=jnp.int32))
    cos_s = jnp.tile(cos_s, (nbs, 1))
    sin_s = jnp.tile(sin_s, (nbs, 1))
    pt_flat = page_table.reshape(-1).astype(jnp.int32)
    cache_mla_t = jnp.swapaxes(cache_mla, 2, 3)
    cache_nsa_t = jnp.transpose(cache_nsa, (0, 1, 3, 4, 2)).reshape(L, cache_nsa.shape[1], 4 * NSA_HD, page)
    win_cache_t = jnp.transpose(cache_nsa_win, (0, 1, 3, 4, 2)).reshape(L, nbs, 2 * NSA_HD, cache_nsa_win.shape[2])

    xp = x_prompt.reshape(Np, D)
    xs = x_sample.reshape(Ns, D)
    outs_p, outs_s = [], []
    for l in range(L):
        P = params[l]
        pe = pe_all[l]
        mp = [m.reshape(nbp, 1, D) for m in jnp.split(mod[l, :nbp], 6, axis=-1)]
        ms = [jnp.repeat(m, T, axis=0) for m in jnp.split(mod[l, nbp:rows], 6, axis=-1)]

        lru, glu, qm, row, nq, nsa4, win, gates = _in_proj(
            xp, mp[0], mp[1], P['g_pre_mix'], P['w_in'], P['mla_gq'], P['wuq'], P['wuk'], P['mla_gkv'],
            cos_p, sin_p, per_token=False, tm=tm_p)
        a_out, b_out, h_last = _seq_prompt(lru.reshape(nbp, S, 512), glu.reshape(nbp, S, 256), P, tc=tc)
        rowb = row.astype(BF16)
        k3 = _key_tiles(rowb, nbp, tk)
        vt3 = _value_tiles_t(rowb[:, :MLA_KV_LORA], nbp, tk)
        Pp = _cmp_prompt(nsa4, P['w4'], nbp)
        nsab = nsa4.astype(BF16)
        ks3 = _key_tiles(jnp.concatenate([blk_onehot_p, nsab[:, 128:192]], axis=1), nbp, tk)
        vst3 = _value_tiles_t(nsab[:, 192:256], nbp, tk)
        winb = win.astype(BF16)
        kw3 = _key_tiles(winb[:, 0:64], nbp, tq, front=NSA_WINDOW // tq)
        vwt3 = _value_tiles_t(winb[:, 64:128], nbp, tq, front=NSA_WINDOW // tq)
        c_out, d_out = _attn_prompt(qm, k3, vt3, P['wuvt'], nq, gates, Pp, pe, ov_p, ks3, vst3, kw3, vwt3, tq=tq,
                                    nc=nc_p, n_sel=min(NSA_N_SEL, nblk_p))
        xp = _post(xp, a_out.reshape(Np, 256), b_out.reshape(Np, 256), c_out, d_out, mp[2], mp[3], mp[4], mp[5],
                   P['w_out'], P['g_post_mix'], P['g_pre_ffn'], P['g_post_ffn'], P['w_ff1'], P['w_ff2'],
                   per_token=False, tm=min(2 * tm_p, S))
        n_win = min(NSA_WINDOW, S)
        outs_p.append((h_last[:, 0], lru.reshape(nbp, S, 512)[:, S - (LRU_CONV - 1):, :D_LRU],
                       glu.reshape(nbp, S, 256)[:, S - (CONF_KERNEL - 1):],
                       row.reshape(nbp, S, MLA_CACHE_W), nsa4.reshape(nbp, S, 4, NSA_HD),
                       win.reshape(nbp, S, 2, NSA_HD)[:, S - n_win:]))

        lru, glu, qm, row, nq, nsa4, win, gates = _in_proj(
            xs, ms[0], ms[1], P['g_pre_mix'], P['w_in'], P['mla_gq'], P['wuq'], P['wuk'], P['mla_gkv'],
            cos_s, sin_s, per_token=True, tm=tm_s)
        tmaj = lambda a: jnp.swapaxes(a, 0, 1)
        a_t, b_t, h_last, ncs, ngs = _seq_sample(
            tmaj(lru.reshape(nbs, T, 512)), tmaj(glu.reshape(nbs, T, 256)), tmaj(state_lru_conv[l]),
            tmaj(state_conv[l]), state_lru_h[l], P)
        c_out = _mla_sample(cache_mla_t, pt_flat, qm, row, P['wuv'], l, nbs, n_pages)
        d_out, win_t = _nsa_sample(cache_nsa_t, pt_flat, nq, gates, P['w4'], pe, ov_s, nsa4, win_cache_t, win, l, nbs,
                                   n_pages, nc=nc_s, n_sel=min(NSA_N_SEL, nblk_s), past=past)
        xs = _post(xs, tmaj(a_t).reshape(Ns, 256), tmaj(b_t).reshape(Ns, 256), c_out, d_out, ms[2], ms[3], ms[4],
                   ms[5], P['w_out'], P['g_post_mix'], P['g_pre_ffn'], P['g_post_ffn'], P['w_ff1'], P['w_ff2'],
                   per_token=True, tm=tm_s)
        win_s = jnp.transpose(win_t.reshape(nbs, 2, NSA_HD, win_t.shape[2]), (0, 3, 1, 2))
        outs_s.append((h_last, tmaj(ncs), tmaj(ngs), row.reshape(nbs, T, MLA_CACHE_W),
                       nsa4.reshape(nbs, T, 4, NSA_HD), win_s))

    stack = lambda outs, i: jnp.stack([o[i] for o in outs])
    return (xp.reshape(nbp, S, D), xs.reshape(nbs, T, D),
            stack(outs_p, 0), stack(outs_s, 0), stack(outs_p, 1), stack(outs_s, 1), stack(outs_p, 2), stack(outs_s, 2),
            stack(outs_p, 3), stack(outs_s, 3), stack(outs_p, 4), stack(outs_s, 4), stack(outs_p, 5), stack(outs_s, 5))
```
